```python
import jax, jax.numpy as jnp
from jax import lax
import numpy as np

D_MODEL = 2048
BATCH = 1
SEQ = 8192
DEPTH = 1

CTX_LEN = 256
GRID_W = 64
HEAD_DIM = 128
N_Q_HEADS = 16
N_KV_HEADS = 4
Q_PER_KV = N_Q_HEADS // N_KV_HEADS
Q_BLOCK = 128
AXIS_DIM = HEAD_DIM // 2
ROPE_THETA = 10000.0
ATTN_SCALE = HEAD_DIM ** -0.5
GMLP_GROUPS = 16
GMLP_WIDTH = 2048
GMLP_GROUP_DIM = GMLP_WIDTH // GMLP_GROUPS
CHUNK = 128
D_FF = 5632
MACARON_WEIGHT = 0.5
N_MOD = 9
EPS = 1e-6

Q_W = N_Q_HEADS * HEAD_DIM
KV_W = N_KV_HEADS * HEAD_DIM
Q_END = Q_W
K_END = Q_END + KV_W
V_END = K_END + KV_W
GV_END = V_END + 2 * GMLP_WIDTH
IN_W = GV_END + 2 * D_MODEL

kernel_name = "hybrid_gqa_gmlp_macaron_dit_layer"


def rmsnorm(x, w):
    xf = x.astype(jnp.float32)
    y = xf * lax.rsqrt(jnp.mean(xf * xf, axis=-1, keepdims=True) + EPS)
    return (y * w.astype(jnp.float32)).astype(x.dtype)


def layernorm(x, w, b):
    xf = x.astype(jnp.float32)
    mu = jnp.mean(xf, axis=-1, keepdims=True)
    xc = xf - mu
    y = xc * lax.rsqrt(jnp.mean(xc * xc, axis=-1, keepdims=True) + EPS)
    return (y * w.astype(jnp.float32) + b.astype(jnp.float32)).astype(x.dtype)


def modulate(h, shift, scale):
    return h * (1 + scale) + shift


def axial_rope(n_rows):
    row = jnp.broadcast_to(jnp.arange(n_rows, dtype=jnp.float32)[:, None], (n_rows, GRID_W)).reshape(-1)
    col = jnp.broadcast_to(jnp.arange(GRID_W, dtype=jnp.float32)[None, :], (n_rows, GRID_W)).reshape(-1)
    inv_freq = ROPE_THETA ** (-jnp.arange(0, AXIS_DIM, 2, dtype=jnp.float32) / AXIS_DIM)
    ang = jnp.concatenate([row[:, None] * inv_freq, col[:, None] * inv_freq], axis=-1)
    return jnp.cos(ang), jnp.sin(ang)


def apply_rope(x, cos, sin):
    B, S, H, Dh = x.shape
    xr = x.astype(jnp.float32).reshape(B, S, H, Dh // 2, 2)
    x1, x2 = xr[..., 0], xr[..., 1]
    cs, sn = cos[None, :, None, :], sin[None, :, None, :]
    out = jnp.stack([x1 * cs - x2 * sn, x1 * sn + x2 * cs], axis=-1)
    return out.reshape(B, S, H, Dh).astype(x.dtype)


def heads_norm(z, n_heads, gain):
    return rmsnorm(z.reshape(z.shape[0], z.shape[1], n_heads, HEAD_DIM), gain)


def gqa_attend(qi, k_all, v_all):
    s = jnp.einsum('bqkgd,bskd->bkgqs', qi, k_all, preferred_element_type=jnp.float32) * ATTN_SCALE
    p = jax.nn.softmax(s, axis=-1).astype(v_all.dtype)
    return jnp.einsum('bkgqs,bskd->bqkgd', p, v_all)


def latent_attention(q, k, v, k_ctx, v_ctx):
    B, S = q.shape[:2]
    k_all = jnp.concatenate([k_ctx, k], axis=1)
    v_all = jnp.concatenate([v_ctx, v], axis=1)
    qb = q.reshape(B, S // Q_BLOCK, Q_BLOCK, N_KV_HEADS, Q_PER_KV, HEAD_DIM).swapaxes(0, 1)
    o = lax.map(lambda qi: gqa_attend(qi, k_all, v_all), qb)
    return o.swapaxes(0, 1).reshape(B, S, Q_W)


def context_attention(q, k, v):
    B, C = q.shape[:2]
    o = gqa_attend(q.reshape(B, C, N_KV_HEADS, Q_PER_KV, HEAD_DIM), k, v)
    return o.reshape(B, C, Q_W)


def gmlp_branch(z_uv, ln_w, ln_b, w_s, b_s):
    B, N, _ = z_uv.shape
    z = jax.nn.gelu(z_uv, approximate=False)
    u, v = z[..., :GMLP_WIDTH], z[..., GMLP_WIDTH:]
    vn = layernorm(v, ln_w, ln_b).reshape(B, N // CHUNK, CHUNK, GMLP_GROUPS, GMLP_GROUP_DIM)
    mixed = jnp.einsum('gpq,bcqgd->bcpgd', w_s, vn) + b_s.T[:, :, None]
    return u * mixed.reshape(B, N, GMLP_WIDTH)


def merge_branches(attn, gm, gate_logits, b_gate_l, w_ba, w_bg, w_o):
    g = jax.nn.sigmoid(gate_logits.reshape(*gate_logits.shape[:-1], 2, D_MODEL) + b_gate_l)
    return (g[..., 0, :] * (attn @ w_ba) + g[..., 1, :] * (gm @ w_bg)) @ w_o


def ffn_sublayer(x, shift, scale, gate, norm_w, w_in, w_out):
    h = modulate(rmsnorm(x, norm_w), shift, scale)
    a, b = jnp.split(h @ w_in, 2, axis=-1)
    return x + MACARON_WEIGHT * gate * ((jax.nn.silu(a) * b) @ w_out)


def _normal(key, shape, scale):
    return jax.random.normal(key, shape, jnp.float32) * scale


def setup_inputs(seed: int = 0) -> dict:
    key = jax.random.key(seed)
    ks = jax.random.split(key, 24)
    L, D, F = DEPTH, D_MODEL, D_FF
    return {
        "x": _normal(ks[0], (BATCH, SEQ, D), 1.0),
        "c": _normal(ks[1], (BATCH, D), 1.0),
        "ctx": _normal(ks[2], (BATCH, CTX_LEN, D), 1.0),
        "c_ctx": _normal(ks[3], (D,), 1.0),
        "w_mod": _normal(ks[4], (L, D, N_MOD * D), 0.5 * D ** -0.5),
        "b_mod": _normal(ks[5], (L, N_MOD * D), 0.02),
        "norm_w": 1.0 + _normal(ks[6], (L, 3, D), 0.05),
        "w_ffn1_in": _normal(ks[7], (L, D, 2 * F), D ** -0.5),
        "w_ffn1_out": _normal(ks[8], (L, F, D), F ** -0.5),
        "w_ffn2_in": _normal(ks[9], (L, D, 2 * F), D ** -0.5),
        "w_ffn2_out": _normal(ks[10], (L, F, D), F ** -0.5),
        "w_in": _normal(ks[11], (L, D, IN_W), D ** -0.5),
        "b_gate": _normal(ks[12], (L, 2, D), 0.1),
        "q_norm_w": 1.0 + _normal(ks[13], (L, HEAD_DIM), 0.05),
        "k_norm_w": 1.0 + _normal(ks[14], (L, HEAD_DIM), 0.05),
        "gmlp_ln_w": 1.0 + _normal(ks[15], (L, GMLP_WIDTH), 0.05),
        "gmlp_ln_b": _normal(ks[16], (L, GMLP_WIDTH), 0.02),
        "w_spatial": _normal(ks[17], (L, GMLP_GROUPS, CHUNK, CHUNK), 0.5 * CHUNK ** -0.5),
        "b_spatial": 1.0 + _normal(ks[18], (L, GMLP_GROUPS, CHUNK), 0.1),
        "w_branch_attn": _normal(ks[19], (L, Q_W, D), Q_W ** -0.5),
        "w_branch_gmlp": _normal(ks[20], (L, GMLP_WIDTH, D), GMLP_WIDTH ** -0.5),
        "w_out": _normal(ks[21], (L, D, D), D ** -0.5),
        "final_norm_w": 1.0 + _normal(ks[22], (D,), 0.05),
    }


def reference(x, c, ctx, c_ctx, w_mod, b_mod, norm_w, w_ffn1_in, w_ffn1_out, w_ffn2_in, w_ffn2_out,
              w_in, b_gate, q_norm_w, k_norm_w, gmlp_ln_w, gmlp_ln_b, w_spatial, b_spatial,
              w_branch_attn, w_branch_gmlp, w_out, final_norm_w):
    B, S, D = x.shape
    rows = S // GRID_W
    cos, sin = axial_rope(rows)
    sc = jax.nn.silu(c)
    scc = jax.nn.silu(c_ctx)
    for l in range(DEPTH):
        mx = (sc @ w_mod[l] + b_mod[l]).reshape(B, N_MOD, 1, D)
        mc = (scc @ w_mod[l] + b_mod[l]).reshape(1, N_MOD, 1, D)

        x = ffn_sublayer(x, mx[:, 0], mx[:, 1], mx[:, 2], norm_w[l, 0], w_ffn1_in[l], w_ffn1_out[l])
        ctx = ffn_sublayer(ctx, mc[:, 0], mc[:, 1], mc[:, 2], norm_w[l, 0], w_ffn1_in[l], w_ffn1_out[l])

        hx = modulate(rmsnorm(x, norm_w[l, 1]), mx[:, 3], mx[:, 4])
        hc = modulate(rmsnorm(ctx, norm_w[l, 1]), mc[:, 3], mc[:, 4])
        zx = hx @ w_in[l]
        qx = apply_rope(heads_norm(zx[..., :Q_END], N_Q_HEADS, q_norm_w[l]), cos, sin)
        kx = apply_rope(heads_norm(zx[..., Q_END:K_END], N_KV_HEADS, k_norm_w[l]), cos, sin)
        vx = zx[..., K_END:V_END].reshape(B, S, N_KV_HEADS, HEAD_DIM)

        zc_kv = hc @ w_in[l][:, Q_END:V_END]
        kc = heads_norm(zc_kv[..., :KV_W], N_KV_HEADS, k_norm_w[l])
        vc = zc_kv[..., KV_W:].reshape(B, hc.shape[1], N_KV_HEADS, HEAD_DIM)

        attn_x = latent_attention(qx, kx, vx, kc, vc)
        gm_x = gmlp_branch(zx[..., V_END:GV_END], gmlp_ln_w[l], gmlp_ln_b[l], w_spatial[l], b_spatial[l])
        y = merge_branches(attn_x, gm_x, zx[..., GV_END:], b_gate[l],
                           w_branch_attn[l], w_branch_gmlp[l], w_out[l])

        if l < DEPTH - 1:
            qc = heads_norm(hc @ w_in[l][:, :Q_END], N_Q_HEADS, q_norm_w[l])
            zc_rest = hc @ w_in[l][:, V_END:]
            attn_c = context_attention(qc, kc, vc)
            gm_c = gmlp_branch(zc_rest[..., :2 * GMLP_WIDTH], gmlp_ln_w[l], gmlp_ln_b[l],
                               w_spatial[l], b_spatial[l])
            yc = merge_branches(attn_c, gm_c, zc_rest[..., 2 * GMLP_WIDTH:], b_gate[l],
                                w_branch_attn[l], w_branch_gmlp[l], w_out[l])
            ctx = ctx + mc[:, 5] * yc
            ctx = ffn_sublayer(ctx, mc[:, 6], mc[:, 7], mc[:, 8], norm_w[l, 2], w_ffn2_in[l], w_ffn2_out[l])

        x = x + mx[:, 5] * y
        x = ffn_sublayer(x, mx[:, 6], mx[:, 7], mx[:, 8], norm_w[l, 2], w_ffn2_in[l], w_ffn2_out[l])
    return rmsnorm(x, final_norm_w)
```

```python
import functools

import jax
import jax.numpy as jnp
from jax import lax
from jax.experimental import pallas as pl
from jax.experimental.pallas import tpu as pltpu

D_MODEL = 2048
SEQ = 8192
CTX_LEN = 256
GRID_W = 64
HEAD_DIM = 128
N_Q_HEADS = 16
N_KV_HEADS = 4
Q_PER_KV = N_Q_HEADS // N_KV_HEADS
ROPE_THETA = 10000.0
ATTN_SCALE = HEAD_DIM ** -0.5
GMLP_GROUPS = 16
GMLP_WIDTH = 2048
GROUP_DIM = GMLP_WIDTH // GMLP_GROUPS
CHUNK = 128
D_FF = 5632
MACARON_WEIGHT = 0.5
N_MOD = 9
EPS = 1e-6

Q_W = N_Q_HEADS * HEAD_DIM
KV_W = N_KV_HEADS * HEAD_DIM
K_OFF = Q_W
V_OFF = K_OFF + KV_W
U_OFF = V_OFF + KV_W
GV_OFF = U_OFF + GMLP_WIDTH
GATE_OFF = GV_OFF + GMLP_WIDTH
IN_W = GATE_OFF + 2 * D_MODEL

V7X_VMEM_BYTES = 64 * 1024 * 1024
VMEM_CAP_BYTES = V7X_VMEM_BYTES - 6 * 1024 * 1024

MOD_ROWS = 16
MOD_TN = 1024
FFN_TM = 512
FFN_TF = 512
QKV_TM = 512
QKV_TN = 512
ATTN_TQ = 128
ATTN_TK = 512
GMLP_TM = 512
GMLP_TN = 512
MERGE_TM = 512
MERGE_TN = 256

F32 = jnp.float32
BF16 = jnp.bfloat16


def _vmem_limit(pipelined_bytes, resident_bytes):
    return int(min(2 * pipelined_bytes + resident_bytes, VMEM_CAP_BYTES))


def _dot(a, b):
    return jnp.dot(a, b, preferred_element_type=F32)


def _rmsnorm(x, w):
    return x * lax.rsqrt(jnp.mean(x * x, axis=-1, keepdims=True) + EPS) * w


def _gelu_exact(x):
    return 0.5 * x * (1 + lax.erf(x * (2.0 ** -0.5)))


def _mod_kernel(c_ref, w_ref, b_ref, o_ref):
    sc = jax.nn.silu(c_ref[...])
    o_ref[...] = _dot(sc.astype(BF16), w_ref[...].astype(BF16)) + b_ref[...]


def _mod_call(c_rows, w_mod, b_mod):
    n = w_mod.shape[1]
    return pl.pallas_call(
        _mod_kernel,
        grid=(n // MOD_TN,),
        in_specs=[
            pl.BlockSpec((MOD_ROWS, D_MODEL), lambda j: (0, 0)),
            pl.BlockSpec((D_MODEL, MOD_TN), lambda j: (0, j)),
            pl.BlockSpec((1, MOD_TN), lambda j: (0, j)),
        ],
        out_specs=pl.BlockSpec((MOD_ROWS, MOD_TN), lambda j: (0, j)),
        out_shape=jax.ShapeDtypeStruct((MOD_ROWS, n), F32),
        compiler_params=pltpu.CompilerParams(
            dimension_semantics=("arbitrary",),
            vmem_limit_bytes=_vmem_limit(D_MODEL * MOD_TN * 4, D_MODEL * MOD_TN * 2 + (4 << 20))),
        name="mod",
    )(c_rows, w_mod, b_mod)


def _ffn_kernel(x_ref, mod_ref, nw_ref, pnw_ref, wa_ref, wb_ref, wo_ref, *refs, mod_row, post_mod_row):
    if post_mod_row is None:
        (o_ref, h_ref, acc_ref) = refs
    else:
        (y_ref, o_ref, h_ref, acc_ref) = refs
    j = pl.program_id(1)

    @pl.when(j == 0)
    def _():
        shift = mod_ref[mod_row:mod_row + 1, :]
        scale = mod_ref[mod_row + 1:mod_row + 2, :]
        h = _rmsnorm(x_ref[...], nw_ref[...]) * (1 + scale) + shift
        h_ref[...] = h.astype(BF16)
        acc_ref[...] = jnp.zeros_like(acc_ref)

    h = h_ref[...]
    a = _dot(h, wa_ref[...])
    b = _dot(h, wb_ref[...])
    act = jax.nn.silu(a) * b
    acc_ref[...] += _dot(act.astype(BF16), wo_ref[...])

    @pl.when(j == pl.num_programs(1) - 1)
    def _():
        gate = mod_ref[mod_row + 2:mod_row + 3, :]
        y = x_ref[...] + MACARON_WEIGHT * gate * acc_ref[...]
        yn = _rmsnorm(y, pnw_ref[...])
        if post_mod_row is None:
            o_ref[...] = yn
        else:
            y_ref[...] = y
            pshift = mod_ref[post_mod_row:post_mod_row + 1, :]
            pscale = mod_ref[post_mod_row + 1:post_mod_row + 2, :]
            o_ref[...] = (yn * (1 + pscale) + pshift).astype(BF16)


def _ffn_call(x, mods, nw, pnw, w_in, w_out, *, mod_row, post_mod_row, tm):
    m = x.shape[0]
    n_f = D_FF // FFN_TF
    row_f32 = pl.BlockSpec((tm, D_MODEL), lambda i, j: (i, 0))
    vec = pl.BlockSpec((1, D_MODEL), lambda i, j: (0, 0))
    if post_mod_row is None:
        out_shape = jax.ShapeDtypeStruct((m, D_MODEL), F32)
        out_specs = row_f32
        out_bytes = tm * D_MODEL * 4
    else:
        out_shape = (jax.ShapeDtypeStruct((m, D_MODEL), F32), jax.ShapeDtypeStruct((m, D_MODEL), BF16))
        out_specs = (row_f32, pl.BlockSpec((tm, D_MODEL), lambda i, j: (i, 0)))
        out_bytes = tm * D_MODEL * 6
    pipelined = tm * D_MODEL * 4 + out_bytes + 3 * D_MODEL * FFN_TF * 2
    resident = tm * D_MODEL * 6 + 6 * tm * FFN_TF * 4 + (2 << 20)
    return pl.pallas_call(
        functools.partial(_ffn_kernel, mod_row=mod_row, post_mod_row=post_mod_row),
        grid=(m // tm, n_f),
        in_specs=[
            row_f32,
            pl.BlockSpec((N_MOD, D_MODEL), lambda i, j: (0, 0)),
            vec,
            vec,
            pl.BlockSpec((D_MODEL, FFN_TF), lambda i, j: (0, j)),
            pl.BlockSpec((D_MODEL, FFN_TF), lambda i, j: (0, j + n_f)),
            pl.BlockSpec((FFN_TF, D_MODEL), lambda i, j: (j, 0)),
        ],
        out_specs=out_specs,
        out_shape=out_shape,
        scratch_shapes=[pltpu.VMEM((tm, D_MODEL), BF16), pltpu.VMEM((tm, D_MODEL), F32)],
        compiler_params=pltpu.CompilerParams(
            dimension_semantics=("parallel", "arbitrary"),
            vmem_limit_bytes=_vmem_limit(pipelined, resident)),
        name="ffn",
    )(x, mods, nw, pnw, w_in, w_in, w_out)


def _qkv_kernel(h_ref, w_ref, gq_ref, gk_ref, cos_ref, sa_ref, sb_ref, o_ref, *, first_tile):
    j = pl.program_id(1) + first_tile
    z = _dot(h_ref[...], w_ref[...])
    n_q_tiles = Q_W // QKV_TN
    n_k_tiles = KV_W // QKV_TN

    def norm_rope(gain, post_scale):
        cos, sa, sb = cos_ref[...], sa_ref[...], sb_ref[...]
        for hh in range(QKV_TN // HEAD_DIM):
            cols = slice(hh * HEAD_DIM, (hh + 1) * HEAD_DIM)
            y = _rmsnorm(z[:, cols], gain)
            y = y * cos + pltpu.roll(y, HEAD_DIM - 1, 1) * sa + pltpu.roll(y, 1, 1) * sb
            if post_scale is not None:
                y = y * post_scale
            o_ref[:, cols] = y.astype(o_ref.dtype)

    @pl.when(j < n_q_tiles)
    def _():
        norm_rope(gq_ref[...], ATTN_SCALE)

    @pl.when(jnp.logical_and(j >= n_q_tiles, j < n_q_tiles + n_k_tiles))
    def _():
        norm_rope(gk_ref[...], None)

    @pl.when(j >= n_q_tiles + n_k_tiles)
    def _():
        o_ref[...] = z.astype(o_ref.dtype)


def _qkv_call(h, w_in, gq, gk, cos, sa, sb, *, first_tile, n_tiles, tm):
    m = h.shape[0]
    tab = pl.BlockSpec((tm, HEAD_DIM), lambda i, j: (i, 0))
    gain = pl.BlockSpec((1, HEAD_DIM), lambda i, j: (0, 0))
    pipelined = tm * D_MODEL * 2 + D_MODEL * QKV_TN * 2 + 3 * tm * HEAD_DIM * 4 + tm * QKV_TN * 2
    resident = 8 * tm * QKV_TN * 4
    return pl.pallas_call(
        functools.partial(_qkv_kernel, first_tile=first_tile),
        grid=(m // tm, n_tiles),
        in_specs=[
            pl.BlockSpec((tm, D_MODEL), lambda i, j: (i, 0)),
            pl.BlockSpec((D_MODEL, QKV_TN), lambda i, j: (0, j + first_tile)),
            gain, gain, tab, tab, tab,
        ],
        out_specs=pl.BlockSpec((tm, QKV_TN), lambda i, j: (i, j)),
        out_shape=jax.ShapeDtypeStruct((m, n_tiles * QKV_TN), BF16),
        compiler_params=pltpu.CompilerParams(
            dimension_semantics=("parallel", "arbitrary"),
            vmem_limit_bytes=_vmem_limit(pipelined, resident)),
        name="qkv",
    )(h, w_in, gq, gk, cos, sa, sb)


def _attn_kernel(q_ref, k_ref, v_ref, kc_ref, vc_ref, o_ref, m_ref, l_ref, acc_ref):
    tq = q_ref.shape[0]
    q = jnp.concatenate([q_ref[:, g * HEAD_DIM:(g + 1) * HEAD_DIM] for g in range(Q_PER_KV)], axis=0)

    def scores(k):
        return lax.dot_general(q, k, (((1,), (1,)), ((), ())), preferred_element_type=F32)

    s = scores(kc_ref[...])
    m0 = jnp.max(s, axis=-1, keepdims=True)
    p = jnp.exp(s - m0)
    m_ref[...] = m0
    l_ref[...] = jnp.sum(p, axis=-1, keepdims=True)
    acc_ref[...] = _dot(p.astype(BF16), vc_ref[...])

    def body(t, carry):
        start = pl.multiple_of(t * ATTN_TK, ATTN_TK)
        s = scores(k_ref[pl.ds(start, ATTN_TK), :])
        m_old = m_ref[...]
        m_new = jnp.maximum(m_old, jnp.max(s, axis=-1, keepdims=True))
        alpha = jnp.exp(m_old - m_new)
        p = jnp.exp(s - m_new)
        l_ref[...] = alpha * l_ref[...] + jnp.sum(p, axis=-1, keepdims=True)
        acc_ref[...] = alpha * acc_ref[...] + _dot(p.astype(BF16), v_ref[pl.ds(start, ATTN_TK), :])
        m_ref[...] = m_new
        return carry

    lax.fori_loop(0, k_ref.shape[0] // ATTN_TK, body, 0)
    o = acc_ref[...] / l_ref[...]
    for g in range(Q_PER_KV):
        o_ref[:, g * HEAD_DIM:(g + 1) * HEAD_DIM] = o[g * tq:(g + 1) * tq].astype(o_ref.dtype)


def _attn_call(qkv, ckv):
    s_len = qkv.shape[0]
    c_len = ckv.shape[0]
    gw = Q_PER_KV * HEAD_DIM
    rows = Q_PER_KV * ATTN_TQ
    pipelined = 2 * ATTN_TQ * gw * 2 + 2 * s_len * HEAD_DIM * 2 + 2 * c_len * HEAD_DIM * 2
    resident = rows * (2 * HEAD_DIM + HEAD_DIM) * 4 + 6 * rows * ATTN_TK * 4
    return pl.pallas_call(
        _attn_kernel,
        grid=(N_KV_HEADS, s_len // ATTN_TQ),
        in_specs=[
            pl.BlockSpec((ATTN_TQ, gw), lambda kv, i: (i, kv)),
            pl.BlockSpec((s_len, HEAD_DIM), lambda kv, i: (0, K_OFF // HEAD_DIM + kv)),
            pl.BlockSpec((s_len, HEAD_DIM), lambda kv, i: (0, V_OFF // HEAD_DIM + kv)),
            pl.BlockSpec((c_len, HEAD_DIM), lambda kv, i: (0, kv)),
            pl.BlockSpec((c_len, HEAD_DIM), lambda kv, i: (0, N_KV_HEADS + kv)),
        ],
        out_specs=pl.BlockSpec((ATTN_TQ, gw), lambda kv, i: (i, kv)),
        out_shape=jax.ShapeDtypeStruct((s_len, Q_W), BF16),
        scratch_shapes=[
            pltpu.VMEM((rows, 1), F32),
            pltpu.VMEM((rows, 1), F32),
            pltpu.VMEM((rows, HEAD_DIM), F32),
        ],
        compiler_params=pltpu.CompilerParams(
            dimension_semantics=("parallel", "parallel"),
            vmem_limit_bytes=_vmem_limit(pipelined, resident)),
        name="attn",
    )(qkv, qkv, qkv, ckv, ckv)


def _gmlp_kernel(h_ref, wu_ref, wv_ref, lnw_ref, lnb_ref, ws_ref, bs_ref, o_ref, gu_ref, gv_ref):
    j = pl.program_id(1)
    n_j = GMLP_WIDTH // GMLP_TN
    h = h_ref[...]
    gu_ref[j] = _gelu_exact(_dot(h, wu_ref[...]))
    gv_ref[j] = _gelu_exact(_dot(h, wv_ref[...]))

    @pl.when(j == n_j - 1)
    def _():
        tm = h_ref.shape[0]
        total = gv_ref[0].sum(axis=-1, keepdims=True)
        for jj in range(1, n_j):
            total = total + gv_ref[jj].sum(axis=-1, keepdims=True)
        mu = total / GMLP_WIDTH
        sq = None
        for jj in range(n_j):
            xc = gv_ref[jj] - mu
            part = (xc * xc).sum(axis=-1, keepdims=True)
            sq = part if sq is None else sq + part
        rstd = lax.rsqrt(sq / GMLP_WIDTH + EPS)
        groups_per_tile = GMLP_TN // GROUP_DIM
        for jj in range(n_j):
            tile_cols = slice(jj * GMLP_TN, (jj + 1) * GMLP_TN)
            vn = ((gv_ref[jj] - mu) * rstd * lnw_ref[:, tile_cols] + lnb_ref[:, tile_cols]).astype(BF16)
            for gg in range(groups_per_tile):
                g = jj * groups_per_tile + gg
                cols = slice(gg * GROUP_DIM, (gg + 1) * GROUP_DIM)
                bias = bs_ref[:, g:g + 1]
                for c in range(tm // CHUNK):
                    rows = slice(c * CHUNK, (c + 1) * CHUNK)
                    mixed = _dot(ws_ref[g], vn[rows, cols]) + bias
                    o_ref[rows, g * GROUP_DIM:(g + 1) * GROUP_DIM] = (
                        gu_ref[jj, rows, cols] * mixed).astype(o_ref.dtype)


def _gmlp_call(h, w_in, lnw, lnb, w_s, b_s_t, *, tm):
    m = h.shape[0]
    n_j = GMLP_WIDTH // GMLP_TN
    vec = pl.BlockSpec((1, GMLP_WIDTH), lambda i, j: (0, 0))
    pipelined = (tm * D_MODEL * 2 + 2 * D_MODEL * GMLP_TN * 2 + tm * GMLP_WIDTH * 2
                 + GMLP_GROUPS * CHUNK * CHUNK * 2 + CHUNK * 128 * 4)
    resident = 2 * tm * GMLP_WIDTH * 4 + 8 * tm * GMLP_TN * 4
    return pl.pallas_call(
        _gmlp_kernel,
        grid=(m // tm, n_j),
        in_specs=[
            pl.BlockSpec((tm, D_MODEL), lambda i, j: (i, 0)),
            pl.BlockSpec((D_MODEL, GMLP_TN), lambda i, j: (0, U_OFF // GMLP_TN + j)),
            pl.BlockSpec((D_MODEL, GMLP_TN), lambda i, j: (0, GV_OFF // GMLP_TN + j)),
            vec, vec,
            pl.BlockSpec((GMLP_GROUPS, CHUNK, CHUNK), lambda i, j: (0, 0, 0)),
            pl.BlockSpec((CHUNK, GMLP_GROUPS), lambda i, j: (0, 0)),
        ],
        out_specs=pl.BlockSpec((tm, GMLP_WIDTH), lambda i, j: (i, 0)),
        out_shape=jax.ShapeDtypeStruct((m, GMLP_WIDTH), BF16),
        scratch_shapes=[pltpu.VMEM((n_j, tm, GMLP_TN), F32), pltpu.VMEM((n_j, tm, GMLP_TN), F32)],
        compiler_params=pltpu.CompilerParams(
            dimension_semantics=("parallel", "arbitrary"),
            vmem_limit_bytes=_vmem_limit(pipelined, resident)),
        name="gmlp",
    )(h, w_in, w_in, lnw, lnb, w_s, b_s_t)


def _merge_kernel(x_ref, mod_ref, h_ref, attn_ref, gm_ref, wga_ref, wgb_ref, bg_ref, wba_ref, wbg_ref, wo_ref,
                  o_ref, acc_ref, *, gate_row):
    j = pl.program_id(1)

    @pl.when(j == 0)
    def _():
        acc_ref[...] = jnp.zeros_like(acc_ref)

    h = h_ref[...]
    ga = jax.nn.sigmoid(_dot(h, wga_ref[...]) + bg_ref[0:1, :])
    gb = jax.nn.sigmoid(_dot(h, wgb_ref[...]) + bg_ref[1:2, :])
    merged = ga * _dot(attn_ref[...], wba_ref[...]) + gb * _dot(gm_ref[...], wbg_ref[...])
    acc_ref[...] += _dot(merged.astype(BF16), wo_ref[...])

    @pl.when(j == pl.num_programs(1) - 1)
    def _():
        o_ref[...] = x_ref[...] + mod_ref[gate_row:gate_row + 1, :] * acc_ref[...]


def _merge_call(x, mods, h, attn, gm, w_in, b_gate, w_ba, w_bg, w_o, *, gate_row, tm):
    m = x.shape[0]
    n_j = D_MODEL // MERGE_TN
    row_bf = pl.BlockSpec((tm, D_MODEL), lambda i, j: (i, 0))
    row_f32 = pl.BlockSpec((tm, D_MODEL), lambda i, j: (i, 0))
    col_w = pl.BlockSpec((D_MODEL, MERGE_TN), lambda i, j: (0, j))
    pipelined = 2 * tm * D_MODEL * 4 + 3 * tm * D_MODEL * 2 + 5 * D_MODEL * MERGE_TN * 2
    resident = tm * D_MODEL * 4 + 8 * tm * MERGE_TN * 4 + (2 << 20)
    return pl.pallas_call(
        functools.partial(_merge_kernel, gate_row=gate_row),
        grid=(m // tm, n_j),
        in_specs=[
            row_f32,
            pl.BlockSpec((N_MOD, D_MODEL), lambda i, j: (0, 0)),
            row_bf, row_bf, row_bf,
            pl.BlockSpec((D_MODEL, MERGE_TN), lambda i, j: (0, GATE_OFF // MERGE_TN + j)),
            pl.BlockSpec((D_MODEL, MERGE_TN), lambda i, j: (0, (GATE_OFF + D_MODEL) // MERGE_TN + j)),
            pl.BlockSpec((2, MERGE_TN), lambda i, j: (0, j)),
            col_w, col_w,
            pl.BlockSpec((MERGE_TN, D_MODEL), lambda i, j: (j, 0)),
        ],
        out_specs=row_f32,
        out_shape=jax.ShapeDtypeStruct((m, D_MODEL), F32),
        scratch_shapes=[pltpu.VMEM((tm, D_MODEL), F32)],
        compiler_params=pltpu.CompilerParams(
            dimension_semantics=("parallel", "arbitrary"),
            vmem_limit_bytes=_vmem_limit(pipelined, resident)),
        name="merge",
    )(x, mods, h, attn, gm, w_in, w_in, b_gate, w_ba, w_bg, w_o)


def _rope_tables(n_rows):
    axis_dim = HEAD_DIM // 2
    row = jnp.broadcast_to(jnp.arange(n_rows, dtype=F32)[:, None], (n_rows, GRID_W)).reshape(-1)
    col = jnp.broadcast_to(jnp.arange(GRID_W, dtype=F32)[None, :], (n_rows, GRID_W)).reshape(-1)
    inv_freq = ROPE_THETA ** (-jnp.arange(0, axis_dim, 2, dtype=F32) / axis_dim)
    ang = jnp.concatenate([row[:, None] * inv_freq, col[:, None] * inv_freq], axis=-1)
    cos, sin = jnp.cos(ang), jnp.sin(ang)
    zero = jnp.zeros_like(sin)
    cos2 = jnp.repeat(cos, 2, axis=-1)
    sa = jnp.stack([-sin, zero], axis=-1).reshape(cos2.shape)
    sb = jnp.stack([zero, sin], axis=-1).reshape(cos2.shape)
    return cos2, sa, sb


def kernel(x, c, ctx, c_ctx, w_mod, b_mod, norm_w, w_ffn1_in, w_ffn1_out, w_ffn2_in, w_ffn2_out, w_in, b_gate,
           q_norm_w, k_norm_w, gmlp_ln_w, gmlp_ln_b, w_spatial, b_spatial, w_branch_attn, w_branch_gmlp, w_out,
           final_norm_w):
    batch, seq, d = x.shape
    assert batch == 1 and d == D_MODEL and seq == SEQ and seq % GRID_W == 0
    assert w_mod.shape[0] == 1 and ctx.shape == (1, CTX_LEN, D_MODEL) and w_in.shape[-1] == IN_W

    x0 = x[0]
    ctx0 = ctx[0]
    w1i, w1o = w_ffn1_in[0].astype(BF16), w_ffn1_out[0].astype(BF16)
    w2i, w2o = w_ffn2_in[0].astype(BF16), w_ffn2_out[0].astype(BF16)
    wi = w_in[0].astype(BF16)
    wba, wbg, wo = w_branch_attn[0].astype(BF16), w_branch_gmlp[0].astype(BF16), w_out[0].astype(BF16)
    ws = w_spatial[0].astype(BF16)
    nw = norm_w[0]

    c_rows = jnp.zeros((MOD_ROWS, D_MODEL), F32).at[0].set(c[0]).at[1].set(c_ctx)
    mods = _mod_call(c_rows, w_mod[0], b_mod).reshape(MOD_ROWS, N_MOD, D_MODEL)
    mx, mc = mods[0], mods[1]

    x1, hx = _ffn_call(x0, mx, nw[0:1], nw[1:2], w1i, w1o, mod_row=0, post_mod_row=3, tm=FFN_TM)
    _, hc = _ffn_call(ctx0, mc, nw[0:1], nw[1:2], w1i, w1o, mod_row=0, post_mod_row=3, tm=CTX_LEN)

    gq, gk = q_norm_w[0][None, :], k_norm_w[0][None, :]
    cos, sa, sb = _rope_tables(seq // GRID_W)
    qkv = _qkv_call(hx, wi, gq, gk, cos, sa, sb, first_tile=0, n_tiles=(Q_W + 2 * KV_W) // QKV_TN, tm=QKV_TM)
    ones, zeros = jnp.ones((CTX_LEN, HEAD_DIM), F32), jnp.zeros((CTX_LEN, HEAD_DIM), F32)
    ckv = _qkv_call(hc, wi, gq, gk, ones, zeros, zeros, first_tile=Q_W // QKV_TN, n_tiles=2 * KV_W // QKV_TN,
                    tm=CTX_LEN)
    attn = _attn_call(qkv, ckv)

    gm = _gmlp_call(hx, wi, gmlp_ln_w, gmlp_ln_b, ws, b_spatial[0].T, tm=GMLP_TM)
    x2 = _merge_call(x1, mx, hx, attn, gm, wi, b_gate[0], wba, wbg, wo, gate_row=5, tm=MERGE_TM)

    out = _ffn_call(x2, mx, nw[2:3], final_norm_w[None, :], w2i, w2o, mod_row=6, post_mod_row=None, tm=FFN_TM)
    return out[None]
```

```python
import functools
import math

import jax
import jax.numpy as jnp
from jax import lax
from jax.experimental import pallas as pl
from jax.experimental.pallas import tpu as pltpu

D_MODEL = 2048
SEQ = 8192
CTX_LEN = 256
GRID_W = 64
HEAD_DIM = 128
N_Q_HEADS = 16
N_KV_HEADS = 4
Q_PER_KV = N_Q_HEADS // N_KV_HEADS
ROPE_THETA = 10000.0
ATTN_SCALE = HEAD_DIM ** -0.5
GMLP_GROUPS = 16
GMLP_WIDTH = 2048
GROUP_DIM = GMLP_WIDTH // GMLP_GROUPS
CHUNK = 128
D_FF = 5632
MACARON_WEIGHT = 0.5
N_MOD = 9
EPS = 1e-6
LOG2_E = math.log2(math.e)

Q_W = N_Q_HEADS * HEAD_DIM
KV_W = N_KV_HEADS * HEAD_DIM
K_OFF = Q_W
V_OFF = K_OFF + KV_W
U_OFF = V_OFF + KV_W
GV_OFF = U_OFF + GMLP_WIDTH
GATE_OFF = GV_OFF + GMLP_WIDTH
IN_W = GATE_OFF + 2 * D_MODEL

V7X_VMEM_BYTES = 64 * 1024 * 1024
VMEM_CAP_BYTES = V7X_VMEM_BYTES - 6 * 1024 * 1024

MOD_ROWS = 16
MOD_TN = 1024
FFN_TM = 512
FFN_TF = 512
QKV_TM = 512
QKV_TN = KV_W
ATTN_TQ = 128
ATTN_UNROLL = 16
GMLP_TM = 512
GMLP_TN = 512
MERGE_TM = 512
MERGE_TN = 256

F32 = jnp.float32
BF16 = jnp.bfloat16


def _vmem_limit(pipelined_bytes, resident_bytes):
    return int(min(2 * pipelined_bytes + resident_bytes, VMEM_CAP_BYTES))


def _dot(a, b):
    return jnp.dot(a, b, preferred_element_type=F32)


def _rmsnorm(x, w):
    return x * lax.rsqrt(jnp.mean(x * x, axis=-1, keepdims=True) + EPS) * w


def _gelu_exact(x):
    return 0.5 * x * (1 + lax.erf(x * (2.0 ** -0.5)))


def _mod_kernel(c_ref, w_ref, b_ref, o_ref):
    sc = jax.nn.silu(c_ref[...])
    o_ref[...] = _dot(sc.astype(BF16), w_ref[...].astype(BF16)) + b_ref[...]


def _mod_call(c_rows, w_mod, b_mod):
    n = w_mod.shape[1]
    return pl.pallas_call(
        _mod_kernel,
        grid=(n // MOD_TN,),
        in_specs=[
            pl.BlockSpec((MOD_ROWS, D_MODEL), lambda j: (0, 0)),
            pl.BlockSpec((D_MODEL, MOD_TN), lambda j: (0, j)),
            pl.BlockSpec((1, MOD_TN), lambda j: (0, j)),
        ],
        out_specs=pl.BlockSpec((MOD_ROWS, MOD_TN), lambda j: (0, j)),
        out_shape=jax.ShapeDtypeStruct((MOD_ROWS, n), F32),
        compiler_params=pltpu.CompilerParams(
            dimension_semantics=("arbitrary",),
            vmem_limit_bytes=_vmem_limit(D_MODEL * MOD_TN * 4, D_MODEL * MOD_TN * 2 + (4 << 20))),
        name="mod",
    )(c_rows, w_mod, b_mod)


def _ffn_kernel(x_ref, mod_ref, nw_ref, pnw_ref, wa_ref, wb_ref, wo_ref, *refs, mod_row, post_mod_row):
    if post_mod_row is None:
        (o_ref, h_ref, acc_ref) = refs
    else:
        (y_ref, o_ref, h_ref, acc_ref) = refs
    j = pl.program_id(1)

    @pl.when(j == 0)
    def _():
        shift = mod_ref[mod_row:mod_row + 1, :]
        scale = mod_ref[mod_row + 1:mod_row + 2, :]
        h = _rmsnorm(x_ref[...], nw_ref[...]) * (1 + scale) + shift
        h_ref[...] = h.astype(BF16)
        acc_ref[...] = jnp.zeros_like(acc_ref)

    h = h_ref[...]
    a = _dot(h, wa_ref[...])
    b = _dot(h, wb_ref[...])
    act = jax.nn.silu(a) * b
    acc_ref[...] += _dot(act.astype(BF16), wo_ref[...])

    @pl.when(j == pl.num_programs(1) - 1)
    def _():
        gate = mod_ref[mod_row + 2:mod_row + 3, :]
        y = x_ref[...] + MACARON_WEIGHT * gate * acc_ref[...]
        yn = _rmsnorm(y, pnw_ref[...])
        if post_mod_row is None:
            o_ref[...] = yn
        else:
            y_ref[...] = y
            pshift = mod_ref[post_mod_row:post_mod_row + 1, :]
            pscale = mod_ref[post_mod_row + 1:post_mod_row + 2, :]
            o_ref[...] = (yn * (1 + pscale) + pshift).astype(BF16)


def _ffn_call(x, mods, nw, pnw, w_in, w_out, *, mod_row, post_mod_row, tm):
    m = x.shape[0]
    n_f = D_FF // FFN_TF
    row_f32 = pl.BlockSpec((tm, D_MODEL), lambda i, j: (i, 0))
    vec = pl.BlockSpec((1, D_MODEL), lambda i, j: (0, 0))
    if post_mod_row is None:
        out_shape = jax.ShapeDtypeStruct((m, D_MODEL), F32)
        out_specs = row_f32
        out_bytes = tm * D_MODEL * 4
    else:
        out_shape = (jax.ShapeDtypeStruct((m, D_MODEL), F32), jax.ShapeDtypeStruct((m, D_MODEL), BF16))
        out_specs = (row_f32, pl.BlockSpec((tm, D_MODEL), lambda i, j: (i, 0)))
        out_bytes = tm * D_MODEL * 6
    pipelined = tm * D_MODEL * 4 + out_bytes + 3 * D_MODEL * FFN_TF * 2
    resident = tm * D_MODEL * 6 + 6 * tm * FFN_TF * 4 + (2 << 20)
    return pl.pallas_call(
        functools.partial(_ffn_kernel, mod_row=mod_row, post_mod_row=post_mod_row),
        grid=(m // tm, n_f),
        in_specs=[
            row_f32,
            pl.BlockSpec((N_MOD, D_MODEL), lambda i, j: (0, 0)),
            vec,
            vec,
            pl.BlockSpec((D_MODEL, FFN_TF), lambda i, j: (0, j)),
            pl.BlockSpec((D_MODEL, FFN_TF), lambda i, j: (0, j + n_f)),
            pl.BlockSpec((FFN_TF, D_MODEL), lambda i, j: (j, 0)),
        ],
        out_specs=out_specs,
        out_shape=out_shape,
        scratch_shapes=[pltpu.VMEM((tm, D_MODEL), BF16), pltpu.VMEM((tm, D_MODEL), F32)],
        compiler_params=pltpu.CompilerParams(
            dimension_semantics=("parallel", "arbitrary"),
            vmem_limit_bytes=_vmem_limit(pipelined, resident)),
        name="ffn",
    )(x, mods, nw, pnw, w_in, w_in, w_out)


def _qkv_kernel(h_ref, w_ref, gq_ref, gk_ref, cos_ref, sa_ref, sb_ref, o_ref, vt_ref, *, first_tile):
    j = pl.program_id(1) + first_tile
    z = _dot(h_ref[...], w_ref[...])
    n_q_tiles = Q_W // QKV_TN
    n_k_tiles = KV_W // QKV_TN

    def norm_rope(gain, post_scale):
        cos, sa, sb = cos_ref[...], sa_ref[...], sb_ref[...]
        for hh in range(QKV_TN // HEAD_DIM):
            cols = slice(hh * HEAD_DIM, (hh + 1) * HEAD_DIM)
            y = _rmsnorm(z[:, cols], gain)
            y = y * cos + pltpu.roll(y, HEAD_DIM - 1, 1) * sa + pltpu.roll(y, 1, 1) * sb
            if post_scale is not None:
                y = y * post_scale
            o_ref[:, cols] = y.astype(o_ref.dtype)

    @pl.when(j < n_q_tiles)
    def _():
        norm_rope(gq_ref[...], ATTN_SCALE * LOG2_E)

    @pl.when(jnp.logical_and(j >= n_q_tiles, j < n_q_tiles + n_k_tiles))
    def _():
        norm_rope(gk_ref[...], None)

    @pl.when(j >= n_q_tiles + n_k_tiles)
    def _():
        vt_ref[0] = z.T.astype(vt_ref.dtype)


def _qkv_call(h, w_in, gq, gk, cos, sa, sb, *, first_tile, n_tiles, tm):
    m = h.shape[0]
    n_qk = n_tiles - 1
    tab = pl.BlockSpec((tm, HEAD_DIM), lambda i, j: (i, 0))
    gain = pl.BlockSpec((1, HEAD_DIM), lambda i, j: (0, 0))
    pipelined = (tm * D_MODEL * 2 + D_MODEL * QKV_TN * 2 + 3 * tm * HEAD_DIM * 4 + tm * QKV_TN * 2
                 + KV_W * tm * 2)
    resident = 8 * tm * QKV_TN * 4
    return pl.pallas_call(
        functools.partial(_qkv_kernel, first_tile=first_tile),
        grid=(m // tm, n_tiles),
        in_specs=[
            pl.BlockSpec((tm, D_MODEL), lambda i, j: (i, 0)),
            pl.BlockSpec((D_MODEL, QKV_TN), lambda i, j: (0, j + first_tile)),
            gain, gain, tab, tab, tab,
        ],
        out_specs=(
            pl.BlockSpec((tm, QKV_TN), lambda i, j: (i, jnp.minimum(j, n_qk - 1))),
            pl.BlockSpec((1, KV_W, tm), lambda i, j: (i, 0, 0)),
        ),
        out_shape=(
            jax.ShapeDtypeStruct((m, n_qk * QKV_TN), BF16),
            jax.ShapeDtypeStruct((m // tm, KV_W, tm), BF16),
        ),
        compiler_params=pltpu.CompilerParams(
            dimension_semantics=("parallel", "arbitrary"),
            vmem_limit_bytes=_vmem_limit(pipelined, resident)),
        name="qkv",
    )(h, w_in, gq, gk, cos, sa, sb)


def _attn_kernel(q_ref, k_ref, vt_ref, kc_ref, vct_ref, o_ref, s_ref, acc_ref):
    tq = q_ref.shape[0]
    c_len = kc_ref.shape[0]
    n_chunks = vt_ref.shape[0]
    tk = vt_ref.shape[2]
    q = jnp.concatenate([q_ref[:, g * HEAD_DIM:(g + 1) * HEAD_DIM] for g in range(Q_PER_KV)], axis=0)
    rows = q.shape[0]

    def scores_t(k):
        return lax.dot_general(k, q, (((1,), (1,)), ((), ())), preferred_element_type=F32)

    def sublane_groups(a):
        return a.reshape(a.shape[0] // 8, 8, rows)

    def key_start(t):
        return pl.multiple_of(c_len + t * tk, math.gcd(c_len, tk))

    s = scores_t(kc_ref[...])
    s_ref[0:c_len, :] = s
    m8 = jnp.max(sublane_groups(s), axis=0)

    def pass1(t, m8):
        s = scores_t(k_ref[pl.ds(pl.multiple_of(t * tk, tk), tk), :])
        s_ref[pl.ds(key_start(t), tk), :] = s
        return jnp.maximum(m8, jnp.max(sublane_groups(s), axis=0))

    m8 = lax.fori_loop(0, n_chunks, pass1, m8, unroll=ATTN_UNROLL)
    m = jnp.max(m8, axis=0, keepdims=True)

    p = jnp.exp2(s_ref[0:c_len, :] - m)
    l8 = jnp.sum(sublane_groups(p), axis=0)
    acc_ref[...] = _dot(vct_ref[0], p.astype(BF16))

    def pass2(t, l8):
        p = jnp.exp2(s_ref[pl.ds(key_start(t), tk), :] - m)
        acc_ref[...] += _dot(vt_ref[t], p.astype(BF16))
        return l8 + jnp.sum(sublane_groups(p), axis=0)

    l8 = lax.fori_loop(0, n_chunks, pass2, l8, unroll=ATTN_UNROLL)
    l = jnp.sum(l8, axis=0, keepdims=True)
    o = (acc_ref[...] / l).T
    for g in range(Q_PER_KV):
        o_ref[:, g * HEAD_DIM:(g + 1) * HEAD_DIM] = o[g * tq:(g + 1) * tq].astype(o_ref.dtype)


def _attn_call(qk, vt, ck, vct):
    s_len = qk.shape[0]
    c_len = ck.shape[0]
    n_chunks, _, tk = vt.shape
    gw = Q_PER_KV * HEAD_DIM
    rows = Q_PER_KV * ATTN_TQ
    pipelined = 2 * ATTN_TQ * gw * 2 + 2 * s_len * HEAD_DIM * 2 + 2 * c_len * HEAD_DIM * 2
    resident = (s_len + c_len) * rows * 4 + HEAD_DIM * rows * 4 + 6 * tk * rows * 4
    return pl.pallas_call(
        _attn_kernel,
        grid=(N_KV_HEADS, s_len // ATTN_TQ),
        in_specs=[
            pl.BlockSpec((ATTN_TQ, gw), lambda kv, i: (i, kv)),
            pl.BlockSpec((s_len, HEAD_DIM), lambda kv, i: (0, K_OFF // HEAD_DIM + kv)),
            pl.BlockSpec((n_chunks, HEAD_DIM, tk), lambda kv, i: (0, kv, 0)),
            pl.BlockSpec((c_len, HEAD_DIM), lambda kv, i: (0, kv)),
            pl.BlockSpec((1, HEAD_DIM, c_len), lambda kv, i: (0, kv, 0)),
        ],
        out_specs=pl.BlockSpec((ATTN_TQ, gw), lambda kv, i: (i, kv)),
        out_shape=jax.ShapeDtypeStruct((s_len, Q_W), BF16),
        scratch_shapes=[
            pltpu.VMEM((c_len + s_len, rows), F32),
            pltpu.VMEM((HEAD_DIM, rows), F32),
        ],
        compiler_params=pltpu.CompilerParams(
            dimension_semantics=("parallel", "parallel"),
            vmem_limit_bytes=_vmem_limit(pipelined, resident)),
        name="attn",
    )(qk, qk, vt, ck, vct)


def _gmlp_kernel(h_ref, wu_ref, wv_ref, lnw_ref, lnb_ref, ws_ref, bs_ref, o_ref, gu_ref, gv_ref):
    j = pl.program_id(1)
    n_j = GMLP_WIDTH // GMLP_TN
    h = h_ref[...]
    gu_ref[j] = _gelu_exact(_dot(h, wu_ref[...]))
    gv_ref[j] = _gelu_exact(_dot(h, wv_ref[...]))

    @pl.when(j == n_j - 1)
    def _():
        tm = h_ref.shape[0]
        total = gv_ref[0].sum(axis=-1, keepdims=True)
        for jj in range(1, n_j):
            total = total + gv_ref[jj].sum(axis=-1, keepdims=True)
        mu = total / GMLP_WIDTH
        sq = None
        for jj in range(n_j):
            xc = gv_ref[jj] - mu
            part = (xc * xc).sum(axis=-1, keepdims=True)
            sq = part if sq is None else sq + part
        rstd = lax.rsqrt(sq / GMLP_WIDTH + EPS)
        groups_per_tile = GMLP_TN // GROUP_DIM
        for jj in range(n_j):
            tile_cols = slice(jj * GMLP_TN, (jj + 1) * GMLP_TN)
            vn = ((gv_ref[jj] - mu) * rstd * lnw_ref[:, tile_cols] + lnb_ref[:, tile_cols]).astype(BF16)
            for gg in range(groups_per_tile):
                g = jj * groups_per_tile + gg
                cols = slice(gg * GROUP_DIM, (gg + 1) * GROUP_DIM)
                bias = bs_ref[:, g:g + 1]
                for c in range(tm // CHUNK):
                    rows = slice(c * CHUNK, (c + 1) * CHUNK)
                    mixed = _dot(ws_ref[g], vn[rows, cols]) + bias
                    o_ref[rows, g * GROUP_DIM:(g + 1) * GROUP_DIM] = (
                        gu_ref[jj, rows, cols] * mixed).astype(o_ref.dtype)


def _gmlp_call(h, w_in, lnw, lnb, w_s, b_s_t, *, tm):
    m = h.shape[0]
    n_j = GMLP_WIDTH // GMLP_TN
    vec = pl.BlockSpec((1, GMLP_WIDTH), lambda i, j: (0, 0))
    pipelined = (tm * D_MODEL * 2 + 2 * D_MODEL * GMLP_TN * 2 + tm * GMLP_WIDTH * 2
                 + GMLP_GROUPS * CHUNK * CHUNK * 2 + CHUNK * 128 * 4)
    resident = 2 * tm * GMLP_WIDTH * 4 + 8 * tm * GMLP_TN * 4
    return pl.pallas_call(
        _gmlp_kernel,
        grid=(m // tm, n_j),
        in_specs=[
            pl.BlockSpec((tm, D_MODEL), lambda i, j: (i, 0)),
            pl.BlockSpec((D_MODEL, GMLP_TN), lambda i, j: (0, U_OFF // GMLP_TN + j)),
            pl.BlockSpec((D_MODEL, GMLP_TN), lambda i, j: (0, GV_OFF // GMLP_TN + j)),
            vec, vec,
            pl.BlockSpec((GMLP_GROUPS, CHUNK, CHUNK), lambda i, j: (0, 0, 0)),
            pl.BlockSpec((CHUNK, GMLP_GROUPS), lambda i, j: (0, 0)),
        ],
        out_specs=pl.BlockSpec((tm, GMLP_WIDTH), lambda i, j: (i, 0)),
        out_shape=jax.ShapeDtypeStruct((m, GMLP_WIDTH), BF16),
        scratch_shapes=[pltpu.VMEM((n_j, tm, GMLP_TN), F32), pltpu.VMEM((n_j, tm, GMLP_TN), F32)],
        compiler_params=pltpu.CompilerParams(
            dimension_semantics=("parallel", "arbitrary"),
            vmem_limit_bytes=_vmem_limit(pipelined, resident)),
        name="gmlp",
    )(h, w_in, w_in, lnw, lnb, w_s, b_s_t)


def _merge_kernel(x_ref, mod_ref, h_ref, attn_ref, gm_ref, wga_ref, wgb_ref, bg_ref, wba_ref, wbg_ref, wo_ref,
                  o_ref, acc_ref, *, gate_row):
    j = pl.program_id(1)

    @pl.when(j == 0)
    def _():
        acc_ref[...] = jnp.zeros_like(acc_ref)

    h = h_ref[...]
    ga = jax.nn.sigmoid(_dot(h, wga_ref[...]) + bg_ref[0:1, :])
    gb = jax.nn.sigmoid(_dot(h, wgb_ref[...]) + bg_ref[1:2, :])
    merged = ga * _dot(attn_ref[...], wba_ref[...]) + gb * _dot(gm_ref[...], wbg_ref[...])
    acc_ref[...] += _dot(merged.astype(BF16), wo_ref[...])

    @pl.when(j == pl.num_programs(1) - 1)
    def _():
        o_ref[...] = x_ref[...] + mod_ref[gate_row:gate_row + 1, :] * acc_ref[...]


def _merge_call(x, mods, h, attn, gm, w_in, b_gate, w_ba, w_bg, w_o, *, gate_row, tm):
    m = x.shape[0]
    n_j = D_MODEL // MERGE_TN
    row_bf = pl.BlockSpec((tm, D_MODEL), lambda i, j: (i, 0))
    row_f32 = pl.BlockSpec((tm, D_MODEL), lambda i, j: (i, 0))
    col_w = pl.BlockSpec((D_MODEL, MERGE_TN), lambda i, j: (0, j))
    pipelined = 2 * tm * D_MODEL * 4 + 3 * tm * D_MODEL * 2 + 5 * D_MODEL * MERGE_TN * 2
    resident = tm * D_MODEL * 4 + 8 * tm * MERGE_TN * 4 + (2 << 20)
    return pl.pallas_call(
        functools.partial(_merge_kernel, gate_row=gate_row),
        grid=(m // tm, n_j),
        in_specs=[
            row_f32,
            pl.BlockSpec((N_MOD, D_MODEL), lambda i, j: (0, 0)),
            row_bf, row_bf, row_bf,
            pl.BlockSpec((D_MODEL, MERGE_TN), lambda i, j: (0, GATE_OFF // MERGE_TN + j)),
            pl.BlockSpec((D_MODEL, MERGE_TN), lambda i, j: (0, (GATE_OFF + D_MODEL) // MERGE_TN + j)),
            pl.BlockSpec((2, MERGE_TN), lambda i, j: (0, j)),
            col_w, col_w,
            pl.BlockSpec((MERGE_TN, D_MODEL), lambda i, j: (j, 0)),
        ],
        out_specs=row_f32,
        out_shape=jax.ShapeDtypeStruct((m, D_MODEL), F32),
        scratch_shapes=[pltpu.VMEM((tm, D_MODEL), F32)],
        compiler_params=pltpu.CompilerParams(
            dimension_semantics=("parallel", "arbitrary"),
            vmem_limit_bytes=_vmem_limit(pipelined, resident)),
        name="merge",
    )(x, mods, h, attn, gm, w_in, w_in, b_gate, w_ba, w_bg, w_o)


def _rope_tables(n_rows):
    axis_dim = HEAD_DIM // 2
    row = jnp.broadcast_to(jnp.arange(n_rows, dtype=F32)[:, None], (n_rows, GRID_W)).reshape(-1)
    col = jnp.broadcast_to(jnp.arange(GRID_W, dtype=F32)[None, :], (n_rows, GRID_W)).reshape(-1)
    inv_freq = ROPE_THETA ** (-jnp.arange(0, axis_dim, 2, dtype=F32) / axis_dim)
    ang = jnp.concatenate([row[:, None] * inv_freq, col[:, None] * inv_freq], axis=-1)
    cos, sin = jnp.cos(ang), jnp.sin(ang)
    zero = jnp.zeros_like(sin)
    cos2 = jnp.repeat(cos, 2, axis=-1)
    sa = jnp.stack([-sin, zero], axis=-1).reshape(cos2.shape)
    sb = jnp.stack([zero, sin], axis=-1).reshape(cos2.shape)
    return cos2, sa, sb


def kernel(x, c, ctx, c_ctx, w_mod, b_mod, norm_w, w_ffn1_in, w_ffn1_out, w_ffn2_in, w_ffn2_out, w_in, b_gate,
           q_norm_w, k_norm_w, gmlp_ln_w, gmlp_ln_b, w_spatial, b_spatial, w_branch_attn, w_branch_gmlp, w_out,
           final_norm_w):
    batch, seq, d = x.shape
    assert batch == 1 and d == D_MODEL and seq == SEQ and seq % GRID_W == 0
    assert w_mod.shape[0] == 1 and ctx.shape == (1, CTX_LEN, D_MODEL) and w_in.shape[-1] == IN_W

    x0 = x[0]
    ctx0 = ctx[0]
    w1i, w1o = w_ffn1_in[0].astype(BF16), w_ffn1_out[0].astype(BF16)
    w2i, w2o = w_ffn2_in[0].astype(BF16), w_ffn2_out[0].astype(BF16)
    wi = w_in[0].astype(BF16)
    wba, wbg, wo = w_branch_attn[0].astype(BF16), w_branch_gmlp[0].astype(BF16), w_out[0].astype(BF16)
    ws = w_spatial[0].astype(BF16)
    nw = norm_w[0]

    c_rows = jnp.zeros((MOD_ROWS, D_MODEL), F32).at[0].set(c[0]).at[1].set(c_ctx)
    mods = _mod_call(c_rows, w_mod[0], b_mod).reshape(MOD_ROWS, N_MOD, D_MODEL)
    mx, mc = mods[0], mods[1]

    x1, hx = _ffn_call(x0, mx, nw[0:1], nw[1:2], w1i, w1o, mod_row=0, post_mod_row=3, tm=FFN_TM)
    _, hc = _ffn_call(ctx0, mc, nw[0:1], nw[1:2], w1i, w1o, mod_row=0, post_mod_row=3, tm=CTX_LEN)

    gq, gk = q_norm_w[0][None, :], k_norm_w[0][None, :]
    cos, sa, sb = _rope_tables(seq // GRID_W)
    qk, vt = _qkv_call(hx, wi, gq, gk, cos, sa, sb, first_tile=0, n_tiles=(Q_W + 2 * KV_W) // QKV_TN, tm=QKV_TM)
    ones, zeros = jnp.ones((CTX_LEN, HEAD_DIM), F32), jnp.zeros((CTX_LEN, HEAD_DIM), F32)
    ck, vct = _qkv_call(hc, wi, gq, gk, ones, zeros, zeros, first_tile=Q_W // QKV_TN, n_tiles=2 * KV_W // QKV_TN,
                        tm=CTX_LEN)
    attn = _attn_call(qk, vt, ck, vct)

    gm = _gmlp_call(hx, wi, gmlp_ln_w, gmlp_ln_b, ws, b_spatial[0].T, tm=GMLP_TM)
    x2 = _merge_call(x1, mx, hx, attn, gm, wi, b_gate[0], wba, wbg, wo, gate_row=5, tm=MERGE_TM)

    out = _ffn_call(x2, mx, nw[2:3], final_norm_w[None, :], w2i, w2o, mod_row=6, post_mod_row=None, tm=FFN_TM)
    return out[None]
```

```python
import functools
import math

import jax
import jax.numpy as jnp
from jax import lax
from jax.experimental import pallas as pl
from jax.experimental.pallas import tpu as pltpu

D_MODEL = 2048
SEQ = 8192
CTX_LEN = 256
GRID_W = 64
HEAD_DIM = 128
N_Q_HEADS = 16
N_KV_HEADS = 4
Q_PER_KV = N_Q_HEADS // N_KV_HEADS
ROPE_THETA = 10000.0
ATTN_SCALE = HEAD_DIM ** -0.5
GMLP_GROUPS = 16
GMLP_WIDTH = 2048
GROUP_DIM = GMLP_WIDTH // GMLP_GROUPS
CHUNK = 128
D_FF = 5632
MACARON_WEIGHT = 0.5
N_MOD = 9
EPS = 1e-6
LOG2_E = math.log2(math.e)

Q_W = N_Q_HEADS * HEAD_DIM
KV_W = N_KV_HEADS * HEAD_DIM
K_OFF = Q_W
V_OFF = K_OFF + KV_W
U_OFF = V_OFF + KV_W
GV_OFF = U_OFF + GMLP_WIDTH
GATE_OFF = GV_OFF + GMLP_WIDTH
IN_W = GATE_OFF + 2 * D_MODEL

V7X_VMEM_BYTES = 64 * 1024 * 1024
VMEM_CAP_BYTES = V7X_VMEM_BYTES - 6 * 1024 * 1024

MOD_ROWS = 16
MOD_TN = 1024
FFN_TM = 512
FFN_TF = 512
QKV_TM = 512
QKV_TN = KV_W
ATTN_TQ = 128
GMLP_TM = 512
GMLP_TN = 512
MERGE_TM = 512
MERGE_TN = 256

F32 = jnp.float32
BF16 = jnp.bfloat16


def _vmem_limit(pipelined_bytes, resident_bytes):
    return int(min(2 * pipelined_bytes + resident_bytes, VMEM_CAP_BYTES))


def _dot(a, b):
    return jnp.dot(a, b, preferred_element_type=F32)


def _rmsnorm(x, w):
    return x * lax.rsqrt(jnp.mean(x * x, axis=-1, keepdims=True) + EPS) * w


def _gelu_exact(x):
    return 0.5 * x * (1 + lax.erf(x * (2.0 ** -0.5)))


def _mod_kernel(c_ref, w_ref, b_ref, o_ref):
    sc = jax.nn.silu(c_ref[...])
    o_ref[...] = _dot(sc.astype(BF16), w_ref[...].astype(BF16)) + b_ref[...]


def _mod_call(c_rows, w_mod, b_mod):
    n = w_mod.shape[1]
    return pl.pallas_call(
        _mod_kernel,
        grid=(n // MOD_TN,),
        in_specs=[
            pl.BlockSpec((MOD_ROWS, D_MODEL), lambda j: (0, 0)),
            pl.BlockSpec((D_MODEL, MOD_TN), lambda j: (0, j)),
            pl.BlockSpec((1, MOD_TN), lambda j: (0, j)),
        ],
        out_specs=pl.BlockSpec((MOD_ROWS, MOD_TN), lambda j: (0, j)),
        out_shape=jax.ShapeDtypeStruct((MOD_ROWS, n), F32),
        compiler_params=pltpu.CompilerParams(
            dimension_semantics=("arbitrary",),
            vmem_limit_bytes=_vmem_limit(D_MODEL * MOD_TN * 4, D_MODEL * MOD_TN * 2 + (4 << 20))),
        name="mod",
    )(c_rows, w_mod, b_mod)


def _ffn_kernel(x_ref, mod_ref, nw_ref, pnw_ref, wa_ref, wb_ref, wo_ref, *refs, mod_row, post_mod_row):
    if post_mod_row is None:
        (o_ref, h_ref, acc_ref) = refs
    else:
        (y_ref, o_ref, h_ref, acc_ref) = refs
    j = pl.program_id(1)

    @pl.when(j == 0)
    def _():
        shift = mod_ref[mod_row:mod_row + 1, :]
        scale = mod_ref[mod_row + 1:mod_row + 2, :]
        h = _rmsnorm(x_ref[...], nw_ref[...]) * (1 + scale) + shift
        h_ref[...] = h.astype(BF16)
        acc_ref[...] = jnp.zeros_like(acc_ref)

    h = h_ref[...]
    a = _dot(h, wa_ref[...])
    b = _dot(h, wb_ref[...])
    act = jax.nn.silu(a) * b
    acc_ref[...] += _dot(act.astype(BF16), wo_ref[...])

    @pl.when(j == pl.num_programs(1) - 1)
    def _():
        gate = mod_ref[mod_row + 2:mod_row + 3, :]
        y = x_ref[...] + MACARON_WEIGHT * gate * acc_ref[...]
        yn = _rmsnorm(y, pnw_ref[...])
        if post_mod_row is None:
            o_ref[...] = yn
        else:
            y_ref[...] = y
            pshift = mod_ref[post_mod_row:post_mod_row + 1, :]
            pscale = mod_ref[post_mod_row + 1:post_mod_row + 2, :]
            o_ref[...] = (yn * (1 + pscale) + pshift).astype(BF16)


def _ffn_call(x, mods, nw, pnw, w_in, w_out, *, mod_row, post_mod_row, tm):
    m = x.shape[0]
    n_f = D_FF // FFN_TF
    row_f32 = pl.BlockSpec((tm, D_MODEL), lambda i, j: (i, 0))
    vec = pl.BlockSpec((1, D_MODEL), lambda i, j: (0, 0))
    if post_mod_row is None:
        out_shape = jax.ShapeDtypeStruct((m, D_MODEL), F32)
        out_specs = row_f32
        out_bytes = tm * D_MODEL * 4
    else:
        out_shape = (jax.ShapeDtypeStruct((m, D_MODEL), F32), jax.ShapeDtypeStruct((m, D_MODEL), BF16))
        out_specs = (row_f32, pl.BlockSpec((tm, D_MODEL), lambda i, j: (i, 0)))
        out_bytes = tm * D_MODEL * 6
    pipelined = tm * D_MODEL * 4 + out_bytes + 3 * D_MODEL * FFN_TF * 2
    resident = tm * D_MODEL * 6 + 6 * tm * FFN_TF * 4 + (2 << 20)
    return pl.pallas_call(
        functools.partial(_ffn_kernel, mod_row=mod_row, post_mod_row=post_mod_row),
        grid=(m // tm, n_f),
        in_specs=[
            row_f32,
            pl.BlockSpec((N_MOD, D_MODEL), lambda i, j: (0, 0)),
            vec,
            vec,
            pl.BlockSpec((D_MODEL, FFN_TF), lambda i, j: (0, j)),
            pl.BlockSpec((D_MODEL, FFN_TF), lambda i, j: (0, j + n_f)),
            pl.BlockSpec((FFN_TF, D_MODEL), lambda i, j: (j, 0)),
        ],
        out_specs=out_specs,
        out_shape=out_shape,
        scratch_shapes=[pltpu.VMEM((tm, D_MODEL), BF16), pltpu.VMEM((tm, D_MODEL), F32)],
        compiler_params=pltpu.CompilerParams(
            dimension_semantics=("parallel", "arbitrary"),
            vmem_limit_bytes=_vmem_limit(pipelined, resident)),
        name="ffn",
    )(x, mods, nw, pnw, w_in, w_in, w_out)


def _qkv_kernel(h_ref, w_ref, gq_ref, gk_ref, cos_ref, sa_ref, sb_ref, o_ref, vt_ref, *, first_tile):
    j = pl.program_id(1) + first_tile
    z = _dot(h_ref[...], w_ref[...])
    n_q_tiles = Q_W // QKV_TN
    n_k_tiles = KV_W // QKV_TN

    def norm_rope(gain, post_scale):
        cos, sa, sb = cos_ref[...], sa_ref[...], sb_ref[...]
        for hh in range(QKV_TN // HEAD_DIM):
            cols = slice(hh * HEAD_DIM, (hh + 1) * HEAD_DIM)
            y = _rmsnorm(z[:, cols], gain)
            y = y * cos + pltpu.roll(y, HEAD_DIM - 1, 1) * sa + pltpu.roll(y, 1, 1) * sb
            if post_scale is not None:
                y = y * post_scale
            o_ref[:, cols] = y.astype(o_ref.dtype)

    @pl.when(j < n_q_tiles)
    def _():
        norm_rope(gq_ref[...], ATTN_SCALE * LOG2_E)

    @pl.when(jnp.logical_and(j >= n_q_tiles, j < n_q_tiles + n_k_tiles))
    def _():
        norm_rope(gk_ref[...], None)

    @pl.when(j >= n_q_tiles + n_k_tiles)
    def _():
        vt_ref[0] = z.T.astype(vt_ref.dtype)


def _qkv_call(h, w_in, gq, gk, cos, sa, sb, *, first_tile, n_tiles, tm):
    m = h.shape[0]
    n_qk = n_tiles - 1
    tab = pl.BlockSpec((tm, HEAD_DIM), lambda i, j: (i, 0))
    gain = pl.BlockSpec((1, HEAD_DIM), lambda i, j: (0, 0))
    pipelined = (tm * D_MODEL * 2 + D_MODEL * QKV_TN * 2 + 3 * tm * HEAD_DIM * 4 + tm * QKV_TN * 2
                 + KV_W * tm * 2)
    resident = 8 * tm * QKV_TN * 4
    return pl.pallas_call(
        functools.partial(_qkv_kernel, first_tile=first_tile),
        grid=(m // tm, n_tiles),
        in_specs=[
            pl.BlockSpec((tm, D_MODEL), lambda i, j: (i, 0)),
            pl.BlockSpec((D_MODEL, QKV_TN), lambda i, j: (0, j + first_tile)),
            gain, gain, tab, tab, tab,
        ],
        out_specs=(
            pl.BlockSpec((tm, QKV_TN), lambda i, j: (i, jnp.minimum(j, n_qk - 1))),
            pl.BlockSpec((1, KV_W, tm), lambda i, j: (i, 0, 0)),
        ),
        out_shape=(
            jax.ShapeDtypeStruct((m, n_qk * QKV_TN), BF16),
            jax.ShapeDtypeStruct((m // tm, KV_W, tm), BF16),
        ),
        compiler_params=pltpu.CompilerParams(
            dimension_semantics=("parallel", "arbitrary"),
            vmem_limit_bytes=_vmem_limit(pipelined, resident)),
        name="qkv",
    )(h, w_in, gq, gk, cos, sa, sb)


def _attn_kernel(q_ref, k_ref, vt_ref, kc_ref, vct_ref, o_ref, s0_ref, s1_ref, m0_ref, m1_ref):
    i = pl.program_id(1)
    tq = q_ref.shape[0]
    c_len = kc_ref.shape[0]
    n_chunks = vt_ref.shape[0]
    tk = vt_ref.shape[2]

    @pl.when(jnp.logical_and(pl.program_id(0) == 0, i == 0))
    def _():
        s1_ref[...] = jnp.zeros(s1_ref.shape, F32)
        m1_ref[...] = jnp.zeros(m1_ref.shape, F32)

    def step(s_cur, m_cur, s_prev, m_prev_ref):
        q = jnp.concatenate([q_ref[:, g * HEAD_DIM:(g + 1) * HEAD_DIM] for g in range(Q_PER_KV)], axis=0)
        rows = q.shape[0]
        m_prev = m_prev_ref[...]

        def sublane_groups(a):
            return a.reshape(a.shape[0] // 8, 8, rows)

        def pass1_chunk(k, key_rows, m8):
            s = lax.dot_general(k, q, (((1,), (1,)), ((), ())), preferred_element_type=F32)
            s_cur[key_rows, :] = s
            part = jnp.max(sublane_groups(s), axis=0)
            return part if m8 is None else jnp.maximum(m8, part)

        def pass2_chunk(vt, key_rows, l8, acc):
            p = jnp.exp2(s_prev[key_rows, :] - m_prev)
            part = jnp.sum(sublane_groups(p), axis=0)
            pv = _dot(vt, p.astype(BF16))
            return (part, pv) if l8 is None else (l8 + part, acc + pv)

        ctx_rows = slice(0, c_len)
        m8 = pass1_chunk(kc_ref[...], ctx_rows, None)
        l8, acc = pass2_chunk(vct_ref[0], ctx_rows, None, None)
        for t in range(n_chunks):
            key_rows = slice(c_len + t * tk, c_len + (t + 1) * tk)
            m8 = pass1_chunk(k_ref[t * tk:(t + 1) * tk, :], key_rows, m8)
            l8, acc = pass2_chunk(vt_ref[t], key_rows, l8, acc)

        m_cur[...] = jnp.max(m8, axis=0, keepdims=True)
        o = (acc / jnp.sum(l8, axis=0, keepdims=True)).T
        for g in range(Q_PER_KV):
            o_ref[:, g * HEAD_DIM:(g + 1) * HEAD_DIM] = o[g * tq:(g + 1) * tq].astype(o_ref.dtype)

    @pl.when(i % 2 == 0)
    def _():
        step(s0_ref, m0_ref, s1_ref, m1_ref)

    @pl.when(i % 2 == 1)
    def _():
        step(s1_ref, m1_ref, s0_ref, m0_ref)


def _attn_call(qk, vt, ck, vct):
    s_len = qk.shape[0]
    c_len = ck.shape[0]
    n_chunks, _, tk = vt.shape
    n_tiles = s_len // ATTN_TQ
    gw = Q_PER_KV * HEAD_DIM
    rows = Q_PER_KV * ATTN_TQ
    pipelined = 2 * ATTN_TQ * gw * 2 + 2 * s_len * HEAD_DIM * 2 + 2 * c_len * HEAD_DIM * 2
    resident = 2 * (s_len + c_len) * rows * 4 + 8 * tk * rows * 4
    return pl.pallas_call(
        _attn_kernel,
        grid=(N_KV_HEADS, n_tiles + 1),
        in_specs=[
            pl.BlockSpec((ATTN_TQ, gw), lambda kv, i: (jnp.minimum(i, n_tiles - 1), kv)),
            pl.BlockSpec((s_len, HEAD_DIM), lambda kv, i: (0, K_OFF // HEAD_DIM + kv)),
            pl.BlockSpec((n_chunks, HEAD_DIM, tk), lambda kv, i: (0, kv, 0)),
            pl.BlockSpec((c_len, HEAD_DIM), lambda kv, i: (0, kv)),
            pl.BlockSpec((1, HEAD_DIM, c_len), lambda kv, i: (0, kv, 0)),
        ],
        out_specs=pl.BlockSpec((ATTN_TQ, gw), lambda kv, i: (jnp.maximum(i - 1, 0), kv)),
        out_shape=jax.ShapeDtypeStruct((s_len, Q_W), BF16),
        scratch_shapes=[
            pltpu.VMEM((c_len + s_len, rows), F32),
            pltpu.VMEM((c_len + s_len, rows), F32),
            pltpu.VMEM((1, rows), F32),
            pltpu.VMEM((1, rows), F32),
        ],
        compiler_params=pltpu.CompilerParams(
            dimension_semantics=("arbitrary", "arbitrary"),
            vmem_limit_bytes=_vmem_limit(pipelined, resident)),
        name="attn",
    )(qk, qk, vt, ck, vct)


def _gmlp_kernel(h_ref, wu_ref, wv_ref, lnw_ref, lnb_ref, ws_ref, bs_ref, o_ref, gu_ref, gv_ref):
    j = pl.program_id(1)
    n_j = GMLP_WIDTH // GMLP_TN
    h = h_ref[...]
    gu_ref[j] = _gelu_exact(_dot(h, wu_ref[...]))
    gv_ref[j] = _gelu_exact(_dot(h, wv_ref[...]))

    @pl.when(j == n_j - 1)
    def _():
        tm = h_ref.shape[0]
        total = gv_ref[0].sum(axis=-1, keepdims=True)
        for jj in range(1, n_j):
            total = total + gv_ref[jj].sum(axis=-1, keepdims=True)
        mu = total / GMLP_WIDTH
        sq = None
        for jj in range(n_j):
            xc = gv_ref[jj] - mu
            part = (xc * xc).sum(axis=-1, keepdims=True)
            sq = part if sq is None else sq + part
        rstd = lax.rsqrt(sq / GMLP_WIDTH + EPS)
        groups_per_tile = GMLP_TN // GROUP_DIM
        for jj in range(n_j):
            tile_cols = slice(jj * GMLP_TN, (jj + 1) * GMLP_TN)
            vn = ((gv_ref[jj] - mu) * rstd * lnw_ref[:, tile_cols] + lnb_ref[:, tile_cols]).astype(BF16)
            for gg in range(groups_per_tile):
                g = jj * groups_per_tile + gg
                cols = slice(gg * GROUP_DIM, (gg + 1) * GROUP_DIM)
                bias = bs_ref[:, g:g + 1]
                for c in range(tm // CHUNK):
                    rows = slice(c * CHUNK, (c + 1) * CHUNK)
                    mixed = _dot(ws_ref[g], vn[rows, cols]) + bias
                    o_ref[rows, g * GROUP_DIM:(g + 1) * GROUP_DIM] = (
                        gu_ref[jj, rows, cols] * mixed).astype(o_ref.dtype)


def _gmlp_call(h, w_in, lnw, lnb, w_s, b_s_t, *, tm):
    m = h.shape[0]
    n_j = GMLP_WIDTH // GMLP_TN
    vec = pl.BlockSpec((1, GMLP_WIDTH), lambda i, j: (0, 0))
    pipelined = (tm * D_MODEL * 2 + 2 * D_MODEL * GMLP_TN * 2 + tm * GMLP_WIDTH * 2
                 + GMLP_GROUPS * CHUNK * CHUNK * 2 + CHUNK * 128 * 4)
    resident = 2 * tm * GMLP_WIDTH * 4 + 8 * tm * GMLP_TN * 4
    return pl.pallas_call(
        _gmlp_kernel,
        grid=(m // tm, n_j),
        in_specs=[
            pl.BlockSpec((tm, D_MODEL), lambda i, j: (i, 0)),
            pl.BlockSpec((D_MODEL, GMLP_TN), lambda i, j: (0, U_OFF // GMLP_TN + j)),
            pl.BlockSpec((D_MODEL, GMLP_TN), lambda i, j: (0, GV_OFF // GMLP_TN + j)),
            vec, vec,
            pl.BlockSpec((GMLP_GROUPS, CHUNK, CHUNK), lambda i, j: (0, 0, 0)),
            pl.BlockSpec((CHUNK, GMLP_GROUPS), lambda i, j: (0, 0)),
        ],
        out_specs=pl.BlockSpec((tm, GMLP_WIDTH), lambda i, j: (i, 0)),
        out_shape=jax.ShapeDtypeStruct((m, GMLP_WIDTH), BF16),
        scratch_shapes=[pltpu.VMEM((n_j, tm, GMLP_TN), F32), pltpu.VMEM((n_j, tm, GMLP_TN), F32)],
        compiler_params=pltpu.CompilerParams(
            dimension_semantics=("parallel", "arbitrary"),
            vmem_limit_bytes=_vmem_limit(pipelined, resident)),
        name="gmlp",
    )(h, w_in, w_in, lnw, lnb, w_s, b_s_t)


def _merge_kernel(x_ref, mod_ref, h_ref, attn_ref, gm_ref, wga_ref, wgb_ref, bg_ref, wba_ref, wbg_ref, wo_ref,
                  o_ref, acc_ref, *, gate_row):
    j = pl.program_id(1)

    @pl.when(j == 0)
    def _():
        acc_ref[...] = jnp.zeros_like(acc_ref)

    h = h_ref[...]
    ga = jax.nn.sigmoid(_dot(h, wga_ref[...]) + bg_ref[0:1, :])
    gb = jax.nn.sigmoid(_dot(h, wgb_ref[...]) + bg_ref[1:2, :])
    merged = ga * _dot(attn_ref[...], wba_ref[...]) + gb * _dot(gm_ref[...], wbg_ref[...])
    acc_ref[...] += _dot(merged.astype(BF16), wo_ref[...])

    @pl.when(j == pl.num_programs(1) - 1)
    def _():
        o_ref[...] = x_ref[...] + mod_ref[gate_row:gate_row + 1, :] * acc_ref[...]


def _merge_call(x, mods, h, attn, gm, w_in, b_gate, w_ba, w_bg, w_o, *, gate_row, tm):
    m = x.shape[0]
    n_j = D_MODEL // MERGE_TN
    row_bf = pl.BlockSpec((tm, D_MODEL), lambda i, j: (i, 0))
    row_f32 = pl.BlockSpec((tm, D_MODEL), lambda i, j: (i, 0))
    col_w = pl.BlockSpec((D_MODEL, MERGE_TN), lambda i, j: (0, j))
    pipelined = 2 * tm * D_MODEL * 4 + 3 * tm * D_MODEL * 2 + 5 * D_MODEL * MERGE_TN * 2
    resident = tm * D_MODEL * 4 + 8 * tm * MERGE_TN * 4 + (2 << 20)
    return pl.pallas_call(
        functools.partial(_merge_kernel, gate_row=gate_row),
        grid=(m // tm, n_j),
        in_specs=[
            row_f32,
            pl.BlockSpec((N_MOD, D_MODEL), lambda i, j: (0, 0)),
            row_bf, row_bf, row_bf,
            pl.BlockSpec((D_MODEL, MERGE_TN), lambda i, j: (0, GATE_OFF // MERGE_TN + j)),
            pl.BlockSpec((D_MODEL, MERGE_TN), lambda i, j: (0, (GATE_OFF + D_MODEL) // MERGE_TN + j)),
            pl.BlockSpec((2, MERGE_TN), lambda i, j: (0, j)),
            col_w, col_w,
            pl.BlockSpec((MERGE_TN, D_MODEL), lambda i, j: (j, 0)),
        ],
        out_specs=row_f32,
        out_shape=jax.ShapeDtypeStruct((m, D_MODEL), F32),
        scratch_shapes=[pltpu.VMEM((tm, D_MODEL), F32)],
        compiler_params=pltpu.CompilerParams(
            dimension_semantics=("parallel", "arbitrary"),
            vmem_limit_bytes=_vmem_limit(pipelined, resident)),
        name="merge",
    )(x, mods, h, attn, gm, w_in, w_in, b_gate, w_ba, w_bg, w_o)


def _rope_tables(n_rows):
    axis_dim = HEAD_DIM // 2
    row = jnp.broadcast_to(jnp.arange(n_rows, dtype=F32)[:, None], (n_rows, GRID_W)).reshape(-1)
    col = jnp.broadcast_to(jnp.arange(GRID_W, dtype=F32)[None, :], (n_rows, GRID_W)).reshape(-1)
    inv_freq = ROPE_THETA ** (-jnp.arange(0, axis_dim, 2, dtype=F32) / axis_dim)
    ang = jnp.concatenate([row[:, None] * inv_freq, col[:, None] * inv_freq], axis=-1)
    cos, sin = jnp.cos(ang), jnp.sin(ang)
    zero = jnp.zeros_like(sin)
    cos2 = jnp.repeat(cos, 2, axis=-1)
    sa = jnp.stack([-sin, zero], axis=-1).reshape(cos2.shape)
    sb = jnp.stack([zero, sin], axis=-1).reshape(cos2.shape)
    return cos2, sa, sb


def kernel(x, c, ctx, c_ctx, w_mod, b_mod, norm_w, w_ffn1_in, w_ffn1_out, w_ffn2_in, w_ffn2_out, w_in, b_gate,
           q_norm_w, k_norm_w, gmlp_ln_w, gmlp_ln_b, w_spatial, b_spatial, w_branch_attn, w_branch_gmlp, w_out,
           final_norm_w):
    batch, seq, d = x.shape
    assert batch == 1 and d == D_MODEL and seq == SEQ and seq % GRID_W == 0
    assert w_mod.shape[0] == 1 and ctx.shape == (1, CTX_LEN, D_MODEL) and w_in.shape[-1] == IN_W

    x0 = x[0]
    ctx0 = ctx[0]
    w1i, w1o = w_ffn1_in[0].astype(BF16), w_ffn1_out[0].astype(BF16)
    w2i, w2o = w_ffn2_in[0].astype(BF16), w_ffn2_out[0].astype(BF16)
    wi = w_in[0].astype(BF16)
    wba, wbg, wo = w_branch_attn[0].astype(BF16), w_branch_gmlp[0].astype(BF16), w_out[0].astype(BF16)
    ws = w_spatial[0].astype(BF16)
    nw = norm_w[0]

    c_rows = jnp.zeros((MOD_ROWS, D_MODEL), F32).at[0].set(c[0]).at[1].set(c_ctx)
    mods = _mod_call(c_rows, w_mod[0], b_mod).reshape(MOD_ROWS, N_MOD, D_MODEL)
    mx, mc = mods[0], mods[1]

    x1, hx = _ffn_call(x0, mx, nw[0:1], nw[1:2], w1i, w1o, mod_row=0, post_mod_row=3, tm=FFN_TM)
    _, hc = _ffn_call(ctx0, mc, nw[0:1], nw[1:2], w1i, w1o, mod_row=0, post_mod_row=3, tm=CTX_LEN)

    gq, gk = q_norm_w[0][None, :], k_norm_w[0][None, :]
    cos, sa, sb = _rope_tables(seq // GRID_W)
    qk, vt = _qkv_call(hx, wi, gq, gk, cos, sa, sb, first_tile=0, n_tiles=(Q_W + 2 * KV_W) // QKV_TN, tm=QKV_TM)
    ones, zeros = jnp.ones((CTX_LEN, HEAD_DIM), F32), jnp.zeros((CTX_LEN, HEAD_DIM), F32)
    ck, vct = _qkv_call(hc, wi, gq, gk, ones, zeros, zeros, first_tile=Q_W // QKV_TN, n_tiles=2 * KV_W // QKV_TN,
                        tm=CTX_LEN)
    attn = _attn_call(qk, vt, ck, vct)

    gm = _gmlp_call(hx, wi, gmlp_ln_w, gmlp_ln_b, ws, b_spatial[0].T, tm=GMLP_TM)
    x2 = _merge_call(x1, mx, hx, attn, gm, wi, b_gate[0], wba, wbg, wo, gate_row=5, tm=MERGE_TM)

    out = _ffn_call(x2, mx, nw[2:3], final_norm_w[None, :], w2i, w2o, mod_row=6, post_mod_row=None, tm=FFN_TM)
    return out[None]
```

```python
import functools
import math

import jax
import jax.numpy as jnp
from jax import lax
from jax.experimental import pallas as pl
from jax.experimental.pallas import tpu as pltpu

D_MODEL = 2048
SEQ = 8192
CTX_LEN = 256
GRID_W = 64
HEAD_DIM = 128
N_Q_HEADS = 16
N_KV_HEADS = 4
Q_PER_KV = N_Q_HEADS // N_KV_HEADS
ROPE_THETA = 10000.0
ATTN_SCALE = HEAD_DIM ** -0.5
GMLP_GROUPS = 16
GMLP_WIDTH = 2048
GROUP_DIM = GMLP_WIDTH // GMLP_GROUPS
CHUNK = 128
D_FF = 5632
MACARON_WEIGHT = 0.5
N_MOD = 9
EPS = 1e-6
LOG2_E = math.log2(math.e)

Q_W = N_Q_HEADS * HEAD_DIM
KV_W = N_KV_HEADS * HEAD_DIM
K_OFF = Q_W
V_OFF = K_OFF + KV_W
U_OFF = V_OFF + KV_W
GV_OFF = U_OFF + GMLP_WIDTH
GATE_OFF = GV_OFF + GMLP_WIDTH
IN_W = GATE_OFF + 2 * D_MODEL

V7X_VMEM_BYTES = 64 * 1024 * 1024
VMEM_CAP_BYTES = V7X_VMEM_BYTES - 6 * 1024 * 1024

MOD_ROWS = 16
MOD_TN = 1024
FFN_TM = 512
FFN_TF = 512
QKV_TM = 512
QKV_TN = KV_W
ATTN_TQ = 128
GMLP_TM = 512
GMLP_TN = 512
MERGE_TM = 512
MERGE_TN = 256

F32 = jnp.float32
BF16 = jnp.bfloat16


def _vmem_limit(pipelined_bytes, resident_bytes):
    return int(min(2 * pipelined_bytes + resident_bytes, VMEM_CAP_BYTES))


def _dot(a, b):
    return jnp.dot(a, b, preferred_element_type=F32)


def _rmsnorm(x, w):
    return x * lax.rsqrt(jnp.mean(x * x, axis=-1, keepdims=True) + EPS) * w


def _gelu_exact(x):
    return 0.5 * x * (1 + lax.erf(x * (2.0 ** -0.5)))


def _mod_kernel(c_ref, w_ref, b_ref, o_ref):
    sc = jax.nn.silu(c_ref[...])
    o_ref[...] = _dot(sc.astype(BF16), w_ref[...].astype(BF16)) + b_ref[...]


def _mod_call(c_rows, w_mod, b_mod):
    n = w_mod.shape[1]
    return pl.pallas_call(
        _mod_kernel,
        grid=(n // MOD_TN,),
        in_specs=[
            pl.BlockSpec((MOD_ROWS, D_MODEL), lambda j: (0, 0)),
            pl.BlockSpec((D_MODEL, MOD_TN), lambda j: (0, j)),
            pl.BlockSpec((1, MOD_TN), lambda j: (0, j)),
        ],
        out_specs=pl.BlockSpec((MOD_ROWS, MOD_TN), lambda j: (0, j)),
        out_shape=jax.ShapeDtypeStruct((MOD_ROWS, n), F32),
        compiler_params=pltpu.CompilerParams(
            dimension_semantics=("arbitrary",),
            vmem_limit_bytes=_vmem_limit(D_MODEL * MOD_TN * 4, D_MODEL * MOD_TN * 2 + (4 << 20))),
        name="mod",
    )(c_rows, w_mod, b_mod)


def _ffn_kernel(x_ref, mod_ref, nw_ref, pnw_ref, wa_ref, wb_ref, wo_ref, *refs, mod_row, post_mod_row):
    if post_mod_row is None:
        (o_ref, h_ref, acc_ref) = refs
    else:
        (y_ref, o_ref, h_ref, acc_ref) = refs
    j = pl.program_id(1)

    @pl.when(j == 0)
    def _():
        shift = mod_ref[mod_row:mod_row + 1, :]
        scale = mod_ref[mod_row + 1:mod_row + 2, :]
        h = _rmsnorm(x_ref[...], nw_ref[...]) * (1 + scale) + shift
        h_ref[...] = h.astype(BF16)
        acc_ref[...] = jnp.zeros_like(acc_ref)

    h = h_ref[...]
    a = _dot(h, wa_ref[...])
    b = _dot(h, wb_ref[...])
    act = jax.nn.silu(a) * b
    acc_ref[...] += _dot(act.astype(BF16), wo_ref[...])

    @pl.when(j == pl.num_programs(1) - 1)
    def _():
        gate = mod_ref[mod_row + 2:mod_row + 3, :]
        y = x_ref[...] + MACARON_WEIGHT * gate * acc_ref[...]
        yn = _rmsnorm(y, pnw_ref[...])
        if post_mod_row is None:
            o_ref[...] = yn
        else:
            y_ref[...] = y
            pshift = mod_ref[post_mod_row:post_mod_row + 1, :]
            pscale = mod_ref[post_mod_row + 1:post_mod_row + 2, :]
            o_ref[...] = (yn * (1 + pscale) + pshift).astype(BF16)


def _ffn_call(x, mods, nw, pnw, w_in, w_out, *, mod_row, post_mod_row, tm):
    m = x.shape[0]
    n_f = D_FF // FFN_TF
    row_f32 = pl.BlockSpec((tm, D_MODEL), lambda i, j: (i, 0))
    vec = pl.BlockSpec((1, D_MODEL), lambda i, j: (0, 0))
    if post_mod_row is None:
        out_shape = jax.ShapeDtypeStruct((m, D_MODEL), F32)
        out_specs = row_f32
        out_bytes = tm * D_MODEL * 4
    else:
        out_shape = (jax.ShapeDtypeStruct((m, D_MODEL), F32), jax.ShapeDtypeStruct((m, D_MODEL), BF16))
        out_specs = (row_f32, pl.BlockSpec((tm, D_MODEL), lambda i, j: (i, 0)))
        out_bytes = tm * D_MODEL * 6
    pipelined = tm * D_MODEL * 4 + out_bytes + 3 * D_MODEL * FFN_TF * 2
    resident = tm * D_MODEL * 6 + 6 * tm * FFN_TF * 4 + (2 << 20)
    return pl.pallas_call(
        functools.partial(_ffn_kernel, mod_row=mod_row, post_mod_row=post_mod_row),
        grid=(m // tm, n_f),
        in_specs=[
            row_f32,
            pl.BlockSpec((N_MOD, D_MODEL), lambda i, j: (0, 0)),
            vec,
            vec,
            pl.BlockSpec((None, D_MODEL, FFN_TF), lambda i, j: (j, 0, 0)),
            pl.BlockSpec((None, D_MODEL, FFN_TF), lambda i, j: (j + n_f, 0, 0)),
            pl.BlockSpec((FFN_TF, D_MODEL), lambda i, j: (j, 0)),
        ],
        out_specs=out_specs,
        out_shape=out_shape,
        scratch_shapes=[pltpu.VMEM((tm, D_MODEL), BF16), pltpu.VMEM((tm, D_MODEL), F32)],
        compiler_params=pltpu.CompilerParams(
            dimension_semantics=("parallel", "arbitrary"),
            vmem_limit_bytes=_vmem_limit(pipelined, resident)),
        name="ffn",
    )(x, mods, nw, pnw, w_in, w_in, w_out)


def _qk_kernel(h_ref, w_ref, gain_ref, scale_ref, cos_ref, sa_ref, sb_ref, o_ref, z_ref):
    @pl.when(pl.program_id(0) == 0)
    def _():
        z_ref[...] = jnp.zeros(z_ref.shape, F32)

    z = z_ref[...]
    gain = gain_ref[...] * scale_ref[...]
    cos, sa, sb = cos_ref[...], sa_ref[...], sb_ref[...]
    for hh in range(QKV_TN // HEAD_DIM):
        cols = slice(hh * HEAD_DIM, (hh + 1) * HEAD_DIM)
        y = _rmsnorm(z[:, cols], gain)
        y = y * cos + pltpu.roll(y, HEAD_DIM - 1, 1) * sa + pltpu.roll(y, 1, 1) * sb
        o_ref[:, cols] = y.astype(o_ref.dtype)
    z_ref[...] = _dot(h_ref[...], w_ref[...])


def _qk_call(h, w_tiles, gains, scales, cos, sa, sb, *, tm):
    m = h.shape[0]
    n_j = w_tiles.shape[0]
    n_steps = (m // tm) * n_j

    def cur(t):
        return jnp.minimum(t, n_steps - 1)

    def prev(t):
        return jnp.maximum(t - 1, 0)

    tab = pl.BlockSpec((tm, HEAD_DIM), lambda t: (prev(t) // n_j, 0))
    vec = pl.BlockSpec((None, 1, HEAD_DIM), lambda t: (prev(t) % n_j, 0, 0))
    pipelined = tm * D_MODEL * 2 + D_MODEL * QKV_TN * 2 + 3 * tm * HEAD_DIM * 4 + tm * QKV_TN * 2
    resident = 10 * tm * QKV_TN * 4
    return pl.pallas_call(
        _qk_kernel,
        grid=(n_steps + 1,),
        in_specs=[
            pl.BlockSpec((tm, D_MODEL), lambda t: (cur(t) // n_j, 0)),
            pl.BlockSpec((None, D_MODEL, QKV_TN), lambda t: (cur(t) % n_j, 0, 0)),
            vec, vec, tab, tab, tab,
        ],
        out_specs=pl.BlockSpec((tm, QKV_TN), lambda t: (prev(t) // n_j, prev(t) % n_j)),
        out_shape=jax.ShapeDtypeStruct((m, n_j * QKV_TN), BF16),
        scratch_shapes=[pltpu.VMEM((tm, QKV_TN), F32)],
        compiler_params=pltpu.CompilerParams(
            dimension_semantics=("arbitrary",),
            vmem_limit_bytes=_vmem_limit(pipelined, resident)),
        name="qk",
    )(h, w_tiles, gains, scales, cos, sa, sb)


def _vt_kernel(h_ref, w_ref, vt_ref):
    vt_ref[0] = _dot(h_ref[...], w_ref[...]).T.astype(vt_ref.dtype)


def _vt_call(h, w_v, *, tm):
    m = h.shape[0]
    pipelined = tm * D_MODEL * 2 + D_MODEL * KV_W * 2 + KV_W * tm * 2
    return pl.pallas_call(
        _vt_kernel,
        grid=(m // tm,),
        in_specs=[
            pl.BlockSpec((tm, D_MODEL), lambda i: (i, 0)),
            pl.BlockSpec((D_MODEL, KV_W), lambda i: (0, 0)),
        ],
        out_specs=pl.BlockSpec((1, KV_W, tm), lambda i: (i, 0, 0)),
        out_shape=jax.ShapeDtypeStruct((m // tm, KV_W, tm), BF16),
        compiler_params=pltpu.CompilerParams(
            dimension_semantics=("parallel",),
            vmem_limit_bytes=_vmem_limit(pipelined, 6 * tm * KV_W * 4)),
        name="vt",
    )(h, w_v)


def _attn_kernel(q_ref, k_ref, vt_ref, kc_ref, vct_ref, o_ref, s0_ref, s1_ref, m0_ref, m1_ref):
    i = pl.program_id(1)
    tq = q_ref.shape[0]
    c_len = kc_ref.shape[0]
    n_chunks = vt_ref.shape[0]
    tk = vt_ref.shape[2]

    @pl.when(jnp.logical_and(pl.program_id(0) == 0, i == 0))
    def _():
        s1_ref[...] = jnp.zeros(s1_ref.shape, F32)
        m1_ref[...] = jnp.zeros(m1_ref.shape, F32)

    def step(s_cur, m_cur, s_prev, m_prev_ref):
        q = jnp.concatenate([q_ref[:, g * HEAD_DIM:(g + 1) * HEAD_DIM] for g in range(Q_PER_KV)], axis=0)
        rows = q.shape[0]
        m_prev = m_prev_ref[...]

        def sublane_groups(a):
            return a.reshape(a.shape[0] // 8, 8, rows)

        def pass1_chunk(k, key_rows, m8):
            s = lax.dot_general(k, q, (((1,), (1,)), ((), ())), preferred_element_type=F32)
            s_cur[key_rows, :] = s
            part = jnp.max(sublane_groups(s), axis=0)
            return part if m8 is None else jnp.maximum(m8, part)

        def pass2_chunk(vt, key_rows, l8, acc):
            p = jnp.exp2(s_prev[key_rows, :] - m_prev)
            part = jnp.sum(sublane_groups(p), axis=0)
            pv = _dot(vt, p.astype(BF16))
            return (part, pv) if l8 is None else (l8 + part, acc + pv)

        ctx_rows = slice(0, c_len)
        m8 = pass1_chunk(kc_ref[...], ctx_rows, None)
        l8, acc = pass2_chunk(vct_ref[0], ctx_rows, None, None)
        for t in range(n_chunks):
            key_rows = slice(c_len + t * tk, c_len + (t + 1) * tk)
            m8 = pass1_chunk(k_ref[t * tk:(t + 1) * tk, :], key_rows, m8)
            l8, acc = pass2_chunk(vt_ref[t], key_rows, l8, acc)

        m_cur[...] = jnp.max(m8, axis=0, keepdims=True)
        o = (acc / jnp.sum(l8, axis=0, keepdims=True)).T
        for g in range(Q_PER_KV):
            o_ref[:, g * HEAD_DIM:(g + 1) * HEAD_DIM] = o[g * tq:(g + 1) * tq].astype(o_ref.dtype)

    @pl.when(i % 2 == 0)
    def _():
        step(s0_ref, m0_ref, s1_ref, m1_ref)

    @pl.when(i % 2 == 1)
    def _():
        step(s1_ref, m1_ref, s0_ref, m0_ref)


def _attn_call(qk, vt, ck, vct):
    s_len = qk.shape[0]
    c_len = ck.shape[0]
    n_chunks, _, tk = vt.shape
    n_tiles = s_len // ATTN_TQ
    gw = Q_PER_KV * HEAD_DIM
    rows = Q_PER_KV * ATTN_TQ
    pipelined = 2 * ATTN_TQ * gw * 2 + 2 * s_len * HEAD_DIM * 2 + 2 * c_len * HEAD_DIM * 2
    resident = 2 * (s_len + c_len) * rows * 4 + 8 * tk * rows * 4
    return pl.pallas_call(
        _attn_kernel,
        grid=(N_KV_HEADS, n_tiles + 1),
        in_specs=[
            pl.BlockSpec((ATTN_TQ, gw), lambda kv, i: (jnp.minimum(i, n_tiles - 1), kv)),
            pl.BlockSpec((s_len, HEAD_DIM), lambda kv, i: (0, K_OFF // HEAD_DIM + kv)),
            pl.BlockSpec((n_chunks, HEAD_DIM, tk), lambda kv, i: (0, kv, 0)),
            pl.BlockSpec((c_len, HEAD_DIM), lambda kv, i: (0, kv)),
            pl.BlockSpec((1, HEAD_DIM, c_len), lambda kv, i: (0, kv, 0)),
        ],
        out_specs=pl.BlockSpec((ATTN_TQ, gw), lambda kv, i: (jnp.maximum(i - 1, 0), kv)),
        out_shape=jax.ShapeDtypeStruct((s_len, Q_W), BF16),
        scratch_shapes=[
            pltpu.VMEM((c_len + s_len, rows), F32),
            pltpu.VMEM((c_len + s_len, rows), F32),
            pltpu.VMEM((1, rows), F32),
            pltpu.VMEM((1, rows), F32),
        ],
        compiler_params=pltpu.CompilerParams(
            dimension_semantics=("arbitrary", "arbitrary"),
            vmem_limit_bytes=_vmem_limit(pipelined, resident)),
        name="attn",
    )(qk, qk, vt, ck, vct)


def _gmlp_kernel(h_ref, wu_ref, wv_ref, lnw_ref, lnb_ref, ws_ref, bs_ref, o_ref, gu_ref, gv_ref):
    j = pl.program_id(1)
    n_j = GMLP_WIDTH // GMLP_TN
    h = h_ref[...]
    gu_ref[j] = _gelu_exact(_dot(h, wu_ref[...]))
    gv_ref[j] = _gelu_exact(_dot(h, wv_ref[...]))

    @pl.when(j == n_j - 1)
    def _():
        tm = h_ref.shape[0]
        total = gv_ref[0].sum(axis=-1, keepdims=True)
        for jj in range(1, n_j):
            total = total + gv_ref[jj].sum(axis=-1, keepdims=True)
        mu = total / GMLP_WIDTH
        sq = None
        for jj in range(n_j):
            xc = gv_ref[jj] - mu
            part = (xc * xc).sum(axis=-1, keepdims=True)
            sq = part if sq is None else sq + part
        rstd = lax.rsqrt(sq / GMLP_WIDTH + EPS)
        groups_per_tile = GMLP_TN // GROUP_DIM
        for jj in range(n_j):
            tile_cols = slice(jj * GMLP_TN, (jj + 1) * GMLP_TN)
            vn = ((gv_ref[jj] - mu) * rstd * lnw_ref[:, tile_cols] + lnb_ref[:, tile_cols]).astype(BF16)
            for gg in range(groups_per_tile):
                g = jj * groups_per_tile + gg
                cols = slice(gg * GROUP_DIM, (gg + 1) * GROUP_DIM)
                bias = bs_ref[:, g:g + 1]
                for c in range(tm // CHUNK):
                    rows = slice(c * CHUNK, (c + 1) * CHUNK)
                    mixed = _dot(ws_ref[g], vn[rows, cols]) + bias
                    o_ref[rows, g * GROUP_DIM:(g + 1) * GROUP_DIM] = (
                        gu_ref[jj, rows, cols] * mixed).astype(o_ref.dtype)


def _gmlp_call(h, w_uv, lnw, lnb, w_s, b_s_t, *, tm):
    m = h.shape[0]
    n_j = GMLP_WIDTH // GMLP_TN
    vec = pl.BlockSpec((1, GMLP_WIDTH), lambda i, j: (0, 0))
    pipelined = (tm * D_MODEL * 2 + 2 * D_MODEL * GMLP_TN * 2 + tm * GMLP_WIDTH * 2
                 + GMLP_GROUPS * CHUNK * CHUNK * 2 + CHUNK * 128 * 4)
    resident = 2 * tm * GMLP_WIDTH * 4 + 8 * tm * GMLP_TN * 4
    return pl.pallas_call(
        _gmlp_kernel,
        grid=(m // tm, n_j),
        in_specs=[
            pl.BlockSpec((tm, D_MODEL), lambda i, j: (i, 0)),
            pl.BlockSpec((None, D_MODEL, GMLP_TN), lambda i, j: (j, 0, 0)),
            pl.BlockSpec((None, D_MODEL, GMLP_TN), lambda i, j: (n_j + j, 0, 0)),
            vec, vec,
            pl.BlockSpec((GMLP_GROUPS, CHUNK, CHUNK), lambda i, j: (0, 0, 0)),
            pl.BlockSpec((CHUNK, GMLP_GROUPS), lambda i, j: (0, 0)),
        ],
        out_specs=pl.BlockSpec((tm, GMLP_WIDTH), lambda i, j: (i, 0)),
        out_shape=jax.ShapeDtypeStruct((m, GMLP_WIDTH), BF16),
        scratch_shapes=[pltpu.VMEM((n_j, tm, GMLP_TN), F32), pltpu.VMEM((n_j, tm, GMLP_TN), F32)],
        compiler_params=pltpu.CompilerParams(
            dimension_semantics=("parallel", "arbitrary"),
            vmem_limit_bytes=_vmem_limit(pipelined, resident)),
        name="gmlp",
    )(h, w_uv, w_uv, lnw, lnb, w_s, b_s_t)


def _merge_kernel(x_ref, mod_ref, h_ref, attn_ref, gm_ref, wga_ref, wgb_ref, bg_ref, wba_ref, wbg_ref, wo_ref,
                  o_ref, acc_ref, *, gate_row):
    j = pl.program_id(1)

    @pl.when(j == 0)
    def _():
        acc_ref[...] = jnp.zeros_like(acc_ref)

    h = h_ref[...]
    ga = jax.nn.sigmoid(_dot(h, wga_ref[...]) + bg_ref[0:1, :])
    gb = jax.nn.sigmoid(_dot(h, wgb_ref[...]) + bg_ref[1:2, :])
    merged = ga * _dot(attn_ref[...], wba_ref[...]) + gb * _dot(gm_ref[...], wbg_ref[...])
    acc_ref[...] += _dot(merged.astype(BF16), wo_ref[...])

    @pl.when(j == pl.num_programs(1) - 1)
    def _():
        o_ref[...] = x_ref[...] + mod_ref[gate_row:gate_row + 1, :] * acc_ref[...]


def _merge_call(x, mods, h, attn, gm, w_gate, b_gate, w_ba, w_bg, w_o, *, gate_row, tm):
    m = x.shape[0]
    n_j = D_MODEL // MERGE_TN
    row_bf = pl.BlockSpec((tm, D_MODEL), lambda i, j: (i, 0))
    row_f32 = pl.BlockSpec((tm, D_MODEL), lambda i, j: (i, 0))
    col_w = pl.BlockSpec((None, D_MODEL, MERGE_TN), lambda i, j: (j, 0, 0))
    pipelined = 2 * tm * D_MODEL * 4 + 3 * tm * D_MODEL * 2 + 5 * D_MODEL * MERGE_TN * 2
    resident = tm * D_MODEL * 4 + 8 * tm * MERGE_TN * 4 + (2 << 20)
    return pl.pallas_call(
        functools.partial(_merge_kernel, gate_row=gate_row),
        grid=(m // tm, n_j),
        in_specs=[
            row_f32,
            pl.BlockSpec((N_MOD, D_MODEL), lambda i, j: (0, 0)),
            row_bf, row_bf, row_bf,
            col_w,
            pl.BlockSpec((None, D_MODEL, MERGE_TN), lambda i, j: (n_j + j, 0, 0)),
            pl.BlockSpec((2, MERGE_TN), lambda i, j: (0, j)),
            col_w, col_w,
            pl.BlockSpec((MERGE_TN, D_MODEL), lambda i, j: (j, 0)),
        ],
        out_specs=row_f32,
        out_shape=jax.ShapeDtypeStruct((m, D_MODEL), F32),
        scratch_shapes=[pltpu.VMEM((tm, D_MODEL), F32)],
        compiler_params=pltpu.CompilerParams(
            dimension_semantics=("parallel", "arbitrary"),
            vmem_limit_bytes=_vmem_limit(pipelined, resident)),
        name="merge",
    )(x, mods, h, attn, gm, w_gate, w_gate, b_gate, w_ba, w_bg, w_o)


def _rope_tables(n_rows):
    axis_dim = HEAD_DIM // 2
    row = jnp.broadcast_to(jnp.arange(n_rows, dtype=F32)[:, None], (n_rows, GRID_W)).reshape(-1)
    col = jnp.broadcast_to(jnp.arange(GRID_W, dtype=F32)[None, :], (n_rows, GRID_W)).reshape(-1)
    inv_freq = ROPE_THETA ** (-jnp.arange(0, axis_dim, 2, dtype=F32) / axis_dim)
    ang = jnp.concatenate([row[:, None] * inv_freq, col[:, None] * inv_freq], axis=-1)
    cos, sin = jnp.cos(ang), jnp.sin(ang)
    zero = jnp.zeros_like(sin)
    cos2 = jnp.repeat(cos, 2, axis=-1)
    sa = jnp.stack([-sin, zero], axis=-1).reshape(cos2.shape)
    sb = jnp.stack([zero, sin], axis=-1).reshape(cos2.shape)
    return cos2, sa, sb


def _tile_major(w, tn):
    k, n = w.shape
    return w.astype(BF16).reshape(k, n // tn, tn).transpose(1, 0, 2)


def kernel(x, c, ctx, c_ctx, w_mod, b_mod, norm_w, w_ffn1_in, w_ffn1_out, w_ffn2_in, w_ffn2_out, w_in, b_gate,
           q_norm_w, k_norm_w, gmlp_ln_w, gmlp_ln_b, w_spatial, b_spatial, w_branch_attn, w_branch_gmlp, w_out,
           final_norm_w):
    batch, seq, d = x.shape
    assert batch == 1 and d == D_MODEL and seq == SEQ and seq % GRID_W == 0
    assert w_mod.shape[0] == 1 and ctx.shape == (1, CTX_LEN, D_MODEL) and w_in.shape[-1] == IN_W

    x0 = x[0]
    ctx0 = ctx[0]
    w1i, w1o = _tile_major(w_ffn1_in[0], FFN_TF), w_ffn1_out[0].astype(BF16)
    w2i, w2o = _tile_major(w_ffn2_in[0], FFN_TF), w_ffn2_out[0].astype(BF16)
    w_qk = _tile_major(w_in[0, :, :V_OFF], QKV_TN)
    w_v = w_in[0, :, V_OFF:U_OFF].astype(BF16)
    w_uv = _tile_major(w_in[0, :, U_OFF:GATE_OFF], GMLP_TN)
    w_gate = _tile_major(w_in[0, :, GATE_OFF:], MERGE_TN)
    wba, wbg = _tile_major(w_branch_attn[0], MERGE_TN), _tile_major(w_branch_gmlp[0], MERGE_TN)
    wo = w_out[0].astype(BF16)
    ws = w_spatial[0].astype(BF16)
    nw = norm_w[0]

    c_rows = jnp.zeros((MOD_ROWS, D_MODEL), F32).at[0].set(c[0]).at[1].set(c_ctx)
    mods = _mod_call(c_rows, w_mod[0], b_mod).reshape(MOD_ROWS, N_MOD, D_MODEL)
    mx, mc = mods[0], mods[1]

    x1, hx = _ffn_call(x0, mx, nw[0:1], nw[1:2], w1i, w1o, mod_row=0, post_mod_row=3, tm=FFN_TM)
    _, hc = _ffn_call(ctx0, mc, nw[0:1], nw[1:2], w1i, w1o, mod_row=0, post_mod_row=3, tm=CTX_LEN)

    n_q_tiles, n_k_tiles = Q_W // QKV_TN, KV_W // QKV_TN
    gains = jnp.stack([q_norm_w[0]] * n_q_tiles + [k_norm_w[0]] * n_k_tiles)[:, None, :]
    scales = jnp.stack([jnp.full((HEAD_DIM,), ATTN_SCALE * LOG2_E, F32)] * n_q_tiles
                       + [jnp.ones((HEAD_DIM,), F32)] * n_k_tiles)[:, None, :]
    cos, sa, sb = _rope_tables(seq // GRID_W)
    qk = _qk_call(hx, w_qk, gains, scales, cos, sa, sb, tm=QKV_TM)
    vt = _vt_call(hx, w_v, tm=QKV_TM)
    ones, zeros = jnp.ones((CTX_LEN, HEAD_DIM), F32), jnp.zeros((CTX_LEN, HEAD_DIM), F32)
    ck = _qk_call(hc, w_qk[n_q_tiles:], gains[n_q_tiles:], scales[n_q_tiles:], ones, zeros, zeros, tm=CTX_LEN)
    vct = _vt_call(hc, w_v, tm=CTX_LEN)
    attn = _attn_call(qk, vt, ck, vct)

    gm = _gmlp_call(hx, w_uv, gmlp_ln_w, gmlp_ln_b, ws, b_spatial[0].T, tm=GMLP_TM)
    x2 = _merge_call(x1, mx, hx, attn, gm, w_gate, b_gate[0], wba, wbg, wo, gate_row=5, tm=MERGE_TM)

    out = _ffn_call(x2, mx, nw[2:3], final_norm_w[None, :], w2i, w2o, mod_row=6, post_mod_row=None, tm=FFN_TM)
    return out[None]
```

```python
import functools
import math

import jax
import jax.numpy as jnp
from jax import lax
from jax.experimental import pallas as pl
from jax.experimental.pallas import tpu as pltpu

D_MODEL = 2048
SEQ = 8192
CTX_LEN = 256
GRID_W = 64
HEAD_DIM = 128
N_Q_HEADS = 16
N_KV_HEADS = 4
Q_PER_KV = N_Q_HEADS // N_KV_HEADS
ROPE_THETA = 10000.0
ATTN_SCALE = HEAD_DIM ** -0.5
GMLP_GROUPS = 16
GMLP_WIDTH = 2048
GROUP_DIM = GMLP_WIDTH // GMLP_GROUPS
CHUNK = 128
D_FF = 5632
MACARON_WEIGHT = 0.5
N_MOD = 9
EPS = 1e-6
LOG2_E = math.log2(math.e)

Q_W = N_Q_HEADS * HEAD_DIM
KV_W = N_KV_HEADS * HEAD_DIM
K_OFF = Q_W
V_OFF = K_OFF + KV_W
U_OFF = V_OFF + KV_W
GV_OFF = U_OFF + GMLP_WIDTH
GATE_OFF = GV_OFF + GMLP_WIDTH
IN_W = GATE_OFF + 2 * D_MODEL

V7X_VMEM_BYTES = 64 * 1024 * 1024
VMEM_CAP_BYTES = V7X_VMEM_BYTES - 6 * 1024 * 1024

BF16_SUBLANES = 16
MOD_ROWS = BF16_SUBLANES
CAST_BLOCK_BYTES = 512 * 1024
MOD_TN = 1024
FFN_TM = 512
FFN_TF = 512
QKV_TM = 512
QKV_TN = KV_W
ATTN_TQ = 128
GMLP_TM = 512
GMLP_TN = 512
MERGE_TM = 512
MERGE_TN = 256

F32 = jnp.float32
BF16 = jnp.bfloat16


def _vmem_limit(pipelined_bytes, resident_bytes):
    return int(min(2 * pipelined_bytes + resident_bytes, VMEM_CAP_BYTES))


def _dot(a, b):
    return jnp.dot(a, b, preferred_element_type=F32)


def _rmsnorm(x, w):
    return x * lax.rsqrt(jnp.mean(x * x, axis=-1, keepdims=True) + EPS) * w


def _gelu_exact(x):
    return 0.5 * x * (1 + lax.erf(x * (2.0 ** -0.5)))


def _mod_kernel(c_ref, w_ref, b_ref, o_ref):
    sc = jax.nn.silu(c_ref[...])
    o_ref[...] = _dot(sc.astype(BF16), w_ref[...].astype(BF16)) + b_ref[...]


def _mod_call(c_rows, w_mod, b_mod):
    n = w_mod.shape[1]
    return pl.pallas_call(
        _mod_kernel,
        grid=(n // MOD_TN,),
        in_specs=[
            pl.BlockSpec((MOD_ROWS, D_MODEL), lambda j: (0, 0)),
            pl.BlockSpec((D_MODEL, MOD_TN), lambda j: (0, j)),
            pl.BlockSpec((1, MOD_TN), lambda j: (0, j)),
        ],
        out_specs=pl.BlockSpec((MOD_ROWS, MOD_TN), lambda j: (0, j)),
        out_shape=jax.ShapeDtypeStruct((MOD_ROWS, n), F32),
        compiler_params=pltpu.CompilerParams(
            dimension_semantics=("arbitrary",),
            vmem_limit_bytes=_vmem_limit(D_MODEL * MOD_TN * 4, D_MODEL * MOD_TN * 2 + (4 << 20))),
        name="mod",
    )(c_rows, w_mod, b_mod)


def _cast_row_block(rows, cols):
    rb = BF16_SUBLANES
    while rows % (2 * rb) == 0 and 2 * rb * cols * 4 <= CAST_BLOCK_BYTES:
        rb *= 2
    return rb


def _cast_plan(weights, step_of, n_steps):
    n_blocks, in_specs, out_specs, out_shapes, nbytes = [], [], [], [], 0
    for w in weights:
        rows, cols = w.shape
        rb = _cast_row_block(rows, cols)
        nb = rows // rb
        assert nb <= n_steps

        def index(*g, nb=nb):
            return (jnp.minimum(step_of(*g), nb - 1), 0)

        n_blocks.append(nb)
        in_specs.append(pl.BlockSpec((rb, cols), index))
        out_specs.append(pl.BlockSpec((rb, cols), index))
        out_shapes.append(jax.ShapeDtypeStruct((rows, cols), BF16))
        nbytes += rb * cols * 6
    return tuple(n_blocks), in_specs, out_specs, out_shapes, nbytes


def _cast_rows(step, srcs, dsts, n_blocks):
    for src, dst, nb in zip(srcs, dsts, n_blocks):
        @pl.when(step < nb)
        def _():
            dst[...] = src[...].astype(dst.dtype)


def _ffn_kernel(x_ref, mod_ref, nw_ref, pnw_ref, wa_ref, wb_ref, wo_ref, *refs, mod_row, post_mod_row,
                emit_weights, cast_blocks):
    refs = list(refs)
    cast_srcs = [refs.pop(0) for _ in cast_blocks]
    y_ref = refs.pop(0) if post_mod_row is not None else None
    o_ref = refs.pop(0)
    w_outs = [refs.pop(0) for _ in range(3)] if emit_weights else None
    cast_dsts = [refs.pop(0) for _ in cast_blocks]
    h_ref, acc_ref = refs
    j = pl.program_id(1)
    _cast_rows(pl.program_id(0) * pl.num_programs(1) + j, cast_srcs, cast_dsts, cast_blocks)

    @pl.when(j == 0)
    def _():
        shift = mod_ref[mod_row:mod_row + 1, :]
        scale = mod_ref[mod_row + 1:mod_row + 2, :]
        h = _rmsnorm(x_ref[...], nw_ref[...]) * (1 + scale) + shift
        h_ref[...] = h.astype(BF16)
        acc_ref[...] = jnp.zeros_like(acc_ref)

    wa, wb, wo = wa_ref[...], wb_ref[...], wo_ref[...]
    if emit_weights:
        wa, wb, wo = wa.astype(BF16), wb.astype(BF16), wo.astype(BF16)
        for w_out, w in zip(w_outs, (wa, wb, wo)):
            w_out[...] = w
    h = h_ref[...]
    a = _dot(h, wa)
    b = _dot(h, wb)
    act = jax.nn.silu(a) * b
    acc_ref[...] += _dot(act.astype(BF16), wo)

    @pl.when(j == pl.num_programs(1) - 1)
    def _():
        gate = mod_ref[mod_row + 2:mod_row + 3, :]
        y = x_ref[...] + MACARON_WEIGHT * gate * acc_ref[...]
        yn = _rmsnorm(y, pnw_ref[...])
        if post_mod_row is None:
            o_ref[...] = yn
        else:
            y_ref[...] = y
            pshift = mod_ref[post_mod_row:post_mod_row + 1, :]
            pscale = mod_ref[post_mod_row + 1:post_mod_row + 2, :]
            o_ref[...] = (yn * (1 + pscale) + pshift).astype(BF16)


def _ffn_call(x, mods, nw, pnw, wa, wb, wo, *, b_tile0, mod_row, post_mod_row, tm, emit_weights=False,
              cast_srcs=()):
    m = x.shape[0]
    n_f = D_FF // FFN_TF
    n_i = m // tm
    assert not emit_weights or n_i == 1
    w_bytes = 4 if emit_weights else 2
    row_f32 = pl.BlockSpec((tm, D_MODEL), lambda i, j: (i, 0))
    vec = pl.BlockSpec((1, D_MODEL), lambda i, j: (0, 0))
    cast_blocks, cast_in, cast_out, cast_shapes, cast_bytes = _cast_plan(
        cast_srcs, lambda i, j: i * n_f + j, n_i * n_f)
    out_shape, out_specs = [], []
    if post_mod_row is not None:
        out_shape += [jax.ShapeDtypeStruct((m, D_MODEL), F32), jax.ShapeDtypeStruct((m, D_MODEL), BF16)]
        out_specs += [row_f32, pl.BlockSpec((tm, D_MODEL), lambda i, j: (i, 0))]
        out_bytes = tm * D_MODEL * 6
    else:
        out_shape += [jax.ShapeDtypeStruct((m, D_MODEL), F32)]
        out_specs += [row_f32]
        out_bytes = tm * D_MODEL * 4
    if emit_weights:
        out_shape += [jax.ShapeDtypeStruct((D_MODEL, D_FF), BF16)] * 2 + [jax.ShapeDtypeStruct((D_FF, D_MODEL), BF16)]
        out_specs += [pl.BlockSpec((D_MODEL, FFN_TF), lambda i, j: (0, j))] * 2
        out_specs += [pl.BlockSpec((FFN_TF, D_MODEL), lambda i, j: (j, 0))]
        out_bytes += 3 * D_MODEL * FFN_TF * 2
    pipelined = tm * D_MODEL * 4 + out_bytes + 3 * D_MODEL * FFN_TF * w_bytes + cast_bytes
    resident = tm * D_MODEL * 6 + 6 * tm * FFN_TF * 4 + (2 << 20)
    return pl.pallas_call(
        functools.partial(_ffn_kernel, mod_row=mod_row, post_mod_row=post_mod_row, emit_weights=emit_weights,
                          cast_blocks=cast_blocks),
        grid=(n_i, n_f),
        in_specs=[
            row_f32,
            pl.BlockSpec((N_MOD, D_MODEL), lambda i, j: (0, 0)),
            vec,
            vec,
            pl.BlockSpec((D_MODEL, FFN_TF), lambda i, j: (0, j)),
            pl.BlockSpec((D_MODEL, FFN_TF), lambda i, j: (0, j + b_tile0)),
            pl.BlockSpec((FFN_TF, D_MODEL), lambda i, j: (j, 0)),
        ] + cast_in,
        out_specs=out_specs + cast_out,
        out_shape=out_shape + cast_shapes,
        scratch_shapes=[pltpu.VMEM((tm, D_MODEL), BF16), pltpu.VMEM((tm, D_MODEL), F32)],
        compiler_params=pltpu.CompilerParams(
            dimension_semantics=("arbitrary", "arbitrary"),
            vmem_limit_bytes=_vmem_limit(pipelined, resident)),
        name="ffn",
    )(x, mods, nw, pnw, wa, wb, wo, *cast_srcs)


def _qk_kernel(h_ref, w_ref, gain_ref, scale_ref, cos_ref, sa_ref, sb_ref, o_ref, z_ref):
    @pl.when(pl.program_id(0) == 0)
    def _():
        z_ref[...] = jnp.zeros(z_ref.shape, F32)

    z = z_ref[...]
    gain = gain_ref[...] * scale_ref[...]
    cos, sa, sb = cos_ref[...], sa_ref[...], sb_ref[...]
    for hh in range(QKV_TN // HEAD_DIM):
        cols = slice(hh * HEAD_DIM, (hh + 1) * HEAD_DIM)
        y = _rmsnorm(z[:, cols], gain)
        y = y * cos + pltpu.roll(y, HEAD_DIM - 1, 1) * sa + pltpu.roll(y, 1, 1) * sb
        o_ref[:, cols] = y.astype(o_ref.dtype)
    z_ref[...] = _dot(h_ref[...], w_ref[...])


def _qk_call(h, w_in, gains, scales, cos, sa, sb, *, first_tile, tm):
    m = h.shape[0]
    n_j = gains.shape[0]
    n_steps = (m // tm) * n_j

    def cur(t):
        return jnp.minimum(t, n_steps - 1)

    def prev(t):
        return jnp.maximum(t - 1, 0)

    tab = pl.BlockSpec((tm, HEAD_DIM), lambda t: (prev(t) // n_j, 0))
    vec = pl.BlockSpec((None, 1, HEAD_DIM), lambda t: (prev(t) % n_j, 0, 0))
    pipelined = tm * D_MODEL * 2 + D_MODEL * QKV_TN * 2 + 3 * tm * HEAD_DIM * 4 + tm * QKV_TN * 2
    resident = 10 * tm * QKV_TN * 4
    return pl.pallas_call(
        _qk_kernel,
        grid=(n_steps + 1,),
        in_specs=[
            pl.BlockSpec((tm, D_MODEL), lambda t: (cur(t) // n_j, 0)),
            pl.BlockSpec((D_MODEL, QKV_TN), lambda t: (0, first_tile + cur(t) % n_j)),
            vec, vec, tab, tab, tab,
        ],
        out_specs=pl.BlockSpec((tm, QKV_TN), lambda t: (prev(t) // n_j, prev(t) % n_j)),
        out_shape=jax.ShapeDtypeStruct((m, n_j * QKV_TN), BF16),
        scratch_shapes=[pltpu.VMEM((tm, QKV_TN), F32)],
        compiler_params=pltpu.CompilerParams(
            dimension_semantics=("arbitrary",),
            vmem_limit_bytes=_vmem_limit(pipelined, resident)),
        name="qk",
    )(h, w_in, gains, scales, cos, sa, sb)


def _vt_kernel(h_ref, w_ref, vt_ref):
    vt_ref[0] = _dot(h_ref[...], w_ref[...]).T.astype(vt_ref.dtype)


def _vt_call(h, w_in, *, tm):
    m = h.shape[0]
    pipelined = tm * D_MODEL * 2 + D_MODEL * KV_W * 2 + KV_W * tm * 2
    return pl.pallas_call(
        _vt_kernel,
        grid=(m // tm,),
        in_specs=[
            pl.BlockSpec((tm, D_MODEL), lambda i: (i, 0)),
            pl.BlockSpec((D_MODEL, KV_W), lambda i: (0, V_OFF // KV_W)),
        ],
        out_specs=pl.BlockSpec((1, KV_W, tm), lambda i: (i, 0, 0)),
        out_shape=jax.ShapeDtypeStruct((m // tm, KV_W, tm), BF16),
        compiler_params=pltpu.CompilerParams(
            dimension_semantics=("parallel",),
            vmem_limit_bytes=_vmem_limit(pipelined, 6 * tm * KV_W * 4)),
        name="vt",
    )(h, w_in)


def _attn_kernel(q_ref, k_ref, vt_ref, kc_ref, vct_ref, *refs, cast_blocks):
    refs = list(refs)
    cast_srcs = [refs.pop(0) for _ in cast_blocks]
    o_ref = refs.pop(0)
    cast_dsts = [refs.pop(0) for _ in cast_blocks]
    s0_ref, s1_ref, m0_ref, m1_ref = refs
    i = pl.program_id(1)
    _cast_rows(pl.program_id(0) * pl.num_programs(1) + i, cast_srcs, cast_dsts, cast_blocks)
    tq = q_ref.shape[0]
    c_len = kc_ref.shape[0]
    n_chunks = vt_ref.shape[0]
    tk = vt_ref.shape[2]

    @pl.when(jnp.logical_and(pl.program_id(0) == 0, i == 0))
    def _():
        s1_ref[...] = jnp.zeros(s1_ref.shape, F32)
        m1_ref[...] = jnp.zeros(m1_ref.shape, F32)

    def step(s_cur, m_cur, s_prev, m_prev_ref):
        q = jnp.concatenate([q_ref[:, g * HEAD_DIM:(g + 1) * HEAD_DIM] for g in range(Q_PER_KV)], axis=0)
        rows = q.shape[0]
        m_prev = m_prev_ref[...]

        def sublane_groups(a):
            return a.reshape(a.shape[0] // 8, 8, rows)

        def pass1_chunk(k, key_rows, m8):
            s = lax.dot_general(k, q, (((1,), (1,)), ((), ())), preferred_element_type=F32)
            s_cur[key_rows, :] = s
            part = jnp.max(sublane_groups(s), axis=0)
            return part if m8 is None else jnp.maximum(m8, part)

        def pass2_chunk(vt, key_rows, l8, acc):
            p = jnp.exp2(s_prev[key_rows, :] - m_prev)
            part = jnp.sum(sublane_groups(p), axis=0)
            pv = _dot(vt, p.astype(BF16))
            return (part, pv) if l8 is None else (l8 + part, acc + pv)

        ctx_rows = slice(0, c_len)
        m8 = pass1_chunk(kc_ref[...], ctx_rows, None)
        l8, acc = pass2_chunk(vct_ref[0], ctx_rows, None, None)
        for t in range(n_chunks):
            key_rows = slice(c_len + t * tk, c_len + (t + 1) * tk)
            m8 = pass1_chunk(k_ref[t * tk:(t + 1) * tk, :], key_rows, m8)
            l8, acc = pass2_chunk(vt_ref[t], key_rows, l8, acc)

        m_cur[...] = jnp.max(m8, axis=0, keepdims=True)
        o = (acc / jnp.sum(l8, axis=0, keepdims=True)).T
        for g in range(Q_PER_KV):
            o_ref[:, g * HEAD_DIM:(g + 1) * HEAD_DIM] = o[g * tq:(g + 1) * tq].astype(o_ref.dtype)

    @pl.when(i % 2 == 0)
    def _():
        step(s0_ref, m0_ref, s1_ref, m1_ref)

    @pl.when(i % 2 == 1)
    def _():
        step(s1_ref, m1_ref, s0_ref, m0_ref)


def _attn_call(qk, vt, ck, vct, cast_srcs=()):
    s_len = qk.shape[0]
    c_len = ck.shape[0]
    n_chunks, _, tk = vt.shape
    n_tiles = s_len // ATTN_TQ
    gw = Q_PER_KV * HEAD_DIM
    rows = Q_PER_KV * ATTN_TQ
    cast_blocks, cast_in, cast_out, cast_shapes, cast_bytes = _cast_plan(
        cast_srcs, lambda kv, i: kv * (n_tiles + 1) + i, N_KV_HEADS * (n_tiles + 1))
    pipelined = 2 * ATTN_TQ * gw * 2 + 2 * s_len * HEAD_DIM * 2 + 2 * c_len * HEAD_DIM * 2 + cast_bytes
    resident = 2 * (s_len + c_len) * rows * 4 + 2 * tk * rows * 4
    return pl.pallas_call(
        functools.partial(_attn_kernel, cast_blocks=cast_blocks),
        grid=(N_KV_HEADS, n_tiles + 1),
        in_specs=[
            pl.BlockSpec((ATTN_TQ, gw), lambda kv, i: (jnp.minimum(i, n_tiles - 1), kv)),
            pl.BlockSpec((s_len, HEAD_DIM), lambda kv, i: (0, K_OFF // HEAD_DIM + kv)),
            pl.BlockSpec((n_chunks, HEAD_DIM, tk), lambda kv, i: (0, kv, 0)),
            pl.BlockSpec((c_len, HEAD_DIM), lambda kv, i: (0, kv)),
            pl.BlockSpec((1, HEAD_DIM, c_len), lambda kv, i: (0, kv, 0)),
        ] + cast_in,
        out_specs=[pl.BlockSpec((ATTN_TQ, gw), lambda kv, i: (jnp.maximum(i - 1, 0), kv))] + cast_out,
        out_shape=[jax.ShapeDtypeStruct((s_len, Q_W), BF16)] + cast_shapes,
        scratch_shapes=[
            pltpu.VMEM((c_len + s_len, rows), F32),
            pltpu.VMEM((c_len + s_len, rows), F32),
            pltpu.VMEM((1, rows), F32),
            pltpu.VMEM((1, rows), F32),
        ],
        compiler_params=pltpu.CompilerParams(
            dimension_semantics=("arbitrary", "arbitrary"),
            vmem_limit_bytes=_vmem_limit(pipelined, resident)),
        name="attn",
    )(qk, qk, vt, ck, vct, *cast_srcs)


def _gmlp_kernel(h_ref, wu_ref, wv_ref, lnw_ref, lnb_ref, ws_ref, bs_ref, o_ref, gu_ref, gv_ref):
    j = pl.program_id(1)
    n_j = GMLP_WIDTH // GMLP_TN
    h = h_ref[...]
    gu_ref[j] = _gelu_exact(_dot(h, wu_ref[...]))
    gv_ref[j] = _gelu_exact(_dot(h, wv_ref[...]))

    @pl.when(j == n_j - 1)
    def _():
        tm = h_ref.shape[0]
        total = gv_ref[0].sum(axis=-1, keepdims=True)
        for jj in range(1, n_j):
            total = total + gv_ref[jj].sum(axis=-1, keepdims=True)
        mu = total / GMLP_WIDTH
        sq = None
        for jj in range(n_j):
            xc = gv_ref[jj] - mu
            part = (xc * xc).sum(axis=-1, keepdims=True)
            sq = part if sq is None else sq + part
        rstd = lax.rsqrt(sq / GMLP_WIDTH + EPS)
        groups_per_tile = GMLP_TN // GROUP_DIM
        for jj in range(n_j):
            tile_cols = slice(jj * GMLP_TN, (jj + 1) * GMLP_TN)
            vn = ((gv_ref[jj] - mu) * rstd * lnw_ref[:, tile_cols] + lnb_ref[:, tile_cols]).astype(BF16)
            for gg in range(groups_per_tile):
                g = jj * groups_per_tile + gg
                cols = slice(gg * GROUP_DIM, (gg + 1) * GROUP_DIM)
                bias = bs_ref[:, g:g + 1]
                for c in range(tm // CHUNK):
                    rows = slice(c * CHUNK, (c + 1) * CHUNK)
                    mixed = _dot(ws_ref[g], vn[rows, cols]) + bias
                    o_ref[rows, g * GROUP_DIM:(g + 1) * GROUP_DIM] = (
                        gu_ref[jj, rows, cols] * mixed).astype(o_ref.dtype)


def _gmlp_call(h, w_in, lnw, lnb, w_s, b_s_t, *, tm):
    m = h.shape[0]
    n_j = GMLP_WIDTH // GMLP_TN
    vec = pl.BlockSpec((1, GMLP_WIDTH), lambda i, j: (0, 0))
    pipelined = (tm * D_MODEL * 2 + 2 * D_MODEL * GMLP_TN * 2 + tm * GMLP_WIDTH * 2
                 + GMLP_GROUPS * CHUNK * CHUNK * 2 + CHUNK * 128 * 4)
    resident = 2 * tm * GMLP_WIDTH * 4 + 8 * tm * GMLP_TN * 4
    return pl.pallas_call(
        _gmlp_kernel,
        grid=(m // tm, n_j),
        in_specs=[
            pl.BlockSpec((tm, D_MODEL), lambda i, j: (i, 0)),
            pl.BlockSpec((D_MODEL, GMLP_TN), lambda i, j: (0, U_OFF // GMLP_TN + j)),
            pl.BlockSpec((D_MODEL, GMLP_TN), lambda i, j: (0, GV_OFF // GMLP_TN + j)),
            vec, vec,
            pl.BlockSpec((GMLP_GROUPS, CHUNK, CHUNK), lambda i, j: (0, 0, 0)),
            pl.BlockSpec((CHUNK, GMLP_GROUPS), lambda i, j: (0, 0)),
        ],
        out_specs=pl.BlockSpec((tm, GMLP_WIDTH), lambda i, j: (i, 0)),
        out_shape=jax.ShapeDtypeStruct((m, GMLP_WIDTH), BF16),
        scratch_shapes=[pltpu.VMEM((n_j, tm, GMLP_TN), F32), pltpu.VMEM((n_j, tm, GMLP_TN), F32)],
        compiler_params=pltpu.CompilerParams(
            dimension_semantics=("parallel", "arbitrary"),
            vmem_limit_bytes=_vmem_limit(pipelined, resident)),
        name="gmlp",
    )(h, w_in, w_in, lnw, lnb, w_s, b_s_t)


def _merge_kernel(x_ref, mod_ref, h_ref, attn_ref, gm_ref, wga_ref, wgb_ref, bg_ref, wba_ref, wbg_ref, wo_ref,
                  o_ref, acc_ref, *, gate_row):
    j = pl.program_id(1)

    @pl.when(j == 0)
    def _():
        acc_ref[...] = jnp.zeros_like(acc_ref)

    h = h_ref[...]
    ga = jax.nn.sigmoid(_dot(h, wga_ref[...]) + bg_ref[0:1, :])
    gb = jax.nn.sigmoid(_dot(h, wgb_ref[...]) + bg_ref[1:2, :])
    merged = ga * _dot(attn_ref[...], wba_ref[...]) + gb * _dot(gm_ref[...], wbg_ref[...])
    acc_ref[...] += _dot(merged.astype(BF16), wo_ref[...])

    @pl.when(j == pl.num_programs(1) - 1)
    def _():
        o_ref[...] = x_ref[...] + mod_ref[gate_row:gate_row + 1, :] * acc_ref[...]


def _merge_call(x, mods, h, attn, gm, w_in, b_gate, w_ba, w_bg, w_o, *, gate_row, tm):
    m = x.shape[0]
    n_j = D_MODEL // MERGE_TN
    row_bf = pl.BlockSpec((tm, D_MODEL), lambda i, j: (i, 0))
    row_f32 = pl.BlockSpec((tm, D_MODEL), lambda i, j: (i, 0))
    col_w = pl.BlockSpec((D_MODEL, MERGE_TN), lambda i, j: (0, j))
    pipelined = 2 * tm * D_MODEL * 4 + 3 * tm * D_MODEL * 2 + 5 * D_MODEL * MERGE_TN * 2
    resident = tm * D_MODEL * 4 + 8 * tm * MERGE_TN * 4 + (2 << 20)
    return pl.pallas_call(
        functools.partial(_merge_kernel, gate_row=gate_row),
        grid=(m // tm, n_j),
        in_specs=[
            row_f32,
            pl.BlockSpec((N_MOD, D_MODEL), lambda i, j: (0, 0)),
            row_bf, row_bf, row_bf,
            pl.BlockSpec((D_MODEL, MERGE_TN), lambda i, j: (0, GATE_OFF // MERGE_TN + j)),
            pl.BlockSpec((D_MODEL, MERGE_TN), lambda i, j: (0, (GATE_OFF + D_MODEL) // MERGE_TN + j)),
            pl.BlockSpec((2, MERGE_TN), lambda i, j: (0, j)),
            col_w, col_w,
            pl.BlockSpec((MERGE_TN, D_MODEL), lambda i, j: (j, 0)),
        ],
        out_specs=row_f32,
        out_shape=jax.ShapeDtypeStruct((m, D_MODEL), F32),
        scratch_shapes=[pltpu.VMEM((tm, D_MODEL), F32)],
        compiler_params=pltpu.CompilerParams(
            dimension_semantics=("parallel", "arbitrary"),
            vmem_limit_bytes=_vmem_limit(pipelined, resident)),
        name="merge",
    )(x, mods, h, attn, gm, w_in, w_in, b_gate, w_ba, w_bg, w_o)


def _rope_tables(n_rows):
    axis_dim = HEAD_DIM // 2
    row = jnp.broadcast_to(jnp.arange(n_rows, dtype=F32)[:, None], (n_rows, GRID_W)).reshape(-1)
    col = jnp.broadcast_to(jnp.arange(GRID_W, dtype=F32)[None, :], (n_rows, GRID_W)).reshape(-1)
    inv_freq = ROPE_THETA ** (-jnp.arange(0, axis_dim, 2, dtype=F32) / axis_dim)
    ang = jnp.concatenate([row[:, None] * inv_freq, col[:, None] * inv_freq], axis=-1)
    cos, sin = jnp.cos(ang), jnp.sin(ang)
    zero = jnp.zeros_like(sin)
    cos2 = jnp.repeat(cos, 2, axis=-1)
    sa = jnp.stack([-sin, zero], axis=-1).reshape(cos2.shape)
    sb = jnp.stack([zero, sin], axis=-1).reshape(cos2.shape)
    return cos2, sa, sb


def kernel(x, c, ctx, c_ctx, w_mod, b_mod, norm_w, w_ffn1_in, w_ffn1_out, w_ffn2_in, w_ffn2_out, w_in, b_gate,
           q_norm_w, k_norm_w, gmlp_ln_w, gmlp_ln_b, w_spatial, b_spatial, w_branch_attn, w_branch_gmlp, w_out,
           final_norm_w):
    batch, seq, d = x.shape
    assert batch == 1 and d == D_MODEL and seq == SEQ and seq % GRID_W == 0
    assert w_mod.shape[0] == 1 and ctx.shape == (1, CTX_LEN, D_MODEL) and w_in.shape[-1] == IN_W

    x0 = x[0]
    ctx0 = ctx[0]
    ws = w_spatial[0].astype(BF16)
    nw = norm_w[0]
    n_f = D_FF // FFN_TF

    c_rows = jnp.zeros((MOD_ROWS, D_MODEL), F32).at[0].set(c[0]).at[1].set(c_ctx)
    mods = _mod_call(c_rows, w_mod[0], b_mod).reshape(MOD_ROWS, N_MOD, D_MODEL)
    mx, mc = mods[0], mods[1]

    _, hc, w1a, w1b, w1o = _ffn_call(ctx0, mc, nw[0:1], nw[1:2], w_ffn1_in[0], w_ffn1_in[0], w_ffn1_out[0],
                                     b_tile0=n_f, mod_row=0, post_mod_row=3, tm=CTX_LEN, emit_weights=True)
    x1, hx, wi = _ffn_call(x0, mx, nw[0:1], nw[1:2], w1a, w1b, w1o, b_tile0=0, mod_row=0, post_mod_row=3,
                           tm=FFN_TM, cast_srcs=(w_in[0],))

    n_q_tiles, n_k_tiles = Q_W // QKV_TN, KV_W // QKV_TN
    gains = jnp.stack([q_norm_w[0]] * n_q_tiles + [k_norm_w[0]] * n_k_tiles)[:, None, :]
    scales = jnp.stack([jnp.full((HEAD_DIM,), ATTN_SCALE * LOG2_E, F32)] * n_q_tiles
                       + [jnp.ones((HEAD_DIM,), F32)] * n_k_tiles)[:, None, :]
    cos, sa, sb = _rope_tables(seq // GRID_W)
    qk = _qk_call(hx, wi, gains, scales, cos, sa, sb, first_tile=0, tm=QKV_TM)
    vt = _vt_call(hx, wi, tm=QKV_TM)
    ones, zeros = jnp.ones((CTX_LEN, HEAD_DIM), F32), jnp.zeros((CTX_LEN, HEAD_DIM), F32)
    ck = _qk_call(hc, wi, gains[n_q_tiles:], scales[n_q_tiles:], ones, zeros, zeros, first_tile=n_q_tiles,
                  tm=CTX_LEN)
    vct = _vt_call(hc, wi, tm=CTX_LEN)
    attn, w2i, w2o, wba, wbg, wo = _attn_call(
        qk, vt, ck, vct,
        cast_srcs=(w_ffn2_in[0], w_ffn2_out[0], w_branch_attn[0], w_branch_gmlp[0], w_out[0]))

    gm = _gmlp_call(hx, wi, gmlp_ln_w, gmlp_ln_b, ws, b_spatial[0].T, tm=GMLP_TM)
    x2 = _merge_call(x1, mx, hx, attn, gm, wi, b_gate[0], wba, wbg, wo, gate_row=5, tm=MERGE_TM)

    (out,) = _ffn_call(x2, mx, nw[2:3], final_norm_w[None, :], w2i, w2i, w2o, b_tile0=n_f, mod_row=6,
                       post_mod_row=None, tm=FFN_TM)
    return out[None]
```

```python
import functools
import math

import jax
import jax.numpy as jnp
from jax import lax
from jax.experimental import pallas as pl
from jax.experimental.pallas import tpu as pltpu

D_MODEL = 2048
SEQ = 8192
CTX_LEN = 256
GRID_W = 64
HEAD_DIM = 128
N_Q_HEADS = 16
N_KV_HEADS = 4
Q_PER_KV = N_Q_HEADS // N_KV_HEADS
ROPE_THETA = 10000.0
ATTN_SCALE = HEAD_DIM ** -0.5
GMLP_GROUPS = 16
GMLP_WIDTH = 2048
GROUP_DIM = GMLP_WIDTH // GMLP_GROUPS
CHUNK = 128
D_FF = 5632
MACARON_WEIGHT = 0.5
N_MOD = 9
EPS = 1e-6
LOG2_E = math.log2(math.e)

Q_W = N_Q_HEADS * HEAD_DIM
KV_W = N_KV_HEADS * HEAD_DIM
K_OFF = Q_W
V_OFF = K_OFF + KV_W
U_OFF = V_OFF + KV_W
GV_OFF = U_OFF + GMLP_WIDTH
GATE_OFF = GV_OFF + GMLP_WIDTH
IN_W = GATE_OFF + 2 * D_MODEL

V7X_VMEM_BYTES = 64 * 1024 * 1024
VMEM_CAP_BYTES = V7X_VMEM_BYTES - 6 * 1024 * 1024

BF16_SUBLANES = 16
MOD_ROWS = BF16_SUBLANES
CAST_BLOCK_BYTES = 512 * 1024
MOD_TN = 1024
FFN_TM = 512
FFN_TF = 512
QKV_TM = 512
QKV_TN = KV_W
ATTN_TQ = 128
GMLP_TM = 512
GMLP_TN = 1024
MERGE_TM = 512
MERGE_TN = 512

F32 = jnp.float32
BF16 = jnp.bfloat16


def _vmem_limit(pipelined_bytes, resident_bytes):
    return int(min(2 * pipelined_bytes + resident_bytes, VMEM_CAP_BYTES))


def _dot(a, b):
    return jnp.dot(a, b, preferred_element_type=F32)


def _rmsnorm(x, w):
    return x * lax.rsqrt(jnp.mean(x * x, axis=-1, keepdims=True) + EPS) * w


def _gelu_exact(x):
    return 0.5 * x * (1 + lax.erf(x * (2.0 ** -0.5)))


def _mod_kernel(c_ref, w_ref, b_ref, o_ref):
    sc = jax.nn.silu(c_ref[...])
    o_ref[...] = _dot(sc.astype(BF16), w_ref[...].astype(BF16)) + b_ref[...]


def _mod_call(c_rows, w_mod, b_mod):
    n = w_mod.shape[1]
    return pl.pallas_call(
        _mod_kernel,
        grid=(n // MOD_TN,),
        in_specs=[
            pl.BlockSpec((MOD_ROWS, D_MODEL), lambda j: (0, 0)),
            pl.BlockSpec((D_MODEL, MOD_TN), lambda j: (0, j)),
            pl.BlockSpec((1, MOD_TN), lambda j: (0, j)),
        ],
        out_specs=pl.BlockSpec((MOD_ROWS, MOD_TN), lambda j: (0, j)),
        out_shape=jax.ShapeDtypeStruct((MOD_ROWS, n), F32),
        compiler_params=pltpu.CompilerParams(
            dimension_semantics=("arbitrary",),
            vmem_limit_bytes=_vmem_limit(D_MODEL * MOD_TN * 4, D_MODEL * MOD_TN * 2 + (4 << 20))),
        name="mod",
    )(c_rows, w_mod, b_mod)


def _cast_row_block(rows, cols):
    rb = BF16_SUBLANES
    while rows % (2 * rb) == 0 and 2 * rb * cols * 4 <= CAST_BLOCK_BYTES:
        rb *= 2
    return rb


def _cast_plan(weights, step_of, n_steps):
    n_blocks, in_specs, out_specs, out_shapes, nbytes = [], [], [], [], 0
    for w in weights:
        rows, cols = w.shape
        rb = _cast_row_block(rows, cols)
        nb = rows // rb
        assert nb <= n_steps

        def index(*g, nb=nb):
            return (jnp.minimum(step_of(*g), nb - 1), 0)

        n_blocks.append(nb)
        in_specs.append(pl.BlockSpec((rb, cols), index))
        out_specs.append(pl.BlockSpec((rb, cols), index))
        out_shapes.append(jax.ShapeDtypeStruct((rows, cols), BF16))
        nbytes += rb * cols * 6
    return tuple(n_blocks), in_specs, out_specs, out_shapes, nbytes


def _cast_rows(step, srcs, dsts, n_blocks):
    for src, dst, nb in zip(srcs, dsts, n_blocks):
        @pl.when(step < nb)
        def _():
            dst[...] = src[...].astype(dst.dtype)


def _ffn_kernel(x_ref, mod_ref, nw_ref, pnw_ref, wa_ref, wb_ref, wo_ref, *refs, mod_row, post_mod_row,
                emit_weights, cast_blocks):
    refs = list(refs)
    cast_srcs = [refs.pop(0) for _ in cast_blocks]
    y_ref = refs.pop(0) if post_mod_row is not None else None
    o_ref = refs.pop(0)
    w_outs = [refs.pop(0) for _ in range(3)] if emit_weights else None
    cast_dsts = [refs.pop(0) for _ in cast_blocks]
    h_ref, acc_ref = refs
    j = pl.program_id(1)
    _cast_rows(pl.program_id(0) * pl.num_programs(1) + j, cast_srcs, cast_dsts, cast_blocks)

    @pl.when(j == 0)
    def _():
        shift = mod_ref[mod_row:mod_row + 1, :]
        scale = mod_ref[mod_row + 1:mod_row + 2, :]
        h = _rmsnorm(x_ref[...], nw_ref[...]) * (1 + scale) + shift
        h_ref[...] = h.astype(BF16)
        acc_ref[...] = jnp.zeros_like(acc_ref)

    wa, wb, wo = wa_ref[...], wb_ref[...], wo_ref[...]
    if emit_weights:
        wa, wb, wo = wa.astype(BF16), wb.astype(BF16), wo.astype(BF16)
        for w_out, w in zip(w_outs, (wa, wb, wo)):
            w_out[...] = w
    h = h_ref[...]
    a = _dot(h, wa)
    b = _dot(h, wb)
    act = jax.nn.silu(a) * b
    acc_ref[...] += _dot(act.astype(BF16), wo)

    @pl.when(j == pl.num_programs(1) - 1)
    def _():
        gate = mod_ref[mod_row + 2:mod_row + 3, :]
        y = x_ref[...] + MACARON_WEIGHT * gate * acc_ref[...]
        yn = _rmsnorm(y, pnw_ref[...])
        if post_mod_row is None:
            o_ref[...] = yn
        else:
            y_ref[...] = y
            pshift = mod_ref[post_mod_row:post_mod_row + 1, :]
            pscale = mod_ref[post_mod_row + 1:post_mod_row + 2, :]
            o_ref[...] = (yn * (1 + pscale) + pshift).astype(BF16)


def _ffn_call(x, mods, nw, pnw, wa, wb, wo, *, b_tile0, mod_row, post_mod_row, tm, emit_weights=False,
              cast_srcs=()):
    m = x.shape[0]
    n_f = D_FF // FFN_TF
    n_i = m // tm
    assert not emit_weights or n_i == 1
    w_bytes = 4 if emit_weights else 2
    row_f32 = pl.BlockSpec((tm, D_MODEL), lambda i, j: (i, 0))
    vec = pl.BlockSpec((1, D_MODEL), lambda i, j: (0, 0))
    cast_blocks, cast_in, cast_out, cast_shapes, cast_bytes = _cast_plan(
        cast_srcs, lambda i, j: i * n_f + j, n_i * n_f)
    out_shape, out_specs = [], []
    if post_mod_row is not None:
        out_shape += [jax.ShapeDtypeStruct((m, D_MODEL), F32), jax.ShapeDtypeStruct((m, D_MODEL), BF16)]
        out_specs += [row_f32, pl.BlockSpec((tm, D_MODEL), lambda i, j: (i, 0))]
        out_bytes = tm * D_MODEL * 6
    else:
        out_shape += [jax.ShapeDtypeStruct((m, D_MODEL), F32)]
        out_specs += [row_f32]
        out_bytes = tm * D_MODEL * 4
    if emit_weights:
        out_shape += [jax.ShapeDtypeStruct((D_MODEL, D_FF), BF16)] * 2 + [jax.ShapeDtypeStruct((D_FF, D_MODEL), BF16)]
        out_specs += [pl.BlockSpec((D_MODEL, FFN_TF), lambda i, j: (0, j))] * 2
        out_specs += [pl.BlockSpec((FFN_TF, D_MODEL), lambda i, j: (j, 0))]
        out_bytes += 3 * D_MODEL * FFN_TF * 2
    pipelined = tm * D_MODEL * 4 + out_bytes + 3 * D_MODEL * FFN_TF * w_bytes + cast_bytes
    resident = tm * D_MODEL * 6 + 6 * tm * FFN_TF * 4 + (2 << 20)
    return pl.pallas_call(
        functools.partial(_ffn_kernel, mod_row=mod_row, post_mod_row=post_mod_row, emit_weights=emit_weights,
                          cast_blocks=cast_blocks),
        grid=(n_i, n_f),
        in_specs=[
            row_f32,
            pl.BlockSpec((N_MOD, D_MODEL), lambda i, j: (0, 0)),
            vec,
            vec,
            pl.BlockSpec((D_MODEL, FFN_TF), lambda i, j: (0, j)),
            pl.BlockSpec((D_MODEL, FFN_TF), lambda i, j: (0, j + b_tile0)),
            pl.BlockSpec((FFN_TF, D_MODEL), lambda i, j: (j, 0)),
        ] + cast_in,
        out_specs=out_specs + cast_out,
        out_shape=out_shape + cast_shapes,
        scratch_shapes=[pltpu.VMEM((tm, D_MODEL), BF16), pltpu.VMEM((tm, D_MODEL), F32)],
        compiler_params=pltpu.CompilerParams(
            dimension_semantics=("arbitrary", "arbitrary"),
            vmem_limit_bytes=_vmem_limit(pipelined, resident)),
        name="ffn",
    )(x, mods, nw, pnw, wa, wb, wo, *cast_srcs)


def _rope_tables(tile, tm, inv_freq):
    t = tile * tm + lax.broadcasted_iota(jnp.int32, (tm, HEAD_DIM), 0)
    lane = lax.broadcasted_iota(jnp.int32, (tm, HEAD_DIM), 1)
    row = lax.shift_right_logical(t, jnp.int32(GRID_W.bit_length() - 1))
    col = lax.bitwise_and(t, jnp.int32(GRID_W - 1))
    ang = jnp.where(lane < HEAD_DIM // 2, row, col).astype(F32) * inv_freq
    cos, sin = jnp.cos(ang), jnp.sin(ang)
    even = lax.bitwise_and(lane, jnp.int32(1)) == 0
    return cos, jnp.where(even, -sin, 0.0), jnp.where(even, 0.0, sin)


def _qk_kernel(h_ref, w_ref, gq_ref, gk_ref, freq_ref, o_ref, z_ref, *, n_q_heads, rope):
    t = pl.program_id(0)
    tm = h_ref.shape[0]

    @pl.when(t == 0)
    def _():
        z_ref[...] = jnp.zeros(z_ref.shape, F32)

    gain_q = gq_ref[...] * (ATTN_SCALE * LOG2_E)
    gain_k = gk_ref[...]
    if rope:
        cos, sa, sb = _rope_tables(jnp.maximum(t - 1, 0), tm, freq_ref[...])
    for head in range(z_ref.shape[1] // HEAD_DIM):
        cols = slice(head * HEAD_DIM, (head + 1) * HEAD_DIM)
        y = _rmsnorm(z_ref[:, cols], gain_q if head < n_q_heads else gain_k)
        if rope:
            y = y * cos + pltpu.roll(y, HEAD_DIM - 1, 1) * sa + pltpu.roll(y, 1, 1) * sb
        o_ref[:, cols] = y.astype(o_ref.dtype)
    z_ref[...] = _dot(h_ref[...], w_ref[...])


def _qk_call(h, w_in, gq, gk, inv_freq, *, first_col, n_q_heads, n_k_heads, rope, tm):
    m = h.shape[0]
    n_i = m // tm
    width = (n_q_heads + n_k_heads) * HEAD_DIM
    assert first_col % width == 0 and GRID_W & (GRID_W - 1) == 0
    vec = pl.BlockSpec((1, HEAD_DIM), lambda t: (0, 0))
    pipelined = tm * D_MODEL * 2 + tm * width * 2
    resident = D_MODEL * width * 2 + 3 * tm * width * 4 + 4 * tm * HEAD_DIM * 4
    return pl.pallas_call(
        functools.partial(_qk_kernel, n_q_heads=n_q_heads, rope=rope),
        grid=(n_i + 1,),
        in_specs=[
            pl.BlockSpec((tm, D_MODEL), lambda t: (jnp.minimum(t, n_i - 1), 0)),
            pl.BlockSpec((D_MODEL, width), lambda t: (0, first_col // width), pipeline_mode=pl.Buffered(1)),
            vec, vec, vec,
        ],
        out_specs=pl.BlockSpec((tm, width), lambda t: (jnp.maximum(t - 1, 0), 0)),
        out_shape=jax.ShapeDtypeStruct((m, width), BF16),
        scratch_shapes=[pltpu.VMEM((tm, width), F32)],
        compiler_params=pltpu.CompilerParams(
            dimension_semantics=("arbitrary",),
            vmem_limit_bytes=_vmem_limit(pipelined, resident)),
        name="qk",
    )(h, w_in, gq, gk, inv_freq)


def _vt_kernel(h_ref, w_ref, vt_ref):
    vt_ref[0] = _dot(h_ref[...], w_ref[...]).T.astype(vt_ref.dtype)


def _vt_call(h, w_in, *, tm):
    m = h.shape[0]
    pipelined = tm * D_MODEL * 2 + D_MODEL * KV_W * 2 + KV_W * tm * 2
    return pl.pallas_call(
        _vt_kernel,
        grid=(m // tm,),
        in_specs=[
            pl.BlockSpec((tm, D_MODEL), lambda i: (i, 0)),
            pl.BlockSpec((D_MODEL, KV_W), lambda i: (0, V_OFF // KV_W)),
        ],
        out_specs=pl.BlockSpec((1, KV_W, tm), lambda i: (i, 0, 0)),
        out_shape=jax.ShapeDtypeStruct((m // tm, KV_W, tm), BF16),
        compiler_params=pltpu.CompilerParams(
            dimension_semantics=("parallel",),
            vmem_limit_bytes=_vmem_limit(pipelined, 6 * tm * KV_W * 4)),
        name="vt",
    )(h, w_in)


def _attn_kernel(q_ref, k_ref, vt_ref, kc_ref, vct_ref, *refs, cast_blocks):
    refs = list(refs)
    cast_srcs = [refs.pop(0) for _ in cast_blocks]
    o_ref = refs.pop(0)
    cast_dsts = [refs.pop(0) for _ in cast_blocks]
    s0_ref, s1_ref, m0_ref, m1_ref = refs
    i = pl.program_id(1)
    _cast_rows(pl.program_id(0) * pl.num_programs(1) + i, cast_srcs, cast_dsts, cast_blocks)
    tq = q_ref.shape[0]
    c_len = kc_ref.shape[0]
    n_chunks = vt_ref.shape[0]
    tk = vt_ref.shape[2]

    @pl.when(jnp.logical_and(pl.program_id(0) == 0, i == 0))
    def _():
        s1_ref[...] = jnp.zeros(s1_ref.shape, F32)
        m1_ref[...] = jnp.zeros(m1_ref.shape, F32)

    def step(s_cur, m_cur, s_prev, m_prev_ref):
        q = jnp.concatenate([q_ref[:, g * HEAD_DIM:(g + 1) * HEAD_DIM] for g in range(Q_PER_KV)], axis=0)
        rows = q.shape[0]
        m_prev = m_prev_ref[...]

        def sublane_groups(a):
            return a.reshape(a.shape[0] // 8, 8, rows)

        def pass1_chunk(k, key_rows, m8):
            s = lax.dot_general(k, q, (((1,), (1,)), ((), ())), preferred_element_type=F32)
            s_cur[key_rows, :] = s
            part = jnp.max(sublane_groups(s), axis=0)
            return part if m8 is None else jnp.maximum(m8, part)

        def pass2_chunk(vt, key_rows, l8, acc):
            p = jnp.exp2(s_prev[key_rows, :] - m_prev)
            part = jnp.sum(sublane_groups(p), axis=0)
            pv = _dot(vt, p.astype(BF16))
            return (part, pv) if l8 is None else (l8 + part, acc + pv)

        ctx_rows = slice(0, c_len)
        m8 = pass1_chunk(kc_ref[...], ctx_rows, None)
        l8, acc = pass2_chunk(vct_ref[0], ctx_rows, None, None)
        for t in range(n_chunks):
            key_rows = slice(c_len + t * tk, c_len + (t + 1) * tk)
            m8 = pass1_chunk(k_ref[t * tk:(t + 1) * tk, :], key_rows, m8)
            l8, acc = pass2_chunk(vt_ref[t], key_rows, l8, acc)

        m_cur[...] = jnp.max(m8, axis=0, keepdims=True)
        o = (acc / jnp.sum(l8, axis=0, keepdims=True)).T
        for g in range(Q_PER_KV):
            o_ref[:, g * HEAD_DIM:(g + 1) * HEAD_DIM] = o[g * tq:(g + 1) * tq].astype(o_ref.dtype)

    @pl.when(i % 2 == 0)
    def _():
        step(s0_ref, m0_ref, s1_ref, m1_ref)

    @pl.when(i % 2 == 1)
    def _():
        step(s1_ref, m1_ref, s0_ref, m0_ref)


def _attn_call(qk, vt, ck, vct, cast_srcs=()):
    s_len = qk.shape[0]
    c_len = ck.shape[0]
    n_chunks, _, tk = vt.shape
    n_tiles = s_len // ATTN_TQ
    gw = Q_PER_KV * HEAD_DIM
    rows = Q_PER_KV * ATTN_TQ
    cast_blocks, cast_in, cast_out, cast_shapes, cast_bytes = _cast_plan(
        cast_srcs, lambda kv, i: kv * (n_tiles + 1) + i, N_KV_HEADS * (n_tiles + 1))
    pipelined = 2 * ATTN_TQ * gw * 2 + 2 * s_len * HEAD_DIM * 2 + 2 * c_len * HEAD_DIM * 2 + cast_bytes
    resident = 2 * (s_len + c_len) * rows * 4 + 2 * tk * rows * 4
    return pl.pallas_call(
        functools.partial(_attn_kernel, cast_blocks=cast_blocks),
        grid=(N_KV_HEADS, n_tiles + 1),
        in_specs=[
            pl.BlockSpec((ATTN_TQ, gw), lambda kv, i: (jnp.minimum(i, n_tiles - 1), kv)),
            pl.BlockSpec((s_len, HEAD_DIM), lambda kv, i: (0, K_OFF // HEAD_DIM + kv)),
            pl.BlockSpec((n_chunks, HEAD_DIM, tk), lambda kv, i: (0, kv, 0)),
            pl.BlockSpec((c_len, HEAD_DIM), lambda kv, i: (0, kv)),
            pl.BlockSpec((1, HEAD_DIM, c_len), lambda kv, i: (0, kv, 0)),
        ] + cast_in,
        out_specs=[pl.BlockSpec((ATTN_TQ, gw), lambda kv, i: (jnp.maximum(i - 1, 0), kv))] + cast_out,
        out_shape=[jax.ShapeDtypeStruct((s_len, Q_W), BF16)] + cast_shapes,
        scratch_shapes=[
            pltpu.VMEM((c_len + s_len, rows), F32),
            pltpu.VMEM((c_len + s_len, rows), F32),
            pltpu.VMEM((1, rows), F32),
            pltpu.VMEM((1, rows), F32),
        ],
        compiler_params=pltpu.CompilerParams(
            dimension_semantics=("arbitrary", "arbitrary"),
            vmem_limit_bytes=_vmem_limit(pipelined, resident)),
        name="attn",
    )(qk, qk, vt, ck, vct, *cast_srcs)


def _gmlp_kernel(h_ref, *refs):
    n_tiles = GMLP_WIDTH // GMLP_TN
    wu_refs, wv_refs = refs[:n_tiles], refs[n_tiles:2 * n_tiles]
    lnw_ref, lnb_ref, ws_ref, bs_ref, o_ref, gu_ref, gv_ref = refs[2 * n_tiles:]
    tm = h_ref.shape[0]

    @pl.when(pl.program_id(0) == 0)
    def _():
        gu_ref[...] = jnp.zeros(gu_ref.shape, F32)
        gv_ref[...] = jnp.zeros(gv_ref.shape, F32)

    tiles = [slice(jj * GMLP_TN, (jj + 1) * GMLP_TN) for jj in range(n_tiles)]
    total = sum(gv_ref[:, tc].sum(axis=-1, keepdims=True) for tc in tiles)
    mu = total / GMLP_WIDTH
    sq = sum(((gv_ref[:, tc] - mu) ** 2).sum(axis=-1, keepdims=True) for tc in tiles)
    rstd = lax.rsqrt(sq / GMLP_WIDTH + EPS)
    groups_per_tile = GMLP_TN // GROUP_DIM
    for jj, tc in enumerate(tiles):
        vn = ((gv_ref[:, tc] - mu) * rstd * lnw_ref[:, tc] + lnb_ref[:, tc]).astype(BF16)
        for gg in range(groups_per_tile):
            g = jj * groups_per_tile + gg
            cols = slice(g * GROUP_DIM, (g + 1) * GROUP_DIM)
            bias = bs_ref[:, g:g + 1]
            for c in range(tm // CHUNK):
                rows = slice(c * CHUNK, (c + 1) * CHUNK)
                mixed = _dot(ws_ref[g], vn[rows, gg * GROUP_DIM:(gg + 1) * GROUP_DIM]) + bias
                o_ref[rows, cols] = (gu_ref[rows, cols] * mixed).astype(o_ref.dtype)

    h = h_ref[...]
    for tc, wu_ref in zip(tiles, wu_refs):
        gu_ref[:, tc] = _gelu_exact(_dot(h, wu_ref[...]))
    for tc, wv_ref in zip(tiles, wv_refs):
        gv_ref[:, tc] = _gelu_exact(_dot(h, wv_ref[...]))


def _gmlp_call(h, w_in, lnw, lnb, w_s, b_s_t, *, tm):
    m = h.shape[0]
    n_i = m // tm
    n_tiles = GMLP_WIDTH // GMLP_TN
    vec = pl.BlockSpec((1, GMLP_WIDTH), lambda t: (0, 0))

    def w_tile(first_col, jj):
        return pl.BlockSpec((D_MODEL, GMLP_TN), lambda t: (0, first_col // GMLP_TN + jj),
                            pipeline_mode=pl.Buffered(1))

    pipelined = (tm * D_MODEL * 2 + tm * GMLP_WIDTH * 2 + GMLP_GROUPS * CHUNK * CHUNK * 2 + CHUNK * 128 * 4)
    resident = 2 * D_MODEL * GMLP_WIDTH * 2 + 2 * tm * GMLP_WIDTH * 4 + 6 * tm * GMLP_TN * 4
    return pl.pallas_call(
        _gmlp_kernel,
        grid=(n_i + 1,),
        in_specs=[pl.BlockSpec((tm, D_MODEL), lambda t: (jnp.minimum(t, n_i - 1), 0))]
        + [w_tile(U_OFF, jj) for jj in range(n_tiles)]
        + [w_tile(GV_OFF, jj) for jj in range(n_tiles)]
        + [
            vec, vec,
            pl.BlockSpec((GMLP_GROUPS, CHUNK, CHUNK), lambda t: (0, 0, 0)),
            pl.BlockSpec((CHUNK, GMLP_GROUPS), lambda t: (0, 0)),
        ],
        out_specs=pl.BlockSpec((tm, GMLP_WIDTH), lambda t: (jnp.maximum(t - 1, 0), 0)),
        out_shape=jax.ShapeDtypeStruct((m, GMLP_WIDTH), BF16),
        scratch_shapes=[pltpu.VMEM((tm, GMLP_WIDTH), F32), pltpu.VMEM((tm, GMLP_WIDTH), F32)],
        compiler_params=pltpu.CompilerParams(
            dimension_semantics=("arbitrary",),
            vmem_limit_bytes=_vmem_limit(pipelined, resident)),
        name="gmlp",
    )(h, *([w_in] * (2 * n_tiles)), lnw, lnb, w_s, b_s_t)


def _merge_kernel(x_ref, mod_ref, h_ref, attn_ref, gm_ref, wga_ref, wgb_ref, bg_ref, wba_ref, wbg_ref, wo_ref,
                  o_ref, *, gate_row):
    j = pl.program_id(1)

    @pl.when(j == 0)
    def _():
        o_ref[...] = jnp.zeros_like(o_ref)

    h = h_ref[...]
    ga = jax.nn.sigmoid(_dot(h, wga_ref[...]) + bg_ref[0:1, :])
    gb = jax.nn.sigmoid(_dot(h, wgb_ref[...]) + bg_ref[1:2, :])
    merged = ga * _dot(attn_ref[...], wba_ref[...]) + gb * _dot(gm_ref[...], wbg_ref[...])
    o_ref[...] += _dot(merged.astype(BF16), wo_ref[...])

    @pl.when(j == pl.num_programs(1) - 1)
    def _():
        o_ref[...] = x_ref[...] + mod_ref[gate_row:gate_row + 1, :] * o_ref[...]


def _merge_call(x, mods, h, attn, gm, w_in, b_gate, w_ba, w_bg, w_o, *, gate_row, tm):
    m = x.shape[0]
    n_j = D_MODEL // MERGE_TN
    row_bf = pl.BlockSpec((tm, D_MODEL), lambda i, j: (i, 0))
    row_f32 = pl.BlockSpec((tm, D_MODEL), lambda i, j: (i, 0))
    col_w = pl.BlockSpec((D_MODEL, MERGE_TN), lambda i, j: (0, j))
    pipelined = 2 * tm * D_MODEL * 4 + 3 * tm * D_MODEL * 2 + 5 * D_MODEL * MERGE_TN * 2
    resident = 8 * tm * MERGE_TN * 4 + (2 << 20)
    return pl.pallas_call(
        functools.partial(_merge_kernel, gate_row=gate_row),
        grid=(m // tm, n_j),
        in_specs=[
            row_f32,
            pl.BlockSpec((N_MOD, D_MODEL), lambda i, j: (0, 0)),
            row_bf, row_bf, row_bf,
            pl.BlockSpec((D_MODEL, MERGE_TN), lambda i, j: (0, GATE_OFF // MERGE_TN + j)),
            pl.BlockSpec((D_MODEL, MERGE_TN), lambda i, j: (0, (GATE_OFF + D_MODEL) // MERGE_TN + j)),
            pl.BlockSpec((2, MERGE_TN), lambda i, j: (0, j)),
            col_w, col_w,
            pl.BlockSpec((MERGE_TN, D_MODEL), lambda i, j: (j, 0)),
        ],
        out_specs=row_f32,
        out_shape=jax.ShapeDtypeStruct((m, D_MODEL), F32),
        compiler_params=pltpu.CompilerParams(
            dimension_semantics=("parallel", "arbitrary"),
            vmem_limit_bytes=_vmem_limit(pipelined, resident)),
        name="merge",
    )(x, mods, h, attn, gm, w_in, w_in, b_gate, w_ba, w_bg, w_o)


def kernel(x, c, ctx, c_ctx, w_mod, b_mod, norm_w, w_ffn1_in, w_ffn1_out, w_ffn2_in, w_ffn2_out, w_in, b_gate,
           q_norm_w, k_norm_w, gmlp_ln_w, gmlp_ln_b, w_spatial, b_spatial, w_branch_attn, w_branch_gmlp, w_out,
           final_norm_w):
    batch, seq, d = x.shape
    assert batch == 1 and d == D_MODEL and seq == SEQ and seq % GRID_W == 0
    assert w_mod.shape[0] == 1 and ctx.shape == (1, CTX_LEN, D_MODEL) and w_in.shape[-1] == IN_W

    x0 = x[0]
    ctx0 = ctx[0]
    ws = w_spatial[0].astype(BF16)
    nw = norm_w[0]
    n_f = D_FF // FFN_TF

    c_rows = jnp.zeros((MOD_ROWS, D_MODEL), F32).at[0].set(c[0]).at[1].set(c_ctx)
    mods = _mod_call(c_rows, w_mod[0], b_mod).reshape(MOD_ROWS, N_MOD, D_MODEL)
    mx, mc = mods[0], mods[1]

    _, hc, w1a, w1b, w1o = _ffn_call(ctx0, mc, nw[0:1], nw[1:2], w_ffn1_in[0], w_ffn1_in[0], w_ffn1_out[0],
                                     b_tile0=n_f, mod_row=0, post_mod_row=3, tm=CTX_LEN, emit_weights=True)
    x1, hx, wi = _ffn_call(x0, mx, nw[0:1], nw[1:2], w1a, w1b, w1o, b_tile0=0, mod_row=0, post_mod_row=3,
                           tm=FFN_TM, cast_srcs=(w_in[0],))

    gq, gk = q_norm_w[0][None, :], k_norm_w[0][None, :]
    axis_dim = HEAD_DIM // 2
    inv_freq = ROPE_THETA ** (-jnp.arange(0, axis_dim, 2, dtype=F32) / axis_dim)
    inv_freq_lanes = jnp.tile(jnp.repeat(inv_freq, 2), 2)[None, :]
    qk = _qk_call(hx, wi, gq, gk, inv_freq_lanes, first_col=0, n_q_heads=N_Q_HEADS, n_k_heads=N_KV_HEADS,
                  rope=True, tm=QKV_TM)
    vt = _vt_call(hx, wi, tm=QKV_TM)
    ck = _qk_call(hc, wi, gq, gk, inv_freq_lanes, first_col=K_OFF, n_q_heads=0, n_k_heads=N_KV_HEADS,
                  rope=False, tm=CTX_LEN)
    vct = _vt_call(hc, wi, tm=CTX_LEN)
    attn, w2i, w2o, wba, wbg, wo = _attn_call(
        qk, vt, ck, vct,
        cast_srcs=(w_ffn2_in[0], w_ffn2_out[0], w_branch_attn[0], w_branch_gmlp[0], w_out[0]))

    gm = _gmlp_call(hx, wi, gmlp_ln_w, gmlp_ln_b, ws, b_spatial[0].T, tm=GMLP_TM)
    x2 = _merge_call(x1, mx, hx, attn, gm, wi, b_gate[0], wba, wbg, wo, gate_row=5, tm=MERGE_TM)

    (out,) = _ffn_call(x2, mx, nw[2:3], final_norm_w[None, :], w2i, w2i, w2o, b_tile0=n_f, mod_row=6,
                       post_mod_row=None, tm=FFN_TM)
    return out[None]
```

```python
import functools
import math

import jax
import jax.numpy as jnp
from jax import lax
from jax.experimental import pallas as pl
from jax.experimental.pallas import tpu as pltpu

D_MODEL = 2048
SEQ = 8192
CTX_LEN = 256
GRID_W = 64
HEAD_DIM = 128
N_Q_HEADS = 16
N_KV_HEADS = 4
Q_PER_KV = N_Q_HEADS // N_KV_HEADS
ROPE_THETA = 10000.0
ATTN_SCALE = HEAD_DIM ** -0.5
GMLP_GROUPS = 16
GMLP_WIDTH = 2048
GROUP_DIM = GMLP_WIDTH // GMLP_GROUPS
CHUNK = 128
D_FF = 5632
MACARON_WEIGHT = 0.5
N_MOD = 9
EPS = 1e-6
LOG2_E = math.log2(math.e)

Q_W = N_Q_HEADS * HEAD_DIM
KV_W = N_KV_HEADS * HEAD_DIM
K_OFF = Q_W
V_OFF = K_OFF + KV_W
U_OFF = V_OFF + KV_W
GV_OFF = U_OFF + GMLP_WIDTH
GATE_OFF = GV_OFF + GMLP_WIDTH
IN_W = GATE_OFF + 2 * D_MODEL

V7X_VMEM_BYTES = 64 * 1024 * 1024
VMEM_CAP_BYTES = V7X_VMEM_BYTES - 6 * 1024 * 1024

BF16_SUBLANES = 16
MOD_ROWS = BF16_SUBLANES
CAST_BLOCK_BYTES = 512 * 1024
MOD_TN = 1024
FFN_TM = 512
FFN_TF = 512
QKV_TM = 512
ATTN_TQ = 128
GMLP_TM = 512
GMLP_TN = 1024
MERGE_TM = 512
MERGE_TN = 512

F32 = jnp.float32
BF16 = jnp.bfloat16


def _vmem_limit(pipelined_bytes, resident_bytes):
    return int(min(2 * pipelined_bytes + resident_bytes, VMEM_CAP_BYTES))


def _dot(a, b):
    return jnp.dot(a, b, preferred_element_type=F32)


def _rms_factor(x):
    return lax.rsqrt(jnp.mean(x * x, axis=-1, keepdims=True) + EPS)


def _rmsnorm(x, w):
    return x * _rms_factor(x) * w


def _gelu_exact(x):
    return 0.5 * x * (1 + lax.erf(x * (2.0 ** -0.5)))


def _mod_kernel(c_ref, w_ref, b_ref, o_ref):
    sc = jax.nn.silu(c_ref[...])
    o_ref[...] = _dot(sc.astype(BF16), w_ref[...].astype(BF16)) + b_ref[...]


def _mod_call(c_rows, w_mod, b_mod):
    n = w_mod.shape[1]
    return pl.pallas_call(
        _mod_kernel,
        grid=(n // MOD_TN,),
        in_specs=[
            pl.BlockSpec((MOD_ROWS, D_MODEL), lambda j: (0, 0)),
            pl.BlockSpec((D_MODEL, MOD_TN), lambda j: (0, j)),
            pl.BlockSpec((1, MOD_TN), lambda j: (0, j)),
        ],
        out_specs=pl.BlockSpec((MOD_ROWS, MOD_TN), lambda j: (0, j)),
        out_shape=jax.ShapeDtypeStruct((MOD_ROWS, n), F32),
        compiler_params=pltpu.CompilerParams(
            dimension_semantics=("arbitrary",),
            vmem_limit_bytes=_vmem_limit(D_MODEL * MOD_TN * 4, D_MODEL * MOD_TN * 2 + (4 << 20))),
        name="mod",
    )(c_rows, w_mod, b_mod)


def _cast_row_block(rows, cols):
    rb = BF16_SUBLANES
    while rows % (2 * rb) == 0 and 2 * rb * cols * 4 <= CAST_BLOCK_BYTES:
        rb *= 2
    return rb


def _cast_plan(weights, step_of, n_steps):
    n_blocks, in_specs, out_specs, out_shapes, nbytes = [], [], [], [], 0
    for w in weights:
        rows, cols = w.shape
        rb = _cast_row_block(rows, cols)
        nb = rows // rb
        assert nb <= n_steps

        def index(*g, nb=nb):
            return (jnp.minimum(step_of(*g), nb - 1), 0)

        n_blocks.append(nb)
        in_specs.append(pl.BlockSpec((rb, cols), index))
        out_specs.append(pl.BlockSpec((rb, cols), index))
        out_shapes.append(jax.ShapeDtypeStruct((rows, cols), BF16))
        nbytes += rb * cols * 6
    return tuple(n_blocks), in_specs, out_specs, out_shapes, nbytes


def _cast_rows(step, srcs, dsts, n_blocks):
    for src, dst, nb in zip(srcs, dsts, n_blocks):
        @pl.when(step < nb)
        def _():
            dst[...] = src[...].astype(dst.dtype)


def _ffn_kernel(x_ref, mod_ref, nw_ref, pnw_ref, wa_ref, wb_ref, wo_ref, *refs, mod_row, post_mod_row,
                emit_weights, cast_blocks):
    refs = list(refs)
    cast_srcs = [refs.pop(0) for _ in cast_blocks]
    y_ref = refs.pop(0) if post_mod_row is not None else None
    o_ref = refs.pop(0)
    w_outs = [refs.pop(0) for _ in range(3)] if emit_weights else None
    cast_dsts = [refs.pop(0) for _ in cast_blocks]
    h_ref, acc_ref, r_ref = refs
    j = pl.program_id(1)
    _cast_rows(pl.program_id(0) * pl.num_programs(1) + j, cast_srcs, cast_dsts, cast_blocks)

    @pl.when(j == 0)
    def _():
        shift = mod_ref[mod_row:mod_row + 1, :]
        scale = mod_ref[mod_row + 1:mod_row + 2, :]
        r_ref[...] = _rms_factor(x_ref[...])
        h = x_ref[...] * r_ref[...] * nw_ref[...] * (1 + scale) + shift
        h_ref[...] = h.astype(BF16)
        acc_ref[...] = jnp.zeros_like(acc_ref)

    wa, wb, wo = wa_ref[...], wb_ref[...], wo_ref[...]
    if emit_weights:
        wa, wb, wo = wa.astype(BF16), wb.astype(BF16), wo.astype(BF16)
        for w_out, w in zip(w_outs, (wa, wb, wo)):
            w_out[...] = w
    h = h_ref[...]
    a = _dot(h, wa)
    b = _dot(h, wb)
    act = jax.nn.silu(a) * b
    acc_ref[...] += _dot(act.astype(BF16), wo)

    @pl.when(j == pl.num_programs(1) - 1)
    def _():
        gate = mod_ref[mod_row + 2:mod_row + 3, :]
        stage_ref = acc_ref if post_mod_row is None else y_ref
        y = x_ref[...] + MACARON_WEIGHT * gate * acc_ref[...]
        stage_ref[...] = y
        r_ref[...] = _rms_factor(y)
        yn = stage_ref[...] * r_ref[...] * pnw_ref[...]
        if post_mod_row is None:
            o_ref[...] = yn
        else:
            pshift = mod_ref[post_mod_row:post_mod_row + 1, :]
            pscale = mod_ref[post_mod_row + 1:post_mod_row + 2, :]
            o_ref[...] = (yn * (1 + pscale) + pshift).astype(BF16)


def _ffn_call(x, mods, nw, pnw, wa, wb, wo, *, b_tile0, mod_row, post_mod_row, tm, emit_weights=False,
              cast_srcs=()):
    m = x.shape[0]
    n_f = D_FF // FFN_TF
    n_i = m // tm
    assert not emit_weights or n_i == 1
    w_bytes = 4 if emit_weights else 2
    row_f32 = pl.BlockSpec((tm, D_MODEL), lambda i, j: (i, 0))
    vec = pl.BlockSpec((1, D_MODEL), lambda i, j: (0, 0))
    cast_blocks, cast_in, cast_out, cast_shapes, cast_bytes = _cast_plan(
        cast_srcs, lambda i, j: i * n_f + j, n_i * n_f)
    out_shape, out_specs = [], []
    if post_mod_row is not None:
        out_shape += [jax.ShapeDtypeStruct((m, D_MODEL), F32), jax.ShapeDtypeStruct((m, D_MODEL), BF16)]
        out_specs += [row_f32, pl.BlockSpec((tm, D_MODEL), lambda i, j: (i, 0))]
        out_bytes = tm * D_MODEL * 6
    else:
        out_shape += [jax.ShapeDtypeStruct((m, D_MODEL), F32)]
        out_specs += [row_f32]
        out_bytes = tm * D_MODEL * 4
    if emit_weights:
        out_shape += [jax.ShapeDtypeStruct((D_MODEL, D_FF), BF16)] * 2 + [jax.ShapeDtypeStruct((D_FF, D_MODEL), BF16)]
        out_specs += [pl.BlockSpec((D_MODEL, FFN_TF), lambda i, j: (0, j))] * 2
        out_specs += [pl.BlockSpec((FFN_TF, D_MODEL), lambda i, j: (j, 0))]
        out_bytes += 3 * D_MODEL * FFN_TF * 2
    pipelined = tm * D_MODEL * 4 + out_bytes + 3 * D_MODEL * FFN_TF * w_bytes + cast_bytes
    resident = tm * D_MODEL * 6 + 6 * tm * FFN_TF * 4 + (2 << 20)
    return pl.pallas_call(
        functools.partial(_ffn_kernel, mod_row=mod_row, post_mod_row=post_mod_row, emit_weights=emit_weights,
                          cast_blocks=cast_blocks),
        grid=(n_i, n_f),
        in_specs=[
            row_f32,
            pl.BlockSpec((N_MOD, D_MODEL), lambda i, j: (0, 0)),
            vec,
            vec,
            pl.BlockSpec((D_MODEL, FFN_TF), lambda i, j: (0, j)),
            pl.BlockSpec((D_MODEL, FFN_TF), lambda i, j: (0, j + b_tile0)),
            pl.BlockSpec((FFN_TF, D_MODEL), lambda i, j: (j, 0)),
        ] + cast_in,
        out_specs=out_specs + cast_out,
        out_shape=out_shape + cast_shapes,
        scratch_shapes=[pltpu.VMEM((tm, D_MODEL), BF16), pltpu.VMEM((tm, D_MODEL), F32), pltpu.VMEM((tm, 1), F32)],
        compiler_params=pltpu.CompilerParams(
            dimension_semantics=("arbitrary", "arbitrary"),
            vmem_limit_bytes=_vmem_limit(pipelined, resident)),
        name="ffn",
    )(x, mods, nw, pnw, wa, wb, wo, *cast_srcs)


def _rope_tables(tile, tm, inv_freq):
    t = tile * tm + lax.broadcasted_iota(jnp.int32, (tm, HEAD_DIM), 0)
    lane = lax.broadcasted_iota(jnp.int32, (tm, HEAD_DIM), 1)
    row = lax.shift_right_logical(t, jnp.int32(GRID_W.bit_length() - 1))
    col = lax.bitwise_and(t, jnp.int32(GRID_W - 1))
    ang = jnp.where(lane < HEAD_DIM // 2, row, col).astype(F32) * inv_freq
    cos, sin = jnp.cos(ang), jnp.sin(ang)
    even = lax.bitwise_and(lane, jnp.int32(1)) == 0
    return cos, jnp.where(even, -sin, 0.0), jnp.where(even, 0.0, sin)


def _qkv_kernel(h_ref, w_ref, gq_ref, gk_ref, freq_ref, o_ref, vt_ref, z_ref, *, n_q_heads, n_k_heads, rope):
    t = pl.program_id(0)
    tm = h_ref.shape[0]

    @pl.when(t == 0)
    def _():
        z_ref[...] = jnp.zeros(z_ref.shape, F32)

    gain_q = gq_ref[...] * (ATTN_SCALE * LOG2_E)
    gain_k = gk_ref[...]
    if rope:
        cos, sa, sb = _rope_tables(jnp.maximum(t - 1, 0), tm, freq_ref[...])
    for head in range(n_q_heads + n_k_heads):
        cols = slice(head * HEAD_DIM, (head + 1) * HEAD_DIM)
        y = _rmsnorm(z_ref[:, cols], gain_q if head < n_q_heads else gain_k)
        if rope:
            y = y * cos + pltpu.roll(y, HEAD_DIM - 1, 1) * sa + pltpu.roll(y, 1, 1) * sb
        o_ref[:, cols] = y.astype(o_ref.dtype)
    vt_ref[0] = z_ref[:, (n_q_heads + n_k_heads) * HEAD_DIM:].T.astype(vt_ref.dtype)
    z_ref[...] = _dot(h_ref[...], w_ref[...])


def _qkv_call(h, w_in, gq, gk, inv_freq, *, first_col, n_q_heads, rope, tm):
    m = h.shape[0]
    n_i = m // tm
    qk_width = (n_q_heads + N_KV_HEADS) * HEAD_DIM
    width = qk_width + KV_W
    assert first_col + width == U_OFF and first_col % width == 0 and GRID_W & (GRID_W - 1) == 0
    vec = pl.BlockSpec((1, HEAD_DIM), lambda t: (0, 0))
    pipelined = tm * D_MODEL * 2 + tm * width * 2
    resident = D_MODEL * width * 2 + 3 * tm * width * 4 + 4 * tm * HEAD_DIM * 4
    return pl.pallas_call(
        functools.partial(_qkv_kernel, n_q_heads=n_q_heads, n_k_heads=N_KV_HEADS, rope=rope),
        grid=(n_i + 1,),
        in_specs=[
            pl.BlockSpec((tm, D_MODEL), lambda t: (jnp.minimum(t, n_i - 1), 0)),
            pl.BlockSpec((D_MODEL, width), lambda t: (0, first_col // width), pipeline_mode=pl.Buffered(1)),
            vec, vec, vec,
        ],
        out_specs=(
            pl.BlockSpec((tm, qk_width), lambda t: (jnp.maximum(t - 1, 0), 0)),
            pl.BlockSpec((1, KV_W, tm), lambda t: (jnp.maximum(t - 1, 0), 0, 0)),
        ),
        out_shape=(
            jax.ShapeDtypeStruct((m, qk_width), BF16),
            jax.ShapeDtypeStruct((n_i, KV_W, tm), BF16),
        ),
        scratch_shapes=[pltpu.VMEM((tm, width), F32)],
        compiler_params=pltpu.CompilerParams(
            dimension_semantics=("arbitrary",),
            vmem_limit_bytes=_vmem_limit(pipelined, resident)),
        name="qkv",
    )(h, w_in, gq, gk, inv_freq)


def _attn_kernel(q_ref, k_ref, vt_ref, kc_ref, vct_ref, *refs, cast_blocks):
    refs = list(refs)
    cast_srcs = [refs.pop(0) for _ in cast_blocks]
    o_ref = refs.pop(0)
    cast_dsts = [refs.pop(0) for _ in cast_blocks]
    s0_ref, s1_ref, m0_ref, m1_ref = refs
    i = pl.program_id(1)
    _cast_rows(pl.program_id(0) * pl.num_programs(1) + i, cast_srcs, cast_dsts, cast_blocks)
    tq = q_ref.shape[0]
    c_len = kc_ref.shape[0]
    n_chunks = vt_ref.shape[0]
    tk = vt_ref.shape[2]

    @pl.when(jnp.logical_and(pl.program_id(0) == 0, i == 0))
    def _():
        s1_ref[...] = jnp.zeros(s1_ref.shape, F32)
        m1_ref[...] = jnp.zeros(m1_ref.shape, F32)

    def step(s_cur, m_cur, s_prev, m_prev_ref):
        q = jnp.concatenate([q_ref[:, g * HEAD_DIM:(g + 1) * HEAD_DIM] for g in range(Q_PER_KV)], axis=0)
        rows = q.shape[0]
        m_prev = m_prev_ref[...]

        def sublane_groups(a):
            return a.reshape(a.shape[0] // 8, 8, rows)

        def pass1_chunk(k, key_rows, m8):
            s = lax.dot_general(k, q, (((1,), (1,)), ((), ())), preferred_element_type=F32)
            s_cur[key_rows, :] = s
            part = jnp.max(sublane_groups(s), axis=0)
            return part if m8 is None else jnp.maximum(m8, part)

        def pass2_chunk(vt, key_rows, l8, acc):
            p = jnp.exp2(s_prev[key_rows, :] - m_prev)
            part = jnp.sum(sublane_groups(p), axis=0)
            pv = _dot(vt, p.astype(BF16))
            return (part, pv) if l8 is None else (l8 + part, acc + pv)

        ctx_rows = slice(0, c_len)
        m8 = pass1_chunk(kc_ref[...], ctx_rows, None)
        l8, acc = pass2_chunk(vct_ref[0], ctx_rows, None, None)
        for t in range(n_chunks):
            key_rows = slice(c_len + t * tk, c_len + (t + 1) * tk)
            m8 = pass1_chunk(k_ref[t * tk:(t + 1) * tk, :], key_rows, m8)
            l8, acc = pass2_chunk(vt_ref[t], key_rows, l8, acc)

        m_cur[...] = jnp.max(m8, axis=0, keepdims=True)
        o = (acc / jnp.sum(l8, axis=0, keepdims=True)).T
        for g in range(Q_PER_KV):
            o_ref[:, g * HEAD_DIM:(g + 1) * HEAD_DIM] = o[g * tq:(g + 1) * tq].astype(o_ref.dtype)

    @pl.when(i % 2 == 0)
    def _():
        step(s0_ref, m0_ref, s1_ref, m1_ref)

    @pl.when(i % 2 == 1)
    def _():
        step(s1_ref, m1_ref, s0_ref, m0_ref)


def _attn_call(qk, vt, ck, vct, cast_srcs=()):
    s_len = qk.shape[0]
    c_len = ck.shape[0]
    n_chunks, _, tk = vt.shape
    n_tiles = s_len // ATTN_TQ
    gw = Q_PER_KV * HEAD_DIM
    rows = Q_PER_KV * ATTN_TQ
    cast_blocks, cast_in, cast_out, cast_shapes, cast_bytes = _cast_plan(
        cast_srcs, lambda kv, i: kv * (n_tiles + 1) + i, N_KV_HEADS * (n_tiles + 1))
    pipelined = 2 * ATTN_TQ * gw * 2 + 2 * s_len * HEAD_DIM * 2 + 2 * c_len * HEAD_DIM * 2 + cast_bytes
    resident = 2 * (s_len + c_len) * rows * 4 + 2 * tk * rows * 4
    return pl.pallas_call(
        functools.partial(_attn_kernel, cast_blocks=cast_blocks),
        grid=(N_KV_HEADS, n_tiles + 1),
        in_specs=[
            pl.BlockSpec((ATTN_TQ, gw), lambda kv, i: (jnp.minimum(i, n_tiles - 1), kv)),
            pl.BlockSpec((s_len, HEAD_DIM), lambda kv, i: (0, K_OFF // HEAD_DIM + kv)),
            pl.BlockSpec((n_chunks, HEAD_DIM, tk), lambda kv, i: (0, kv, 0)),
            pl.BlockSpec((c_len, HEAD_DIM), lambda kv, i: (0, kv)),
            pl.BlockSpec((1, HEAD_DIM, c_len), lambda kv, i: (0, kv, 0)),
        ] + cast_in,
        out_specs=[pl.BlockSpec((ATTN_TQ, gw), lambda kv, i: (jnp.maximum(i - 1, 0), kv))] + cast_out,
        out_shape=[jax.ShapeDtypeStruct((s_len, Q_W), BF16)] + cast_shapes,
        scratch_shapes=[
            pltpu.VMEM((c_len + s_len, rows), F32),
            pltpu.VMEM((c_len + s_len, rows), F32),
            pltpu.VMEM((1, rows), F32),
            pltpu.VMEM((1, rows), F32),
        ],
        compiler_params=pltpu.CompilerParams(
            dimension_semantics=("arbitrary", "arbitrary"),
            vmem_limit_bytes=_vmem_limit(pipelined, resident)),
        name="attn",
    )(qk, qk, vt, ck, vct, *cast_srcs)


def _gmlp_kernel(h_ref, *refs):
    n_tiles = GMLP_WIDTH // GMLP_TN
    wu_refs, wv_refs = refs[:n_tiles], refs[n_tiles:2 * n_tiles]
    lnw_ref, lnb_ref, ws_ref, bs_ref, o_ref, gu_ref, gv_ref = refs[2 * n_tiles:]
    tm = h_ref.shape[0]

    @pl.when(pl.program_id(0) == 0)
    def _():
        gu_ref[...] = jnp.zeros(gu_ref.shape, F32)
        gv_ref[...] = jnp.zeros(gv_ref.shape, F32)

    tiles = [slice(jj * GMLP_TN, (jj + 1) * GMLP_TN) for jj in range(n_tiles)]
    total = sum(gv_ref[:, tc].sum(axis=-1, keepdims=True) for tc in tiles)
    mu = total / GMLP_WIDTH
    sq = sum(((gv_ref[:, tc] - mu) ** 2).sum(axis=-1, keepdims=True) for tc in tiles)
    rstd = lax.rsqrt(sq / GMLP_WIDTH + EPS)
    groups_per_tile = GMLP_TN // GROUP_DIM
    for jj, tc in enumerate(tiles):
        vn = ((gv_ref[:, tc] - mu) * rstd * lnw_ref[:, tc] + lnb_ref[:, tc]).astype(BF16)
        for gg in range(groups_per_tile):
            g = jj * groups_per_tile + gg
            cols = slice(g * GROUP_DIM, (g + 1) * GROUP_DIM)
            bias = bs_ref[:, g:g + 1]
            for c in range(tm // CHUNK):
                rows = slice(c * CHUNK, (c + 1) * CHUNK)
                mixed = _dot(ws_ref[g], vn[rows, gg * GROUP_DIM:(gg + 1) * GROUP_DIM]) + bias
                o_ref[rows, cols] = (gu_ref[rows, cols] * mixed).astype(o_ref.dtype)

    h = h_ref[...]
    for tc, wu_ref in zip(tiles, wu_refs):
        gu_ref[:, tc] = _gelu_exact(_dot(h, wu_ref[...]))
    for tc, wv_ref in zip(tiles, wv_refs):
        gv_ref[:, tc] = _gelu_exact(_dot(h, wv_ref[...]))


def _gmlp_call(h, w_in, lnw, lnb, w_s, b_s_t, *, tm):
    m = h.shape[0]
    n_i = m // tm
    n_tiles = GMLP_WIDTH // GMLP_TN
    vec = pl.BlockSpec((1, GMLP_WIDTH), lambda t: (0, 0))

    def w_tile(first_col, jj):
        return pl.BlockSpec((D_MODEL, GMLP_TN), lambda t: (0, first_col // GMLP_TN + jj),
                            pipeline_mode=pl.Buffered(1))

    pipelined = (tm * D_MODEL * 2 + tm * GMLP_WIDTH * 2 + GMLP_GROUPS * CHUNK * CHUNK * 2 + CHUNK * 128 * 4)
    resident = 2 * D_MODEL * GMLP_WIDTH * 2 + 2 * tm * GMLP_WIDTH * 4 + 6 * tm * GMLP_TN * 4
    return pl.pallas_call(
        _gmlp_kernel,
        grid=(n_i + 1,),
        in_specs=[pl.BlockSpec((tm, D_MODEL), lambda t: (jnp.minimum(t, n_i - 1), 0))]
        + [w_tile(U_OFF, jj) for jj in range(n_tiles)]
        + [w_tile(GV_OFF, jj) for jj in range(n_tiles)]
        + [
            vec, vec,
            pl.BlockSpec((GMLP_GROUPS, CHUNK, CHUNK), lambda t: (0, 0, 0)),
            pl.BlockSpec((CHUNK, GMLP_GROUPS), lambda t: (0, 0)),
        ],
        out_specs=pl.BlockSpec((tm, GMLP_WIDTH), lambda t: (jnp.maximum(t - 1, 0), 0)),
        out_shape=jax.ShapeDtypeStruct((m, GMLP_WIDTH), BF16),
        scratch_shapes=[pltpu.VMEM((tm, GMLP_WIDTH), F32), pltpu.VMEM((tm, GMLP_WIDTH), F32)],
        compiler_params=pltpu.CompilerParams(
            dimension_semantics=("arbitrary",),
            vmem_limit_bytes=_vmem_limit(pipelined, resident)),
        name="gmlp",
    )(h, *([w_in] * (2 * n_tiles)), lnw, lnb, w_s, b_s_t)


def _merge_kernel(x_ref, mod_ref, h_ref, attn_ref, gm_ref, wga_ref, wgb_ref, bg_ref, wba_ref, wbg_ref, wo_ref,
                  o_ref, *, gate_row):
    j = pl.program_id(1)

    @pl.when(j == 0)
    def _():
        o_ref[...] = jnp.zeros_like(o_ref)

    h = h_ref[...]
    ga = jax.nn.sigmoid(_dot(h, wga_ref[...]) + bg_ref[0:1, :])
    gb = jax.nn.sigmoid(_dot(h, wgb_ref[...]) + bg_ref[1:2, :])
    merged = ga * _dot(attn_ref[...], wba_ref[...]) + gb * _dot(gm_ref[...], wbg_ref[...])
    o_ref[...] += _dot(merged.astype(BF16), wo_ref[...])

    @pl.when(j == pl.num_programs(1) - 1)
    def _():
        o_ref[...] = x_ref[...] + mod_ref[gate_row:gate_row + 1, :] * o_ref[...]


def _merge_call(x, mods, h, attn, gm, w_in, b_gate, w_ba, w_bg, w_o, *, gate_row, tm):
    m = x.shape[0]
    n_j = D_MODEL // MERGE_TN
    row_bf = pl.BlockSpec((tm, D_MODEL), lambda i, j: (i, 0))
    row_f32 = pl.BlockSpec((tm, D_MODEL), lambda i, j: (i, 0))
    col_w = pl.BlockSpec((D_MODEL, MERGE_TN), lambda i, j: (0, j))
    pipelined = 2 * tm * D_MODEL * 4 + 3 * tm * D_MODEL * 2 + 5 * D_MODEL * MERGE_TN * 2
    resident = 8 * tm * MERGE_TN * 4 + (2 << 20)
    return pl.pallas_call(
        functools.partial(_merge_kernel, gate_row=gate_row),
        grid=(m // tm, n_j),
        in_specs=[
            row_f32,
            pl.BlockSpec((N_MOD, D_MODEL), lambda i, j: (0, 0)),
            row_bf, row_bf, row_bf,
            pl.BlockSpec((D_MODEL, MERGE_TN), lambda i, j: (0, GATE_OFF // MERGE_TN + j)),
            pl.BlockSpec((D_MODEL, MERGE_TN), lambda i, j: (0, (GATE_OFF + D_MODEL) // MERGE_TN + j)),
            pl.BlockSpec((2, MERGE_TN), lambda i, j: (0, j)),
            col_w, col_w,
            pl.BlockSpec((MERGE_TN, D_MODEL), lambda i, j: (j, 0)),
        ],
        out_specs=row_f32,
        out_shape=jax.ShapeDtypeStruct((m, D_MODEL), F32),
        compiler_params=pltpu.CompilerParams(
            dimension_semantics=("parallel", "arbitrary"),
            vmem_limit_bytes=_vmem_limit(pipelined, resident)),
        name="merge",
    )(x, mods, h, attn, gm, w_in, w_in, b_gate, w_ba, w_bg, w_o)


def kernel(x, c, ctx, c_ctx, w_mod, b_mod, norm_w, w_ffn1_in, w_ffn1_out, w_ffn2_in, w_ffn2_out, w_in, b_gate,
           q_norm_w, k_norm_w, gmlp_ln_w, gmlp_ln_b, w_spatial, b_spatial, w_branch_attn, w_branch_gmlp, w_out,
           final_norm_w):
    batch, seq, d = x.shape
    assert batch == 1 and d == D_MODEL and seq == SEQ and seq % GRID_W == 0
    assert w_mod.shape[0] == 1 and ctx.shape == (1, CTX_LEN, D_MODEL) and w_in.shape[-1] == IN_W

    x0 = x[0]
    ctx0 = ctx[0]
    ws = w_spatial[0].astype(BF16)
    nw = norm_w[0]
    n_f = D_FF // FFN_TF

    c_rows = jnp.zeros((MOD_ROWS, D_MODEL), F32).at[0].set(c[0]).at[1].set(c_ctx)
    mods = _mod_call(c_rows, w_mod[0], b_mod).reshape(MOD_ROWS, N_MOD, D_MODEL)
    mx, mc = mods[0], mods[1]

    _, hc, w1a, w1b, w1o = _ffn_call(ctx0, mc, nw[0:1], nw[1:2], w_ffn1_in[0], w_ffn1_in[0], w_ffn1_out[0],
                                     b_tile0=n_f, mod_row=0, post_mod_row=3, tm=CTX_LEN, emit_weights=True)
    x1, hx, wi = _ffn_call(x0, mx, nw[0:1], nw[1:2], w1a, w1b, w1o, b_tile0=0, mod_row=0, post_mod_row=3,
                           tm=FFN_TM, cast_srcs=(w_in[0],))

    gq, gk = q_norm_w[0][None, :], k_norm_w[0][None, :]
    axis_dim = HEAD_DIM // 2
    inv_freq = ROPE_THETA ** (-jnp.arange(0, axis_dim, 2, dtype=F32) / axis_dim)
    inv_freq_lanes = jnp.tile(jnp.repeat(inv_freq, 2), 2)[None, :]
    qk, vt = _qkv_call(hx, wi, gq, gk, inv_freq_lanes, first_col=0, n_q_heads=N_Q_HEADS, rope=True, tm=QKV_TM)
    ck, vct = _qkv_call(hc, wi, gq, gk, inv_freq_lanes, first_col=K_OFF, n_q_heads=0, rope=False, tm=CTX_LEN)
    attn, w2i, w2o, wba, wbg, wo = _attn_call(
        qk, vt, ck, vct,
        cast_srcs=(w_ffn2_in[0], w_ffn2_out[0], w_branch_attn[0], w_branch_gmlp[0], w_out[0]))

    gm = _gmlp_call(hx, wi, gmlp_ln_w, gmlp_ln_b, ws, b_spatial[0].T, tm=GMLP_TM)
    x2 = _merge_call(x1, mx, hx, attn, gm, wi, b_gate[0], wba, wbg, wo, gate_row=5, tm=MERGE_TM)

    (out,) = _ffn_call(x2, mx, nw[2:3], final_norm_w[None, :], w2i, w2i, w2o, b_tile0=n_f, mod_row=6,
                       post_mod_row=None, tm=FFN_TM)
    return out[None]
```

```python
import functools
import math

import jax
import jax.numpy as jnp
from jax import lax
from jax.experimental import pallas as pl
from jax.experimental.pallas import tpu as pltpu

D_MODEL = 2048
SEQ = 8192
CTX_LEN = 256
GRID_W = 64
HEAD_DIM = 128
N_Q_HEADS = 16
N_KV_HEADS = 4
Q_PER_KV = N_Q_HEADS // N_KV_HEADS
ROPE_THETA = 10000.0
ATTN_SCALE = HEAD_DIM ** -0.5
GMLP_GROUPS = 16
GMLP_WIDTH = 2048
GROUP_DIM = GMLP_WIDTH // GMLP_GROUPS
CHUNK = 128
D_FF = 5632
MACARON_WEIGHT = 0.5
N_MOD = 9
EPS = 1e-6
LOG2_E = math.log2(math.e)

Q_W = N_Q_HEADS * HEAD_DIM
KV_W = N_KV_HEADS * HEAD_DIM
K_OFF = Q_W
V_OFF = K_OFF + KV_W
U_OFF = V_OFF + KV_W
GV_OFF = U_OFF + GMLP_WIDTH
GATE_OFF = GV_OFF + GMLP_WIDTH
IN_W = GATE_OFF + 2 * D_MODEL

V7X_VMEM_BYTES = 64 * 1024 * 1024
VMEM_CAP_BYTES = V7X_VMEM_BYTES - 6 * 1024 * 1024

BF16_SUBLANES = 16
MOD_ROWS = BF16_SUBLANES
CAST_BLOCK_BYTES = 1024 * 1024
MOD_TN = 1024
FFN_TM = 512
FFN_TF = 512
QKV_TM = 512
ATTN_TQ = 128
GMLP_TM = 512
GMLP_TN = 1024
MERGE_TM = 512
MERGE_TN = 512

F32 = jnp.float32
BF16 = jnp.bfloat16


def _vmem_limit(pipelined_bytes, resident_bytes):
    return int(min(2 * pipelined_bytes + resident_bytes, VMEM_CAP_BYTES))


def _dot(a, b):
    return jnp.dot(a, b, preferred_element_type=F32)


def _rms_factor(x):
    return lax.rsqrt(jnp.mean(x * x, axis=-1, keepdims=True) + EPS)


def _rmsnorm(x, w):
    return x * _rms_factor(x) * w


def _gelu_exact(x):
    return 0.5 * x * (1 + lax.erf(x * (2.0 ** -0.5)))


def _mod_kernel(c_ref, w_ref, b_ref, o_ref):
    sc = jax.nn.silu(c_ref[...])
    o_ref[...] = _dot(sc.astype(BF16), w_ref[...].astype(BF16)) + b_ref[...]


def _mod_call(c_rows, w_mod, b_mod):
    n = w_mod.shape[1]
    return pl.pallas_call(
        _mod_kernel,
        grid=(n // MOD_TN,),
        in_specs=[
            pl.BlockSpec((MOD_ROWS, D_MODEL), lambda j: (0, 0)),
            pl.BlockSpec((D_MODEL, MOD_TN), lambda j: (0, j)),
            pl.BlockSpec((1, MOD_TN), lambda j: (0, j)),
        ],
        out_specs=pl.BlockSpec((MOD_ROWS, MOD_TN), lambda j: (0, j)),
        out_shape=jax.ShapeDtypeStruct((MOD_ROWS, n), F32),
        compiler_params=pltpu.CompilerParams(
            dimension_semantics=("arbitrary",),
            vmem_limit_bytes=_vmem_limit(D_MODEL * MOD_TN * 4, D_MODEL * MOD_TN * 2 + (4 << 20))),
        name="mod",
    )(c_rows, w_mod, b_mod)


def _cast_row_block(rows, cols):
    rb = BF16_SUBLANES
    while rows % (2 * rb) == 0 and 2 * rb * cols * 4 <= CAST_BLOCK_BYTES:
        rb *= 2
    return rb


def _cast_plan(weights, step_of, n_steps):
    n_blocks, in_specs, out_specs, out_shapes, nbytes = [], [], [], [], 0
    for w in weights:
        rows, cols = w.shape
        rb = _cast_row_block(rows, cols)
        nb = rows // rb
        assert nb <= n_steps

        def index(*g, nb=nb):
            return (jnp.minimum(step_of(*g), nb - 1), 0)

        n_blocks.append(nb)
        in_specs.append(pl.BlockSpec((rb, cols), index))
        out_specs.append(pl.BlockSpec((rb, cols), index))
        out_shapes.append(jax.ShapeDtypeStruct((rows, cols), BF16))
        nbytes += rb * cols * 6
    return tuple(n_blocks), in_specs, out_specs, out_shapes, nbytes


def _cast_rows(step, srcs, dsts, n_blocks):
    for src, dst, nb in zip(srcs, dsts, n_blocks):
        @pl.when(step < nb)
        def _():
            dst[...] = src[...].astype(dst.dtype)


def _ffn_kernel(x_ref, mod_ref, nw_ref, pnw_ref, wa_ref, wb_ref, wo_ref, *refs, mod_row, post_mod_row,
                emit_weights, cast_blocks):
    refs = list(refs)
    cast_srcs = [refs.pop(0) for _ in cast_blocks]
    y_ref = refs.pop(0) if post_mod_row is not None else None
    o_ref = refs.pop(0)
    w_outs = [refs.pop(0) for _ in range(3)] if emit_weights else None
    cast_dsts = [refs.pop(0) for _ in cast_blocks]
    h_ref, acc_ref, r_ref = refs
    j = pl.program_id(1)
    _cast_rows(pl.program_id(0) * pl.num_programs(1) + j, cast_srcs, cast_dsts, cast_blocks)

    @pl.when(j == 0)
    def _():
        shift = mod_ref[mod_row:mod_row + 1, :]
        scale = mod_ref[mod_row + 1:mod_row + 2, :]
        r_ref[...] = _rms_factor(x_ref[...])
        h = x_ref[...] * r_ref[...] * nw_ref[...] * (1 + scale) + shift
        h_ref[...] = h.astype(BF16)
        acc_ref[...] = jnp.zeros_like(acc_ref)

    wa, wb, wo = wa_ref[...], wb_ref[...], wo_ref[...]
    if emit_weights:
        wa, wb, wo = wa.astype(BF16), wb.astype(BF16), wo.astype(BF16)
        for w_out, w in zip(w_outs, (wa, wb, wo)):
            w_out[...] = w
    h = h_ref[...]
    a = _dot(h, wa)
    b = _dot(h, wb)
    act = jax.nn.silu(a) * b
    acc_ref[...] += _dot(act.astype(BF16), wo)

    @pl.when(j == pl.num_programs(1) - 1)
    def _():
        gate = mod_ref[mod_row + 2:mod_row + 3, :]
        stage_ref = acc_ref if post_mod_row is None else y_ref
        y = x_ref[...] + MACARON_WEIGHT * gate * acc_ref[...]
        stage_ref[...] = y
        r_ref[...] = _rms_factor(y)
        yn = stage_ref[...] * r_ref[...] * pnw_ref[...]
        if post_mod_row is None:
            o_ref[...] = yn
        else:
            pshift = mod_ref[post_mod_row:post_mod_row + 1, :]
            pscale = mod_ref[post_mod_row + 1:post_mod_row + 2, :]
            o_ref[...] = (yn * (1 + pscale) + pshift).astype(BF16)


def _ffn_call(x, mods, nw, pnw, wa, wb, wo, *, b_tile0, mod_row, post_mod_row, tm, emit_weights=False,
              cast_srcs=()):
    m = x.shape[0]
    n_f = D_FF // FFN_TF
    n_i = m // tm
    assert not emit_weights or n_i == 1
    w_bytes = 4 if emit_weights else 2
    row_f32 = pl.BlockSpec((tm, D_MODEL), lambda i, j: (i, 0))
    vec = pl.BlockSpec((1, D_MODEL), lambda i, j: (0, 0))
    cast_blocks, cast_in, cast_out, cast_shapes, cast_bytes = _cast_plan(
        cast_srcs, lambda i, j: i * n_f + j, n_i * n_f)
    out_shape, out_specs = [], []
    if post_mod_row is not None:
        out_shape += [jax.ShapeDtypeStruct((m, D_MODEL), F32), jax.ShapeDtypeStruct((m, D_MODEL), BF16)]
        out_specs += [row_f32, pl.BlockSpec((tm, D_MODEL), lambda i, j: (i, 0))]
        out_bytes = tm * D_MODEL * 6
    else:
        out_shape += [jax.ShapeDtypeStruct((m, D_MODEL), F32)]
        out_specs += [row_f32]
        out_bytes = tm * D_MODEL * 4
    if emit_weights:
        out_shape += [jax.ShapeDtypeStruct((D_MODEL, D_FF), BF16)] * 2 + [jax.ShapeDtypeStruct((D_FF, D_MODEL), BF16)]
        out_specs += [pl.BlockSpec((D_MODEL, FFN_TF), lambda i, j: (0, j))] * 2
        out_specs += [pl.BlockSpec((FFN_TF, D_MODEL), lambda i, j: (j, 0))]
        out_bytes += 3 * D_MODEL * FFN_TF * 2
    pipelined = tm * D_MODEL * 4 + out_bytes + 3 * D_MODEL * FFN_TF * w_bytes + cast_bytes
    resident = tm * D_MODEL * 6 + 6 * tm * FFN_TF * 4 + (2 << 20)
    return pl.pallas_call(
        functools.partial(_ffn_kernel, mod_row=mod_row, post_mod_row=post_mod_row, emit_weights=emit_weights,
                          cast_blocks=cast_blocks),
        grid=(n_i, n_f),
        in_specs=[
            row_f32,
            pl.BlockSpec((N_MOD, D_MODEL), lambda i, j: (0, 0)),
            vec,
            vec,
            pl.BlockSpec((D_MODEL, FFN_TF), lambda i, j: (0, j)),
            pl.BlockSpec((D_MODEL, FFN_TF), lambda i, j: (0, j + b_tile0)),
            pl.BlockSpec((FFN_TF, D_MODEL), lambda i, j: (j, 0)),
        ] + cast_in,
        out_specs=out_specs + cast_out,
        out_shape=out_shape + cast_shapes,
        scratch_shapes=[pltpu.VMEM((tm, D_MODEL), BF16), pltpu.VMEM((tm, D_MODEL), F32), pltpu.VMEM((tm, 1), F32)],
        compiler_params=pltpu.CompilerParams(
            dimension_semantics=("arbitrary", "arbitrary"),
            vmem_limit_bytes=_vmem_limit(pipelined, resident)),
        name="ffn",
    )(x, mods, nw, pnw, wa, wb, wo, *cast_srcs)


def _rope_tables(tile, tm, inv_freq):
    t = tile * tm + lax.broadcasted_iota(jnp.int32, (tm, HEAD_DIM), 0)
    lane = lax.broadcasted_iota(jnp.int32, (tm, HEAD_DIM), 1)
    row = lax.shift_right_logical(t, jnp.int32(GRID_W.bit_length() - 1))
    col = lax.bitwise_and(t, jnp.int32(GRID_W - 1))
    ang = jnp.where(lane < HEAD_DIM // 2, row, col).astype(F32) * inv_freq
    cos, sin = jnp.cos(ang), jnp.sin(ang)
    even = lax.bitwise_and(lane, jnp.int32(1)) == 0
    return cos, jnp.where(even, -sin, 0.0), jnp.where(even, 0.0, sin)


def _qkv_kernel(h_ref, w_ref, gq_ref, gk_ref, freq_ref, *refs, n_q_heads, n_k_heads, rope, cast_blocks):
    refs = list(refs)
    cast_srcs = [refs.pop(0) for _ in cast_blocks]
    o_ref, vt_ref = refs.pop(0), refs.pop(0)
    cast_dsts = [refs.pop(0) for _ in cast_blocks]
    (z_ref,) = refs
    t = pl.program_id(0)
    _cast_rows(t, cast_srcs, cast_dsts, cast_blocks)
    tm = h_ref.shape[0]

    @pl.when(t == 0)
    def _():
        z_ref[...] = jnp.zeros(z_ref.shape, F32)

    gain_q = gq_ref[...] * (ATTN_SCALE * LOG2_E)
    gain_k = gk_ref[...]
    if rope:
        cos, sa, sb = _rope_tables(jnp.maximum(t - 1, 0), tm, freq_ref[...])
    for head in range(n_q_heads + n_k_heads):
        cols = slice(head * HEAD_DIM, (head + 1) * HEAD_DIM)
        y = _rmsnorm(z_ref[:, cols], gain_q if head < n_q_heads else gain_k)
        if rope:
            y = y * cos + pltpu.roll(y, HEAD_DIM - 1, 1) * sa + pltpu.roll(y, 1, 1) * sb
        o_ref[:, cols] = y.astype(o_ref.dtype)
    vt_ref[0] = z_ref[:, (n_q_heads + n_k_heads) * HEAD_DIM:].T.astype(vt_ref.dtype)
    z_ref[...] = _dot(h_ref[...], w_ref[...])


def _qkv_call(h, w_in, gq, gk, inv_freq, *, first_col, n_q_heads, rope, tm, cast_srcs=()):
    m = h.shape[0]
    n_i = m // tm
    qk_width = (n_q_heads + N_KV_HEADS) * HEAD_DIM
    width = qk_width + KV_W
    assert first_col + width == U_OFF and first_col % width == 0 and GRID_W & (GRID_W - 1) == 0
    vec = pl.BlockSpec((1, HEAD_DIM), lambda t: (0, 0))
    cast_blocks, cast_in, cast_out, cast_shapes, cast_bytes = _cast_plan(cast_srcs, lambda t: t, n_i + 1)
    pipelined = tm * D_MODEL * 2 + tm * width * 2 + cast_bytes
    resident = D_MODEL * width * 2 + 3 * tm * width * 4 + 4 * tm * HEAD_DIM * 4
    return pl.pallas_call(
        functools.partial(_qkv_kernel, n_q_heads=n_q_heads, n_k_heads=N_KV_HEADS, rope=rope,
                          cast_blocks=cast_blocks),
        grid=(n_i + 1,),
        in_specs=[
            pl.BlockSpec((tm, D_MODEL), lambda t: (jnp.minimum(t, n_i - 1), 0)),
            pl.BlockSpec((D_MODEL, width), lambda t: (0, first_col // width), pipeline_mode=pl.Buffered(1)),
            vec, vec, vec,
        ] + cast_in,
        out_specs=[
            pl.BlockSpec((tm, qk_width), lambda t: (jnp.maximum(t - 1, 0), 0)),
            pl.BlockSpec((1, KV_W, tm), lambda t: (jnp.maximum(t - 1, 0), 0, 0)),
        ] + cast_out,
        out_shape=[
            jax.ShapeDtypeStruct((m, qk_width), BF16),
            jax.ShapeDtypeStruct((n_i, KV_W, tm), BF16),
        ] + cast_shapes,
        scratch_shapes=[pltpu.VMEM((tm, width), F32)],
        compiler_params=pltpu.CompilerParams(
            dimension_semantics=("arbitrary",),
            vmem_limit_bytes=_vmem_limit(pipelined, resident)),
        name="qkv",
    )(h, w_in, gq, gk, inv_freq, *cast_srcs)


def _attn_kernel(q_ref, k_ref, vt_ref, kc_ref, vct_ref, *refs, cast_blocks):
    refs = list(refs)
    cast_srcs = [refs.pop(0) for _ in cast_blocks]
    o_ref = refs.pop(0)
    cast_dsts = [refs.pop(0) for _ in cast_blocks]
    s0_ref, s1_ref, m0_ref, m1_ref = refs
    i = pl.program_id(1)
    _cast_rows(pl.program_id(0) * pl.num_programs(1) + i, cast_srcs, cast_dsts, cast_blocks)
    tq = q_ref.shape[0]
    c_len = kc_ref.shape[0]
    n_chunks = vt_ref.shape[0]
    tk = vt_ref.shape[2]

    @pl.when(jnp.logical_and(pl.program_id(0) == 0, i == 0))
    def _():
        s1_ref[...] = jnp.zeros(s1_ref.shape, F32)
        m1_ref[...] = jnp.zeros(m1_ref.shape, F32)

    def step(s_cur, m_cur, s_prev, m_prev_ref):
        q = jnp.concatenate([q_ref[:, g * HEAD_DIM:(g + 1) * HEAD_DIM] for g in range(Q_PER_KV)], axis=0)
        rows = q.shape[0]
        m_prev = m_prev_ref[...]

        def sublane_groups(a):
            return a.reshape(a.shape[0] // 8, 8, rows)

        def pass1_chunk(k, key_rows, m8):
            s = lax.dot_general(k, q, (((1,), (1,)), ((), ())), preferred_element_type=F32)
            s_cur[key_rows, :] = s
            part = jnp.max(sublane_groups(s), axis=0)
            return part if m8 is None else jnp.maximum(m8, part)

        def pass2_chunk(vt, key_rows, l8, acc):
            p = jnp.exp2(s_prev[key_rows, :] - m_prev)
            part = jnp.sum(sublane_groups(p), axis=0)
            pv = _dot(vt, p.astype(BF16))
            return (part, pv) if l8 is None else (l8 + part, acc + pv)

        ctx_rows = slice(0, c_len)
        m8 = pass1_chunk(kc_ref[...], ctx_rows, None)
        l8, acc = pass2_chunk(vct_ref[0], ctx_rows, None, None)
        for t in range(n_chunks):
            key_rows = slice(c_len + t * tk, c_len + (t + 1) * tk)
            m8 = pass1_chunk(k_ref[t * tk:(t + 1) * tk, :], key_rows, m8)
            l8, acc = pass2_chunk(vt_ref[t], key_rows, l8, acc)

        m_cur[...] = jnp.max(m8, axis=0, keepdims=True)
        o = (acc / jnp.sum(l8, axis=0, keepdims=True)).T
        for g in range(Q_PER_KV):
            o_ref[:, g * HEAD_DIM:(g + 1) * HEAD_DIM] = o[g * tq:(g + 1) * tq].astype(o_ref.dtype)

    @pl.when(i % 2 == 0)
    def _():
        step(s0_ref, m0_ref, s1_ref, m1_ref)

    @pl.when(i % 2 == 1)
    def _():
        step(s1_ref, m1_ref, s0_ref, m0_ref)


def _attn_call(qk, vt, ck, vct, cast_srcs=()):
    s_len = qk.shape[0]
    c_len = ck.shape[0]
    n_chunks, _, tk = vt.shape
    n_tiles = s_len // ATTN_TQ
    gw = Q_PER_KV * HEAD_DIM
    rows = Q_PER_KV * ATTN_TQ
    cast_blocks, cast_in, cast_out, cast_shapes, cast_bytes = _cast_plan(
        cast_srcs, lambda kv, i: kv * (n_tiles + 1) + i, N_KV_HEADS * (n_tiles + 1))
    pipelined = 2 * ATTN_TQ * gw * 2 + 2 * s_len * HEAD_DIM * 2 + 2 * c_len * HEAD_DIM * 2 + cast_bytes
    resident = 2 * (s_len + c_len) * rows * 4 + 2 * tk * rows * 4
    return pl.pallas_call(
        functools.partial(_attn_kernel, cast_blocks=cast_blocks),
        grid=(N_KV_HEADS, n_tiles + 1),
        in_specs=[
            pl.BlockSpec((ATTN_TQ, gw), lambda kv, i: (jnp.minimum(i, n_tiles - 1), kv)),
            pl.BlockSpec((s_len, HEAD_DIM), lambda kv, i: (0, K_OFF // HEAD_DIM + kv)),
            pl.BlockSpec((n_chunks, HEAD_DIM, tk), lambda kv, i: (0, kv, 0)),
            pl.BlockSpec((c_len, HEAD_DIM), lambda kv, i: (0, kv)),
            pl.BlockSpec((1, HEAD_DIM, c_len), lambda kv, i: (0, kv, 0)),
        ] + cast_in,
        out_specs=[pl.BlockSpec((ATTN_TQ, gw), lambda kv, i: (jnp.maximum(i - 1, 0), kv))] + cast_out,
        out_shape=[jax.ShapeDtypeStruct((s_len, Q_W), BF16)] + cast_shapes,
        scratch_shapes=[
            pltpu.VMEM((c_len + s_len, rows), F32),
            pltpu.VMEM((c_len + s_len, rows), F32),
            pltpu.VMEM((1, rows), F32),
            pltpu.VMEM((1, rows), F32),
        ],
        compiler_params=pltpu.CompilerParams(
            dimension_semantics=("arbitrary", "arbitrary"),
            vmem_limit_bytes=_vmem_limit(pipelined, resident)),
        name="attn",
    )(qk, qk, vt, ck, vct, *cast_srcs)


def _gmlp_kernel(h_ref, *refs):
    n_tiles = GMLP_WIDTH // GMLP_TN
    wu_refs, wv_refs = refs[:n_tiles], refs[n_tiles:2 * n_tiles]
    lnw_ref, lnb_ref, ws_ref, bs_ref, o_ref, gu_ref, gv_ref = refs[2 * n_tiles:]
    tm = h_ref.shape[0]

    @pl.when(pl.program_id(0) == 0)
    def _():
        gu_ref[...] = jnp.zeros(gu_ref.shape, F32)
        gv_ref[...] = jnp.zeros(gv_ref.shape, F32)

    tiles = [slice(jj * GMLP_TN, (jj + 1) * GMLP_TN) for jj in range(n_tiles)]
    total = sum(gv_ref[:, tc].sum(axis=-1, keepdims=True) for tc in tiles)
    mu = total / GMLP_WIDTH
    sq = sum(((gv_ref[:, tc] - mu) ** 2).sum(axis=-1, keepdims=True) for tc in tiles)
    rstd = lax.rsqrt(sq / GMLP_WIDTH + EPS)
    groups_per_tile = GMLP_TN // GROUP_DIM
    for jj, tc in enumerate(tiles):
        vn = ((gv_ref[:, tc] - mu) * rstd * lnw_ref[:, tc] + lnb_ref[:, tc]).astype(BF16)
        for gg in range(groups_per_tile):
            g = jj * groups_per_tile + gg
            cols = slice(g * GROUP_DIM, (g + 1) * GROUP_DIM)
            bias = bs_ref[:, g:g + 1]
            for c in range(tm // CHUNK):
                rows = slice(c * CHUNK, (c + 1) * CHUNK)
                mixed = _dot(ws_ref[g], vn[rows, gg * GROUP_DIM:(gg + 1) * GROUP_DIM]) + bias
                o_ref[rows, cols] = (gu_ref[rows, cols] * mixed).astype(o_ref.dtype)

    h = h_ref[...]
    for tc, wu_ref in zip(tiles, wu_refs):
        gu_ref[:, tc] = _gelu_exact(_dot(h, wu_ref[...]))
    for tc, wv_ref in zip(tiles, wv_refs):
        gv_ref[:, tc] = _gelu_exact(_dot(h, wv_ref[...]))


def _gmlp_call(h, w_in, lnw, lnb, w_s, b_s_t, *, tm):
    m = h.shape[0]
    n_i = m // tm
    n_tiles = GMLP_WIDTH // GMLP_TN
    vec = pl.BlockSpec((1, GMLP_WIDTH), lambda t: (0, 0))

    def w_tile(first_col, jj):
        return pl.BlockSpec((D_MODEL, GMLP_TN), lambda t: (0, first_col // GMLP_TN + jj),
                            pipeline_mode=pl.Buffered(1))

    pipelined = (tm * D_MODEL * 2 + tm * GMLP_WIDTH * 2 + GMLP_GROUPS * CHUNK * CHUNK * 2 + CHUNK * 128 * 4)
    resident = 2 * D_MODEL * GMLP_WIDTH * 2 + 2 * tm * GMLP_WIDTH * 4 + 6 * tm * GMLP_TN * 4
    return pl.pallas_call(
        _gmlp_kernel,
        grid=(n_i + 1,),
        in_specs=[pl.BlockSpec((tm, D_MODEL), lambda t: (jnp.minimum(t, n_i - 1), 0))]
        + [w_tile(U_OFF, jj) for jj in range(n_tiles)]
        + [w_tile(GV_OFF, jj) for jj in range(n_tiles)]
        + [
            vec, vec,
            pl.BlockSpec((GMLP_GROUPS, CHUNK, CHUNK), lambda t: (0, 0, 0)),
            pl.BlockSpec((CHUNK, GMLP_GROUPS), lambda t: (0, 0)),
        ],
        out_specs=pl.BlockSpec((tm, GMLP_WIDTH), lambda t: (jnp.maximum(t - 1, 0), 0)),
        out_shape=jax.ShapeDtypeStruct((m, GMLP_WIDTH), BF16),
        scratch_shapes=[pltpu.VMEM((tm, GMLP_WIDTH), F32), pltpu.VMEM((tm, GMLP_WIDTH), F32)],
        compiler_params=pltpu.CompilerParams(
            dimension_semantics=("arbitrary",),
            vmem_limit_bytes=_vmem_limit(pipelined, resident)),
        name="gmlp",
    )(h, *([w_in] * (2 * n_tiles)), lnw, lnb, w_s, b_s_t)


def _merge_kernel(x_ref, mod_ref, h_ref, attn_ref, gm_ref, wga_ref, wgb_ref, bg_ref, wba_ref, wbg_ref, wo_ref,
                  o_ref, *, gate_row):
    j = pl.program_id(1)

    @pl.when(j == 0)
    def _():
        o_ref[...] = jnp.zeros_like(o_ref)

    h = h_ref[...]
    ga = jax.nn.sigmoid(_dot(h, wga_ref[...]) + bg_ref[0:1, :])
    gb = jax.nn.sigmoid(_dot(h, wgb_ref[...]) + bg_ref[1:2, :])
    merged = ga * _dot(attn_ref[...], wba_ref[...]) + gb * _dot(gm_ref[...], wbg_ref[...])
    o_ref[...] += _dot(merged.astype(BF16), wo_ref[...])

    @pl.when(j == pl.num_programs(1) - 1)
    def _():
        o_ref[...] = x_ref[...] + mod_ref[gate_row:gate_row + 1, :] * o_ref[...]


def _merge_call(x, mods, h, attn, gm, w_in, b_gate, w_ba, w_bg, w_o, *, gate_row, tm):
    m = x.shape[0]
    n_j = D_MODEL // MERGE_TN
    row_bf = pl.BlockSpec((tm, D_MODEL), lambda i, j: (i, 0))
    row_f32 = pl.BlockSpec((tm, D_MODEL), lambda i, j: (i, 0))
    col_w = pl.BlockSpec((D_MODEL, MERGE_TN), lambda i, j: (0, j))
    pipelined = 2 * tm * D_MODEL * 4 + 3 * tm * D_MODEL * 2 + 5 * D_MODEL * MERGE_TN * 2
    resident = 8 * tm * MERGE_TN * 4 + (2 << 20)
    return pl.pallas_call(
        functools.partial(_merge_kernel, gate_row=gate_row),
        grid=(m // tm, n_j),
        in_specs=[
            row_f32,
            pl.BlockSpec((N_MOD, D_MODEL), lambda i, j: (0, 0)),
            row_bf, row_bf, row_bf,
            pl.BlockSpec((D_MODEL, MERGE_TN), lambda i, j: (0, GATE_OFF // MERGE_TN + j)),
            pl.BlockSpec((D_MODEL, MERGE_TN), lambda i, j: (0, (GATE_OFF + D_MODEL) // MERGE_TN + j)),
            pl.BlockSpec((2, MERGE_TN), lambda i, j: (0, j)),
            col_w, col_w,
            pl.BlockSpec((MERGE_TN, D_MODEL), lambda i, j: (j, 0)),
        ],
        out_specs=row_f32,
        out_shape=jax.ShapeDtypeStruct((m, D_MODEL), F32),
        compiler_params=pltpu.CompilerParams(
            dimension_semantics=("parallel", "arbitrary"),
            vmem_limit_bytes=_vmem_limit(pipelined, resident)),
        name="merge",
    )(x, mods, h, attn, gm, w_in, w_in, b_gate, w_ba, w_bg, w_o)


def kernel(x, c, ctx, c_ctx, w_mod, b_mod, norm_w, w_ffn1_in, w_ffn1_out, w_ffn2_in, w_ffn2_out, w_in, b_gate,
           q_norm_w, k_norm_w, gmlp_ln_w, gmlp_ln_b, w_spatial, b_spatial, w_branch_attn, w_branch_gmlp, w_out,
           final_norm_w):
    batch, seq, d = x.shape
    assert batch == 1 and d == D_MODEL and seq == SEQ and seq % GRID_W == 0
    assert w_mod.shape[0] == 1 and ctx.shape == (1, CTX_LEN, D_MODEL) and w_in.shape[-1] == IN_W

    x0 = x[0]
    ctx0 = ctx[0]
    ws = w_spatial[0].astype(BF16)
    nw = norm_w[0]
    n_f = D_FF // FFN_TF

    c_rows = jnp.zeros((MOD_ROWS, D_MODEL), F32).at[0].set(c[0]).at[1].set(c_ctx)
    mods = _mod_call(c_rows, w_mod[0], b_mod).reshape(MOD_ROWS, N_MOD, D_MODEL)
    mx, mc = mods[0], mods[1]

    _, hc, w1a, w1b, w1o = _ffn_call(ctx0, mc, nw[0:1], nw[1:2], w_ffn1_in[0], w_ffn1_in[0], w_ffn1_out[0],
                                     b_tile0=n_f, mod_row=0, post_mod_row=3, tm=CTX_LEN, emit_weights=True)
    x1, hx, wi = _ffn_call(x0, mx, nw[0:1], nw[1:2], w1a, w1b, w1o, b_tile0=0, mod_row=0, post_mod_row=3,
                           tm=FFN_TM, cast_srcs=(w_in[0],))

    gq, gk = q_norm_w[0][None, :], k_norm_w[0][None, :]
    axis_dim = HEAD_DIM // 2
    inv_freq = ROPE_THETA ** (-jnp.arange(0, axis_dim, 2, dtype=F32) / axis_dim)
    inv_freq_lanes = jnp.tile(jnp.repeat(inv_freq, 2), 2)[None, :]
    qk, vt, wba, wbg, wo = _qkv_call(hx, wi, gq, gk, inv_freq_lanes, first_col=0, n_q_heads=N_Q_HEADS, rope=True,
                                     tm=QKV_TM, cast_srcs=(w_branch_attn[0], w_branch_gmlp[0], w_out[0]))
    ck, vct = _qkv_call(hc, wi, gq, gk, inv_freq_lanes, first_col=K_OFF, n_q_heads=0, rope=False, tm=CTX_LEN)
    attn, w2i, w2o = _attn_call(qk, vt, ck, vct, cast_srcs=(w_ffn2_in[0], w_ffn2_out[0]))

    gm = _gmlp_call(hx, wi, gmlp_ln_w, gmlp_ln_b, ws, b_spatial[0].T, tm=GMLP_TM)
    x2 = _merge_call(x1, mx, hx, attn, gm, wi, b_gate[0], wba, wbg, wo, gate_row=5, tm=MERGE_TM)

    (out,) = _ffn_call(x2, mx, nw[2:3], final_norm_w[None, :], w2i, w2i, w2o, b_tile0=n_f, mod_row=6,
                       post_mod_row=None, tm=FFN_TM)
    return out[None]
```

```python
import functools
import math

import jax
import jax.numpy as jnp
from jax import lax
from jax.experimental import pallas as pl
from jax.experimental.pallas import tpu as pltpu

D_MODEL = 2048
SEQ = 8192
CTX_LEN = 256
GRID_W = 64
HEAD_DIM = 128
N_Q_HEADS = 16
N_KV_HEADS = 4
Q_PER_KV = N_Q_HEADS // N_KV_HEADS
ROPE_THETA = 10000.0
ATTN_SCALE = HEAD_DIM ** -0.5
GMLP_GROUPS = 16
GMLP_WIDTH = 2048
GROUP_DIM = GMLP_WIDTH // GMLP_GROUPS
CHUNK = 128
D_FF = 5632
MACARON_WEIGHT = 0.5
N_MOD = 9
EPS = 1e-6
LOG2_E = math.log2(math.e)

Q_W = N_Q_HEADS * HEAD_DIM
KV_W = N_KV_HEADS * HEAD_DIM
K_OFF = Q_W
V_OFF = K_OFF + KV_W
U_OFF = V_OFF + KV_W
GV_OFF = U_OFF + GMLP_WIDTH
GATE_OFF = GV_OFF + GMLP_WIDTH
IN_W = GATE_OFF + 2 * D_MODEL

V7X_VMEM_BYTES = 64 * 1024 * 1024
VMEM_CAP_BYTES = V7X_VMEM_BYTES - 6 * 1024 * 1024

BF16_SUBLANES = 16
MOD_ROWS = BF16_SUBLANES
CAST_BLOCK_BYTES = 1024 * 1024
MOD_TN = 1024
FFN_TM = 512
FFN_TF = 512
QKV_TM = 512
ATTN_TQ = 128
GMLP_TM = 512
GMLP_TN = 1024
MERGE_TM = 512
MERGE_TN = 512

F32 = jnp.float32
BF16 = jnp.bfloat16


def _vmem_limit(pipelined_bytes, resident_bytes):
    return int(min(2 * pipelined_bytes + resident_bytes, VMEM_CAP_BYTES))


def _dot(a, b):
    return jnp.dot(a, b, preferred_element_type=F32)


def _rms_factor(x):
    return lax.rsqrt(jnp.mean(x * x, axis=-1, keepdims=True) + EPS)


def _rmsnorm(x, w):
    return x * _rms_factor(x) * w


def _gelu_exact(x):
    return 0.5 * x * (1 + lax.erf(x * (2.0 ** -0.5)))


def _mod_kernel(c_ref, w_ref, b_ref, o_ref):
    sc = jax.nn.silu(c_ref[...])
    o_ref[...] = _dot(sc.astype(BF16), w_ref[...].astype(BF16)) + b_ref[...]


def _mod_call(c_rows, w_mod, b_mod):
    n = w_mod.shape[1]
    return pl.pallas_call(
        _mod_kernel,
        grid=(n // MOD_TN,),
        in_specs=[
            pl.BlockSpec((MOD_ROWS, D_MODEL), lambda j: (0, 0)),
            pl.BlockSpec((D_MODEL, MOD_TN), lambda j: (0, j)),
            pl.BlockSpec((1, MOD_TN), lambda j: (0, j)),
        ],
        out_specs=pl.BlockSpec((MOD_ROWS, MOD_TN), lambda j: (0, j)),
        out_shape=jax.ShapeDtypeStruct((MOD_ROWS, n), F32),
        compiler_params=pltpu.CompilerParams(
            dimension_semantics=("arbitrary",),
            vmem_limit_bytes=_vmem_limit(D_MODEL * MOD_TN * 4, D_MODEL * MOD_TN * 2 + (4 << 20))),
        name="mod",
    )(c_rows, w_mod, b_mod)


def _cast_row_block(rows, cols):
    rb = BF16_SUBLANES
    while rows % (2 * rb) == 0 and 2 * rb * cols * 4 <= CAST_BLOCK_BYTES:
        rb *= 2
    return rb


def _cast_plan(weights, step_of, n_steps):
    n_blocks, in_specs, out_specs, out_shapes, nbytes = [], [], [], [], 0
    for w in weights:
        rows, cols = w.shape
        rb = _cast_row_block(rows, cols)
        nb = rows // rb
        assert nb <= n_steps

        def index(*g, nb=nb):
            return (jnp.minimum(step_of(*g), nb - 1), 0)

        n_blocks.append(nb)
        in_specs.append(pl.BlockSpec((rb, cols), index))
        out_specs.append(pl.BlockSpec((rb, cols), index))
        out_shapes.append(jax.ShapeDtypeStruct((rows, cols), BF16))
        nbytes += rb * cols * 6
    return tuple(n_blocks), in_specs, out_specs, out_shapes, nbytes


def _cast_rows(step, srcs, dsts, n_blocks):
    for src, dst, nb in zip(srcs, dsts, n_blocks):
        @pl.when(step < nb)
        def _():
            dst[...] = src[...].astype(dst.dtype)


def _ffn_kernel(x_ref, mod_ref, nw_ref, pnw_ref, wa_ref, wb_ref, wo_ref, *refs, mod_row, post_mod_row,
                emit_weights, cast_blocks):
    refs = list(refs)
    cast_srcs = [refs.pop(0) for _ in cast_blocks]
    y_ref = refs.pop(0) if post_mod_row is not None else None
    o_ref = refs.pop(0)
    w_outs = [refs.pop(0) for _ in range(3)] if emit_weights else None
    cast_dsts = [refs.pop(0) for _ in cast_blocks]
    h_ref, acc_ref, r_ref = refs
    j = pl.program_id(1)
    _cast_rows(pl.program_id(0) * pl.num_programs(1) + j, cast_srcs, cast_dsts, cast_blocks)

    @pl.when(j == 0)
    def _():
        shift = mod_ref[mod_row:mod_row + 1, :]
        scale = mod_ref[mod_row + 1:mod_row + 2, :]
        r_ref[...] = _rms_factor(x_ref[...])
        h = x_ref[...] * r_ref[...] * nw_ref[...] * (1 + scale) + shift
        h_ref[...] = h.astype(BF16)
        acc_ref[...] = jnp.zeros_like(acc_ref)

    wa, wb, wo = wa_ref[...], wb_ref[...], wo_ref[...]
    if emit_weights:
        wa, wb, wo = wa.astype(BF16), wb.astype(BF16), wo.astype(BF16)
        for w_out, w in zip(w_outs, (wa, wb, wo)):
            w_out[...] = w
    h = h_ref[...]
    a = _dot(h, wa)
    b = _dot(h, wb)
    act = jax.nn.silu(a) * b
    acc_ref[...] += _dot(act.astype(BF16), wo)

    @pl.when(j == pl.num_programs(1) - 1)
    def _():
        gate = mod_ref[mod_row + 2:mod_row + 3, :]
        stage_ref = acc_ref if post_mod_row is None else y_ref
        y = x_ref[...] + MACARON_WEIGHT * gate * acc_ref[...]
        stage_ref[...] = y
        r_ref[...] = _rms_factor(y)
        yn = stage_ref[...] * r_ref[...] * pnw_ref[...]
        if post_mod_row is None:
            o_ref[...] = yn
        else:
            pshift = mod_ref[post_mod_row:post_mod_row + 1, :]
            pscale = mod_ref[post_mod_row + 1:post_mod_row + 2, :]
            o_ref[...] = (yn * (1 + pscale) + pshift).astype(BF16)


def _ffn_call(x, mods, nw, pnw, wa, wb, wo, *, b_tile0, mod_row, post_mod_row, tm, emit_weights=False,
              cast_srcs=()):
    m = x.shape[0]
    n_f = D_FF // FFN_TF
    n_i = m // tm
    assert not emit_weights or n_i == 1
    w_bytes = 4 if emit_weights else 2
    row_f32 = pl.BlockSpec((tm, D_MODEL), lambda i, j: (i, 0))
    vec = pl.BlockSpec((1, D_MODEL), lambda i, j: (0, 0))
    cast_blocks, cast_in, cast_out, cast_shapes, cast_bytes = _cast_plan(
        cast_srcs, lambda i, j: i * n_f + j, n_i * n_f)
    out_shape, out_specs = [], []
    if post_mod_row is not None:
        out_shape += [jax.ShapeDtypeStruct((m, D_MODEL), F32), jax.ShapeDtypeStruct((m, D_MODEL), BF16)]
        out_specs += [row_f32, pl.BlockSpec((tm, D_MODEL), lambda i, j: (i, 0))]
        out_bytes = tm * D_MODEL * 6
    else:
        out_shape += [jax.ShapeDtypeStruct((m, D_MODEL), F32)]
        out_specs += [row_f32]
        out_bytes = tm * D_MODEL * 4
    if emit_weights:
        out_shape += [jax.ShapeDtypeStruct((D_MODEL, D_FF), BF16)] * 2 + [jax.ShapeDtypeStruct((D_FF, D_MODEL), BF16)]
        out_specs += [pl.BlockSpec((D_MODEL, FFN_TF), lambda i, j: (0, j))] * 2
        out_specs += [pl.BlockSpec((FFN_TF, D_MODEL), lambda i, j: (j, 0))]
        out_bytes += 3 * D_MODEL * FFN_TF * 2
    pipelined = tm * D_MODEL * 4 + out_bytes + 3 * D_MODEL * FFN_TF * w_bytes + cast_bytes
    resident = tm * D_MODEL * 6 + 6 * tm * FFN_TF * 4 + (2 << 20)
    return pl.pallas_call(
        functools.partial(_ffn_kernel, mod_row=mod_row, post_mod_row=post_mod_row, emit_weights=emit_weights,
                          cast_blocks=cast_blocks),
        grid=(n_i, n_f),
        in_specs=[
            row_f32,
            pl.BlockSpec((N_MOD, D_MODEL), lambda i, j: (0, 0)),
            vec,
            vec,
            pl.BlockSpec((D_MODEL, FFN_TF), lambda i, j: (0, j)),
            pl.BlockSpec((D_MODEL, FFN_TF), lambda i, j: (0, j + b_tile0)),
            pl.BlockSpec((FFN_TF, D_MODEL), lambda i, j: (j, 0)),
        ] + cast_in,
        out_specs=out_specs + cast_out,
        out_shape=out_shape + cast_shapes,
        scratch_shapes=[pltpu.VMEM((tm, D_MODEL), BF16), pltpu.VMEM((tm, D_MODEL), F32), pltpu.VMEM((tm, 1), F32)],
        compiler_params=pltpu.CompilerParams(
            dimension_semantics=("arbitrary", "arbitrary"),
            vmem_limit_bytes=_vmem_limit(pipelined, resident)),
        name="ffn",
    )(x, mods, nw, pnw, wa, wb, wo, *cast_srcs)


def _rope_tables(tile, tm, inv_freq):
    t = tile * tm + lax.broadcasted_iota(jnp.int32, (tm, HEAD_DIM), 0)
    lane = lax.broadcasted_iota(jnp.int32, (tm, HEAD_DIM), 1)
    row = lax.shift_right_logical(t, jnp.int32(GRID_W.bit_length() - 1))
    col = lax.bitwise_and(t, jnp.int32(GRID_W - 1))
    ang = jnp.where(lane < HEAD_DIM // 2, row, col).astype(F32) * inv_freq
    cos, sin = jnp.cos(ang), jnp.sin(ang)
    even = lax.bitwise_and(lane, jnp.int32(1)) == 0
    return cos, jnp.where(even, -sin, 0.0), jnp.where(even, 0.0, sin)


def _qkv_kernel(h_ref, w_ref, gq_ref, gk_ref, freq_ref, *refs, n_q_heads, n_k_heads, rope, cast_blocks):
    refs = list(refs)
    cast_srcs = [refs.pop(0) for _ in cast_blocks]
    o_ref, vt_ref = refs.pop(0), refs.pop(0)
    cast_dsts = [refs.pop(0) for _ in cast_blocks]
    (z_ref,) = refs
    t = pl.program_id(0)
    _cast_rows(t, cast_srcs, cast_dsts, cast_blocks)
    tm = h_ref.shape[0]

    @pl.when(t == 0)
    def _():
        z_ref[...] = jnp.zeros(z_ref.shape, F32)

    gain_q = gq_ref[...] * (ATTN_SCALE * LOG2_E)
    gain_k = gk_ref[...]
    if rope:
        cos, sa, sb = _rope_tables(jnp.maximum(t - 1, 0), tm, freq_ref[...])
    for head in range(n_q_heads + n_k_heads):
        cols = slice(head * HEAD_DIM, (head + 1) * HEAD_DIM)
        y = _rmsnorm(z_ref[:, cols], gain_q if head < n_q_heads else gain_k)
        if rope:
            y = y * cos + pltpu.roll(y, HEAD_DIM - 1, 1) * sa + pltpu.roll(y, 1, 1) * sb
        o_ref[:, cols] = y.astype(o_ref.dtype)
    vt_ref[0] = z_ref[:, (n_q_heads + n_k_heads) * HEAD_DIM:].T.astype(vt_ref.dtype)
    z_ref[...] = _dot(h_ref[...], w_ref[...])


def _qkv_call(h, w_in, gq, gk, inv_freq, *, first_col, n_q_heads, rope, tm, cast_srcs=()):
    m = h.shape[0]
    n_i = m // tm
    qk_width = (n_q_heads + N_KV_HEADS) * HEAD_DIM
    width = qk_width + KV_W
    assert first_col + width == U_OFF and first_col % width == 0 and GRID_W & (GRID_W - 1) == 0
    vec = pl.BlockSpec((1, HEAD_DIM), lambda t: (0, 0))
    cast_blocks, cast_in, cast_out, cast_shapes, cast_bytes = _cast_plan(cast_srcs, lambda t: t, n_i + 1)
    pipelined = tm * D_MODEL * 2 + tm * width * 2 + cast_bytes
    resident = D_MODEL * width * 2 + 3 * tm * width * 4 + 4 * tm * HEAD_DIM * 4
    return pl.pallas_call(
        functools.partial(_qkv_kernel, n_q_heads=n_q_heads, n_k_heads=N_KV_HEADS, rope=rope,
                          cast_blocks=cast_blocks),
        grid=(n_i + 1,),
        in_specs=[
            pl.BlockSpec((tm, D_MODEL), lambda t: (jnp.minimum(t, n_i - 1), 0)),
            pl.BlockSpec((D_MODEL, width), lambda t: (0, first_col // width), pipeline_mode=pl.Buffered(1)),
            vec, vec, vec,
        ] + cast_in,
        out_specs=[
            pl.BlockSpec((tm, qk_width), lambda t: (jnp.maximum(t - 1, 0), 0)),
            pl.BlockSpec((1, KV_W, tm), lambda t: (jnp.maximum(t - 1, 0), 0, 0)),
        ] + cast_out,
        out_shape=[
            jax.ShapeDtypeStruct((m, qk_width), BF16),
            jax.ShapeDtypeStruct((n_i, KV_W, tm), BF16),
        ] + cast_shapes,
        scratch_shapes=[pltpu.VMEM((tm, width), F32)],
        compiler_params=pltpu.CompilerParams(
            dimension_semantics=("arbitrary",),
            vmem_limit_bytes=_vmem_limit(pipelined, resident)),
        name="qkv",
    )(h, w_in, gq, gk, inv_freq, *cast_srcs)


def _attn_kernel(q_ref, k_ref, vt_ref, kc_ref, vct_ref, *refs, cast_blocks):
    refs = list(refs)
    cast_srcs = [refs.pop(0) for _ in cast_blocks]
    o_ref = refs.pop(0)
    cast_dsts = [refs.pop(0) for _ in cast_blocks]
    s0_ref, s1_ref, m0_ref, m1_ref, acc_ref, l8_ref = refs
    i = pl.program_id(1)
    n_tiles = pl.num_programs(1) - 2
    _cast_rows(pl.program_id(0) * pl.num_programs(1) + i, cast_srcs, cast_dsts, cast_blocks)
    tq = q_ref.shape[0]
    c_len = kc_ref.shape[0]
    n_chunks = vt_ref.shape[0]
    tk = vt_ref.shape[2]
    rows = Q_PER_KV * tq

    @pl.when(jnp.logical_and(pl.program_id(0) == 0, i == 0))
    def _():
        s1_ref[...] = jnp.zeros(s1_ref.shape, F32)
        m1_ref[...] = jnp.zeros(m1_ref.shape, F32)
        acc_ref[...] = jnp.zeros(acc_ref.shape, F32)
        l8_ref[...] = jnp.ones(l8_ref.shape, F32)

    def finalize():
        o = (acc_ref[...] / jnp.sum(l8_ref[...], axis=0, keepdims=True)).T
        for g in range(Q_PER_KV):
            o_ref[:, g * HEAD_DIM:(g + 1) * HEAD_DIM] = o[g * tq:(g + 1) * tq].astype(o_ref.dtype)

    def sublane_groups(a):
        return a.reshape(a.shape[0] // 8, 8, rows)

    def step(s_cur, m_cur, s_prev, m_prev_ref):
        finalize()
        m_prev = m_prev_ref[...]
        if s_cur is not None:
            q = jnp.concatenate([q_ref[:, g * HEAD_DIM:(g + 1) * HEAD_DIM] for g in range(Q_PER_KV)], axis=0)

        def pass1_chunk(k, key_rows, m8):
            if s_cur is None:
                return None
            s = lax.dot_general(k, q, (((1,), (1,)), ((), ())), preferred_element_type=F32)
            s_cur[key_rows, :] = s
            part = jnp.max(sublane_groups(s), axis=0)
            return part if m8 is None else jnp.maximum(m8, part)

        def pass2_chunk(vt, key_rows, l8, acc):
            p = jnp.exp2(s_prev[key_rows, :] - m_prev)
            part = jnp.sum(sublane_groups(p), axis=0)
            pv = _dot(vt, p.astype(BF16))
            return (part, pv) if l8 is None else (l8 + part, acc + pv)

        ctx_rows = slice(0, c_len)
        m8 = pass1_chunk(kc_ref[...], ctx_rows, None)
        l8, acc = pass2_chunk(vct_ref[0], ctx_rows, None, None)
        for t in range(n_chunks):
            key_rows = slice(c_len + t * tk, c_len + (t + 1) * tk)
            m8 = pass1_chunk(k_ref[t * tk:(t + 1) * tk, :], key_rows, m8)
            l8, acc = pass2_chunk(vt_ref[t], key_rows, l8, acc)

        if s_cur is not None:
            m_cur[...] = jnp.max(m8, axis=0, keepdims=True)
        acc_ref[...] = acc
        l8_ref[...] = l8

    @pl.when(jnp.logical_and(i < n_tiles, i % 2 == 0))
    def _():
        step(s0_ref, m0_ref, s1_ref, m1_ref)

    @pl.when(jnp.logical_and(i < n_tiles, i % 2 == 1))
    def _():
        step(s1_ref, m1_ref, s0_ref, m0_ref)

    @pl.when(i == n_tiles)
    def _():
        step(None, None, s1_ref, m1_ref)

    @pl.when(i == n_tiles + 1)
    def _():
        finalize()


def _attn_call(qk, vt, ck, vct, cast_srcs=()):
    s_len = qk.shape[0]
    c_len = ck.shape[0]
    n_chunks, _, tk = vt.shape
    n_tiles = s_len // ATTN_TQ
    gw = Q_PER_KV * HEAD_DIM
    rows = Q_PER_KV * ATTN_TQ
    n_steps = n_tiles + 2
    assert n_tiles % 2 == 0
    cast_blocks, cast_in, cast_out, cast_shapes, cast_bytes = _cast_plan(
        cast_srcs, lambda kv, i: kv * n_steps + i, N_KV_HEADS * n_steps)
    pipelined = 2 * ATTN_TQ * gw * 2 + 2 * s_len * HEAD_DIM * 2 + 2 * c_len * HEAD_DIM * 2 + cast_bytes
    resident = 2 * (s_len + c_len) * rows * 4 + 2 * tk * rows * 4
    return pl.pallas_call(
        functools.partial(_attn_kernel, cast_blocks=cast_blocks),
        grid=(N_KV_HEADS, n_steps),
        in_specs=[
            pl.BlockSpec((ATTN_TQ, gw), lambda kv, i: (jnp.minimum(i, n_tiles - 1), kv)),
            pl.BlockSpec((s_len, HEAD_DIM), lambda kv, i: (0, K_OFF // HEAD_DIM + kv)),
            pl.BlockSpec((n_chunks, HEAD_DIM, tk), lambda kv, i: (0, kv, 0)),
            pl.BlockSpec((c_len, HEAD_DIM), lambda kv, i: (0, kv)),
            pl.BlockSpec((1, HEAD_DIM, c_len), lambda kv, i: (0, kv, 0)),
        ] + cast_in,
        out_specs=[pl.BlockSpec((ATTN_TQ, gw), lambda kv, i: (jnp.maximum(i - 2, 0), kv))] + cast_out,
        out_shape=[jax.ShapeDtypeStruct((s_len, Q_W), BF16)] + cast_shapes,
        scratch_shapes=[
            pltpu.VMEM((c_len + s_len, rows), F32),
            pltpu.VMEM((c_len + s_len, rows), F32),
            pltpu.VMEM((1, rows), F32),
            pltpu.VMEM((1, rows), F32),
            pltpu.VMEM((HEAD_DIM, rows), F32),
            pltpu.VMEM((8, rows), F32),
        ],
        compiler_params=pltpu.CompilerParams(
            dimension_semantics=("arbitrary", "arbitrary"),
            vmem_limit_bytes=_vmem_limit(pipelined, resident)),
        name="attn",
    )(qk, qk, vt, ck, vct, *cast_srcs)


def _gmlp_kernel(h_ref, *refs):
    n_tiles = GMLP_WIDTH // GMLP_TN
    wu_refs, wv_refs = refs[:n_tiles], refs[n_tiles:2 * n_tiles]
    lnw_ref, lnb_ref, ws_ref, bs_ref, o_ref, gu_ref, gv_ref = refs[2 * n_tiles:]
    tm = h_ref.shape[0]

    @pl.when(pl.program_id(0) == 0)
    def _():
        gu_ref[...] = jnp.zeros(gu_ref.shape, F32)
        gv_ref[...] = jnp.zeros(gv_ref.shape, F32)

    tiles = [slice(jj * GMLP_TN, (jj + 1) * GMLP_TN) for jj in range(n_tiles)]
    total = sum(gv_ref[:, tc].sum(axis=-1, keepdims=True) for tc in tiles)
    mu = total / GMLP_WIDTH
    sq = sum(((gv_ref[:, tc] - mu) ** 2).sum(axis=-1, keepdims=True) for tc in tiles)
    rstd = lax.rsqrt(sq / GMLP_WIDTH + EPS)
    groups_per_tile = GMLP_TN // GROUP_DIM
    for jj, tc in enumerate(tiles):
        vn = ((gv_ref[:, tc] - mu) * rstd * lnw_ref[:, tc] + lnb_ref[:, tc]).astype(BF16)
        for gg in range(groups_per_tile):
            g = jj * groups_per_tile + gg
            cols = slice(g * GROUP_DIM, (g + 1) * GROUP_DIM)
            bias = bs_ref[:, g:g + 1]
            for c in range(tm // CHUNK):
                rows = slice(c * CHUNK, (c + 1) * CHUNK)
                mixed = _dot(ws_ref[g], vn[rows, gg * GROUP_DIM:(gg + 1) * GROUP_DIM]) + bias
                o_ref[rows, cols] = (gu_ref[rows, cols] * mixed).astype(o_ref.dtype)

    h = h_ref[...]
    for tc, wu_ref in zip(tiles, wu_refs):
        gu_ref[:, tc] = _gelu_exact(_dot(h, wu_ref[...]))
    for tc, wv_ref in zip(tiles, wv_refs):
        gv_ref[:, tc] = _gelu_exact(_dot(h, wv_ref[...]))


def _gmlp_call(h, w_in, lnw, lnb, w_s, b_s_t, *, tm):
    m = h.shape[0]
    n_i = m // tm
    n_tiles = GMLP_WIDTH // GMLP_TN
    vec = pl.BlockSpec((1, GMLP_WIDTH), lambda t: (0, 0))

    def w_tile(first_col, jj):
        return pl.BlockSpec((D_MODEL, GMLP_TN), lambda t: (0, first_col // GMLP_TN + jj),
                            pipeline_mode=pl.Buffered(1))

    pipelined = (tm * D_MODEL * 2 + tm * GMLP_WIDTH * 2 + GMLP_GROUPS * CHUNK * CHUNK * 2 + CHUNK * 128 * 4)
    resident = 2 * D_MODEL * GMLP_WIDTH * 2 + 2 * tm * GMLP_WIDTH * 4 + 6 * tm * GMLP_TN * 4
    return pl.pallas_call(
        _gmlp_kernel,
        grid=(n_i + 1,),
        in_specs=[pl.BlockSpec((tm, D_MODEL), lambda t: (jnp.minimum(t, n_i - 1), 0))]
        + [w_tile(U_OFF, jj) for jj in range(n_tiles)]
        + [w_tile(GV_OFF, jj) for jj in range(n_tiles)]
        + [
            vec, vec,
            pl.BlockSpec((GMLP_GROUPS, CHUNK, CHUNK), lambda t: (0, 0, 0)),
            pl.BlockSpec((CHUNK, GMLP_GROUPS), lambda t: (0, 0)),
        ],
        out_specs=pl.BlockSpec((tm, GMLP_WIDTH), lambda t: (jnp.maximum(t - 1, 0), 0)),
        out_shape=jax.ShapeDtypeStruct((m, GMLP_WIDTH), BF16),
        scratch_shapes=[pltpu.VMEM((tm, GMLP_WIDTH), F32), pltpu.VMEM((tm, GMLP_WIDTH), F32)],
        compiler_params=pltpu.CompilerParams(
            dimension_semantics=("arbitrary",),
            vmem_limit_bytes=_vmem_limit(pipelined, resident)),
        name="gmlp",
    )(h, *([w_in] * (2 * n_tiles)), lnw, lnb, w_s, b_s_t)


def _merge_kernel(x_ref, mod_ref, h_ref, attn_ref, gm_ref, wga_ref, wgb_ref, bg_ref, wba_ref, wbg_ref, wo_ref,
                  o_ref, *, gate_row):
    j = pl.program_id(1)

    @pl.when(j == 0)
    def _():
        o_ref[...] = jnp.zeros_like(o_ref)

    h = h_ref[...]
    ga = jax.nn.sigmoid(_dot(h, wga_ref[...]) + bg_ref[0:1, :])
    gb = jax.nn.sigmoid(_dot(h, wgb_ref[...]) + bg_ref[1:2, :])
    merged = ga * _dot(attn_ref[...], wba_ref[...]) + gb * _dot(gm_ref[...], wbg_ref[...])
    o_ref[...] += _dot(merged.astype(BF16), wo_ref[...])

    @pl.when(j == pl.num_programs(1) - 1)
    def _():
        o_ref[...] = x_ref[...] + mod_ref[gate_row:gate_row + 1, :] * o_ref[...]


def _merge_call(x, mods, h, attn, gm, w_in, b_gate, w_ba, w_bg, w_o, *, gate_row, tm):
    m = x.shape[0]
    n_j = D_MODEL // MERGE_TN
    row_bf = pl.BlockSpec((tm, D_MODEL), lambda i, j: (i, 0))
    row_f32 = pl.BlockSpec((tm, D_MODEL), lambda i, j: (i, 0))
    col_w = pl.BlockSpec((D_MODEL, MERGE_TN), lambda i, j: (0, j))
    pipelined = 2 * tm * D_MODEL * 4 + 3 * tm * D_MODEL * 2 + 5 * D_MODEL * MERGE_TN * 2
    resident = 8 * tm * MERGE_TN * 4 + (2 << 20)
    return pl.pallas_call(
        functools.partial(_merge_kernel, gate_row=gate_row),
        grid=(m // tm, n_j),
        in_specs=[
            row_f32,
            pl.BlockSpec((N_MOD, D_MODEL), lambda i, j: (0, 0)),
            row_bf, row_bf, row_bf,
            pl.BlockSpec((D_MODEL, MERGE_TN), lambda i, j: (0, GATE_OFF // MERGE_TN + j)),
            pl.BlockSpec((D_MODEL, MERGE_TN), lambda i, j: (0, (GATE_OFF + D_MODEL) // MERGE_TN + j)),
            pl.BlockSpec((2, MERGE_TN), lambda i, j: (0, j)),
            col_w, col_w,
            pl.BlockSpec((MERGE_TN, D_MODEL), lambda i, j: (j, 0)),
        ],
        out_specs=row_f32,
        out_shape=jax.ShapeDtypeStruct((m, D_MODEL), F32),
        compiler_params=pltpu.CompilerParams(
            dimension_semantics=("parallel", "arbitrary"),
            vmem_limit_bytes=_vmem_limit(pipelined, resident)),
        name="merge",
    )(x, mods, h, attn, gm, w_in, w_in, b_gate, w_ba, w_bg, w_o)


def kernel(x, c, ctx, c_ctx, w_mod, b_mod, norm_w, w_ffn1_in, w_ffn1_out, w_ffn2_in, w_ffn2_out, w_in, b_gate,
           q_norm_w, k_norm_w, gmlp_ln_w, gmlp_ln_b, w_spatial, b_spatial, w_branch_attn, w_branch_gmlp, w_out,
           final_norm_w):
    batch, seq, d = x.shape
    assert batch == 1 and d == D_MODEL and seq == SEQ and seq % GRID_W == 0
    assert w_mod.shape[0] == 1 and ctx.shape == (1, CTX_LEN, D_MODEL) and w_in.shape[-1] == IN_W

    x0 = x[0]
    ctx0 = ctx[0]
    ws = w_spatial[0].astype(BF16)
    nw = norm_w[0]
    n_f = D_FF // FFN_TF

    c_rows = jnp.zeros((MOD_ROWS, D_MODEL), F32).at[0].set(c[0]).at[1].set(c_ctx)
    mods = _mod_call(c_rows, w_mod[0], b_mod).reshape(MOD_ROWS, N_MOD, D_MODEL)
    mx, mc = mods[0], mods[1]

    _, hc, w1a, w1b, w1o = _ffn_call(ctx0, mc, nw[0:1], nw[1:2], w_ffn1_in[0], w_ffn1_in[0], w_ffn1_out[0],
                                     b_tile0=n_f, mod_row=0, post_mod_row=3, tm=CTX_LEN, emit_weights=True)
    x1, hx, wi = _ffn_call(x0, mx, nw[0:1], nw[1:2], w1a, w1b, w1o, b_tile0=0, mod_row=0, post_mod_row=3,
                           tm=FFN_TM, cast_srcs=(w_in[0],))

    gq, gk = q_norm_w[0][None, :], k_norm_w[0][None, :]
    axis_dim = HEAD_DIM // 2
    inv_freq = ROPE_THETA ** (-jnp.arange(0, axis_dim, 2, dtype=F32) / axis_dim)
    inv_freq_lanes = jnp.tile(jnp.repeat(inv_freq, 2), 2)[None, :]
    qk, vt, wba, wbg, wo = _qkv_call(hx, wi, gq, gk, inv_freq_lanes, first_col=0, n_q_heads=N_Q_HEADS, rope=True,
                                     tm=QKV_TM, cast_srcs=(w_branch_attn[0], w_branch_gmlp[0], w_out[0]))
    ck, vct = _qkv_call(hc, wi, gq, gk, inv_freq_lanes, first_col=K_OFF, n_q_heads=0, rope=False, tm=CTX_LEN)
    attn, w2i, w2o = _attn_call(qk, vt, ck, vct, cast_srcs=(w_ffn2_in[0], w_ffn2_out[0]))

    gm = _gmlp_call(hx, wi, gmlp_ln_w, gmlp_ln_b, ws, b_spatial[0].T, tm=GMLP_TM)
    x2 = _merge_call(x1, mx, hx, attn, gm, wi, b_gate[0], wba, wbg, wo, gate_row=5, tm=MERGE_TM)

    (out,) = _ffn_call(x2, mx, nw[2:3], final_norm_w[None, :], w2i, w2i, w2o, b_tile0=n_f, mod_row=6,
                       post_mod_row=None, tm=FFN_TM)
    return out[None]
```

```python
import functools
import math

import jax
import jax.numpy as jnp
from jax import lax
from jax.experimental import pallas as pl
from jax.experimental.pallas import tpu as pltpu

D_MODEL = 2048
SEQ = 8192
CTX_LEN = 256
GRID_W = 64
HEAD_DIM = 128
N_Q_HEADS = 16
N_KV_HEADS = 4
Q_PER_KV = N_Q_HEADS // N_KV_HEADS
ROPE_THETA = 10000.0
ATTN_SCALE = HEAD_DIM ** -0.5
GMLP_GROUPS = 16
GMLP_WIDTH = 2048
GROUP_DIM = GMLP_WIDTH // GMLP_GROUPS
CHUNK = 128
D_FF = 5632
MACARON_WEIGHT = 0.5
N_MOD = 9
EPS = 1e-6
LOG2_E = math.log2(math.e)

Q_W = N_Q_HEADS * HEAD_DIM
KV_W = N_KV_HEADS * HEAD_DIM
K_OFF = Q_W
V_OFF = K_OFF + KV_W
U_OFF = V_OFF + KV_W
GV_OFF = U_OFF + GMLP_WIDTH
GATE_OFF = GV_OFF + GMLP_WIDTH
IN_W = GATE_OFF + 2 * D_MODEL

V7X_VMEM_BYTES = 64 * 1024 * 1024
VMEM_CAP_BYTES = V7X_VMEM_BYTES - 6 * 1024 * 1024

BF16_SUBLANES = 16
MOD_ROWS = BF16_SUBLANES
CAST_BLOCK_BYTES = 1024 * 1024
MOD_TN = 1024
FFN_TM = 512
FFN_TF = 512
QKV_TM = 512
ATTN_TQ = 128
GMLP_TM = 512
GMLP_TN = 1024
MERGE_TM = 512
MERGE_TN = 512

F32 = jnp.float32
BF16 = jnp.bfloat16


def _vmem_limit(pipelined_bytes, resident_bytes):
    return int(min(2 * pipelined_bytes + resident_bytes, VMEM_CAP_BYTES))


def _dot(a, b):
    return jnp.dot(a, b, preferred_element_type=F32)


def _rms_factor(x):
    return lax.rsqrt(jnp.mean(x * x, axis=-1, keepdims=True) + EPS)


def _rmsnorm(x, w):
    return x * _rms_factor(x) * w


def _gelu_exact(x):
    return 0.5 * x * (1 + lax.erf(x * (2.0 ** -0.5)))


def _mod_kernel(c_ref, w_ref, b_ref, o_ref):
    sc = jax.nn.silu(c_ref[...])
    o_ref[...] = _dot(sc.astype(BF16), w_ref[...].astype(BF16)) + b_ref[...]


def _mod_call(c_rows, w_mod, b_mod):
    n = w_mod.shape[1]
    return pl.pallas_call(
        _mod_kernel,
        grid=(n // MOD_TN,),
        in_specs=[
            pl.BlockSpec((MOD_ROWS, D_MODEL), lambda j: (0, 0)),
            pl.BlockSpec((D_MODEL, MOD_TN), lambda j: (0, j)),
            pl.BlockSpec((1, MOD_TN), lambda j: (0, j)),
        ],
        out_specs=pl.BlockSpec((MOD_ROWS, MOD_TN), lambda j: (0, j)),
        out_shape=jax.ShapeDtypeStruct((MOD_ROWS, n), F32),
        compiler_params=pltpu.CompilerParams(
            dimension_semantics=("arbitrary",),
            vmem_limit_bytes=_vmem_limit(D_MODEL * MOD_TN * 4, D_MODEL * MOD_TN * 2 + (4 << 20))),
        name="mod",
    )(c_rows, w_mod, b_mod)


def _cast_row_block(rows, cols):
    rb = BF16_SUBLANES
    while rows % (2 * rb) == 0 and 2 * rb * cols * 4 <= CAST_BLOCK_BYTES:
        rb *= 2
    return rb


def _cast_plan(weights, step_of, n_steps):
    n_blocks, in_specs, out_specs, out_shapes, nbytes = [], [], [], [], 0
    for w in weights:
        rows, cols = w.shape
        rb = _cast_row_block(rows, cols)
        nb = rows // rb
        assert nb <= n_steps

        def index(*g, nb=nb):
            return (jnp.minimum(step_of(*g), nb - 1), 0)

        n_blocks.append(nb)
        in_specs.append(pl.BlockSpec((rb, cols), index))
        out_specs.append(pl.BlockSpec((rb, cols), index))
        out_shapes.append(jax.ShapeDtypeStruct((rows, cols), BF16))
        nbytes += rb * cols * 6
    return tuple(n_blocks), in_specs, out_specs, out_shapes, nbytes


def _cast_rows(step, srcs, dsts, n_blocks):
    for src, dst, nb in zip(srcs, dsts, n_blocks):
        @pl.when(step < nb)
        def _():
            dst[...] = src[...].astype(dst.dtype)


def _ffn_kernel(x_ref, mod_ref, nw_ref, pnw_ref, wa_ref, wb_ref, wo_ref, *refs, mod_row, post_mod_row,
                emit_weights, cast_blocks):
    refs = list(refs)
    cast_srcs = [refs.pop(0) for _ in cast_blocks]
    y_ref = refs.pop(0) if post_mod_row is not None else None
    o_ref = refs.pop(0)
    w_outs = [refs.pop(0) for _ in range(3)] if emit_weights else None
    cast_dsts = [refs.pop(0) for _ in cast_blocks]
    h_ref, acc_ref, r_ref = refs
    j = pl.program_id(1)
    _cast_rows(pl.program_id(0) * pl.num_programs(1) + j, cast_srcs, cast_dsts, cast_blocks)

    def hidden_chunk_product():
        wa, wb, wo = wa_ref[...], wb_ref[...], wo_ref[...]
        if emit_weights:
            wa, wb, wo = wa.astype(BF16), wb.astype(BF16), wo.astype(BF16)
            for w_out, w in zip(w_outs, (wa, wb, wo)):
                w_out[...] = w
        h = h_ref[...]
        a = _dot(h, wa)
        b = _dot(h, wb)
        act = jax.nn.silu(a) * b
        return _dot(act.astype(BF16), wo)

    @pl.when(j == 0)
    def _():
        shift = mod_ref[mod_row:mod_row + 1, :]
        scale = mod_ref[mod_row + 1:mod_row + 2, :]
        r_ref[...] = _rms_factor(x_ref[...])
        h = x_ref[...] * r_ref[...] * nw_ref[...] * (1 + scale) + shift
        h_ref[...] = h.astype(BF16)
        acc_ref[...] = hidden_chunk_product()

    last = pl.num_programs(1) - 1

    @pl.when(jnp.logical_and(j > 0, j < last))
    def _():
        acc_ref[...] += hidden_chunk_product()

    @pl.when(j == last)
    def _():
        acc_ref[...] += hidden_chunk_product()
        gate = mod_ref[mod_row + 2:mod_row + 3, :]
        stage_ref = acc_ref if post_mod_row is None else y_ref
        y = x_ref[...] + MACARON_WEIGHT * gate * acc_ref[...]
        stage_ref[...] = y
        r_ref[...] = _rms_factor(y)
        yn = stage_ref[...] * r_ref[...] * pnw_ref[...]
        if post_mod_row is None:
            o_ref[...] = yn
        else:
            pshift = mod_ref[post_mod_row:post_mod_row + 1, :]
            pscale = mod_ref[post_mod_row + 1:post_mod_row + 2, :]
            o_ref[...] = (yn * (1 + pscale) + pshift).astype(BF16)


def _ffn_call(x, mods, nw, pnw, wa, wb, wo, *, b_tile0, mod_row, post_mod_row, tm, emit_weights=False,
              cast_srcs=()):
    m = x.shape[0]
    n_f = D_FF // FFN_TF
    n_i = m // tm
    assert not emit_weights or n_i == 1
    w_bytes = 4 if emit_weights else 2
    row_f32 = pl.BlockSpec((tm, D_MODEL), lambda i, j: (i, 0))
    vec = pl.BlockSpec((1, D_MODEL), lambda i, j: (0, 0))
    cast_blocks, cast_in, cast_out, cast_shapes, cast_bytes = _cast_plan(
        cast_srcs, lambda i, j: i * n_f + j, n_i * n_f)
    out_shape, out_specs = [], []
    if post_mod_row is not None:
        out_shape += [jax.ShapeDtypeStruct((m, D_MODEL), F32), jax.ShapeDtypeStruct((m, D_MODEL), BF16)]
        out_specs += [row_f32, pl.BlockSpec((tm, D_MODEL), lambda i, j: (i, 0))]
        out_bytes = tm * D_MODEL * 6
    else:
        out_shape += [jax.ShapeDtypeStruct((m, D_MODEL), F32)]
        out_specs += [row_f32]
        out_bytes = tm * D_MODEL * 4
    if emit_weights:
        out_shape += [jax.ShapeDtypeStruct((D_MODEL, D_FF), BF16)] * 2 + [jax.ShapeDtypeStruct((D_FF, D_MODEL), BF16)]
        out_specs += [pl.BlockSpec((D_MODEL, FFN_TF), lambda i, j: (0, j))] * 2
        out_specs += [pl.BlockSpec((FFN_TF, D_MODEL), lambda i, j: (j, 0))]
        out_bytes += 3 * D_MODEL * FFN_TF * 2
    pipelined = tm * D_MODEL * 4 + out_bytes + 3 * D_MODEL * FFN_TF * w_bytes + cast_bytes
    resident = tm * D_MODEL * 6 + 6 * tm * FFN_TF * 4 + (2 << 20)
    return pl.pallas_call(
        functools.partial(_ffn_kernel, mod_row=mod_row, post_mod_row=post_mod_row, emit_weights=emit_weights,
                          cast_blocks=cast_blocks),
        grid=(n_i, n_f),
        in_specs=[
            row_f32,
            pl.BlockSpec((N_MOD, D_MODEL), lambda i, j: (0, 0)),
            vec,
            vec,
            pl.BlockSpec((D_MODEL, FFN_TF), lambda i, j: (0, j)),
            pl.BlockSpec((D_MODEL, FFN_TF), lambda i, j: (0, j + b_tile0)),
            pl.BlockSpec((FFN_TF, D_MODEL), lambda i, j: (j, 0)),
        ] + cast_in,
        out_specs=out_specs + cast_out,
        out_shape=out_shape + cast_shapes,
        scratch_shapes=[pltpu.VMEM((tm, D_MODEL), BF16), pltpu.VMEM((tm, D_MODEL), F32), pltpu.VMEM((tm, 1), F32)],
        compiler_params=pltpu.CompilerParams(
            dimension_semantics=("arbitrary", "arbitrary"),
            vmem_limit_bytes=_vmem_limit(pipelined, resident)),
        name="ffn",
    )(x, mods, nw, pnw, wa, wb, wo, *cast_srcs)


def _rope_tables(tile, tm, inv_freq):
    t = tile * tm + lax.broadcasted_iota(jnp.int32, (tm, HEAD_DIM), 0)
    lane = lax.broadcasted_iota(jnp.int32, (tm, HEAD_DIM), 1)
    row = lax.shift_right_logical(t, jnp.int32(GRID_W.bit_length() - 1))
    col = lax.bitwise_and(t, jnp.int32(GRID_W - 1))
    ang = jnp.where(lane < HEAD_DIM // 2, row, col).astype(F32) * inv_freq
    cos, sin = jnp.cos(ang), jnp.sin(ang)
    even = lax.bitwise_and(lane, jnp.int32(1)) == 0
    return cos, jnp.where(even, -sin, 0.0), jnp.where(even, 0.0, sin)


def _qkv_kernel(h_ref, w_ref, gq_ref, gk_ref, freq_ref, *refs, n_q_heads, n_k_heads, rope, cast_blocks):
    refs = list(refs)
    cast_srcs = [refs.pop(0) for _ in cast_blocks]
    o_ref, vt_ref = refs.pop(0), refs.pop(0)
    cast_dsts = [refs.pop(0) for _ in cast_blocks]
    (z_ref,) = refs
    t = pl.program_id(0)
    _cast_rows(t, cast_srcs, cast_dsts, cast_blocks)
    tm = h_ref.shape[0]

    @pl.when(t == 0)
    def _():
        z_ref[...] = jnp.zeros(z_ref.shape, F32)

    def finish_previous_tile():
        gain_q = gq_ref[...] * (ATTN_SCALE * LOG2_E)
        gain_k = gk_ref[...]
        if rope:
            cos, sa, sb = _rope_tables(jnp.maximum(t - 1, 0), tm, freq_ref[...])
        for head in range(n_q_heads + n_k_heads):
            cols = slice(head * HEAD_DIM, (head + 1) * HEAD_DIM)
            y = _rmsnorm(z_ref[:, cols], gain_q if head < n_q_heads else gain_k)
            if rope:
                y = y * cos + pltpu.roll(y, HEAD_DIM - 1, 1) * sa + pltpu.roll(y, 1, 1) * sb
            o_ref[:, cols] = y.astype(o_ref.dtype)
        vt_ref[0] = z_ref[:, (n_q_heads + n_k_heads) * HEAD_DIM:].T.astype(vt_ref.dtype)

    last = pl.num_programs(0) - 1

    @pl.when(t < last)
    def _():
        finish_previous_tile()
        z_ref[...] = _dot(h_ref[...], w_ref[...])

    @pl.when(t == last)
    def _():
        finish_previous_tile()


def _qkv_call(h, w_in, gq, gk, inv_freq, *, first_col, n_q_heads, rope, tm, cast_srcs=()):
    m = h.shape[0]
    n_i = m // tm
    qk_width = (n_q_heads + N_KV_HEADS) * HEAD_DIM
    width = qk_width + KV_W
    assert first_col + width == U_OFF and first_col % width == 0 and GRID_W & (GRID_W - 1) == 0
    vec = pl.BlockSpec((1, HEAD_DIM), lambda t: (0, 0))
    cast_blocks, cast_in, cast_out, cast_shapes, cast_bytes = _cast_plan(cast_srcs, lambda t: t, n_i + 1)
    pipelined = tm * D_MODEL * 2 + tm * width * 2 + cast_bytes
    resident = D_MODEL * width * 2 + 3 * tm * width * 4 + 4 * tm * HEAD_DIM * 4
    return pl.pallas_call(
        functools.partial(_qkv_kernel, n_q_heads=n_q_heads, n_k_heads=N_KV_HEADS, rope=rope,
                          cast_blocks=cast_blocks),
        grid=(n_i + 1,),
        in_specs=[
            pl.BlockSpec((tm, D_MODEL), lambda t: (jnp.minimum(t, n_i - 1), 0)),
            pl.BlockSpec((D_MODEL, width), lambda t: (0, first_col // width), pipeline_mode=pl.Buffered(1)),
            vec, vec, vec,
        ] + cast_in,
        out_specs=[
            pl.BlockSpec((tm, qk_width), lambda t: (jnp.maximum(t - 1, 0), 0)),
            pl.BlockSpec((1, KV_W, tm), lambda t: (jnp.maximum(t - 1, 0), 0, 0)),
        ] + cast_out,
        out_shape=[
            jax.ShapeDtypeStruct((m, qk_width), BF16),
            jax.ShapeDtypeStruct((n_i, KV_W, tm), BF16),
        ] + cast_shapes,
        scratch_shapes=[pltpu.VMEM((tm, width), F32)],
        compiler_params=pltpu.CompilerParams(
            dimension_semantics=("arbitrary",),
            vmem_limit_bytes=_vmem_limit(pipelined, resident)),
        name="qkv",
    )(h, w_in, gq, gk, inv_freq, *cast_srcs)


def _attn_kernel(q_ref, k_ref, vt_ref, kc_ref, vct_ref, *refs, cast_blocks):
    refs = list(refs)
    cast_srcs = [refs.pop(0) for _ in cast_blocks]
    o_ref = refs.pop(0)
    cast_dsts = [refs.pop(0) for _ in cast_blocks]
    s0_ref, s1_ref, m0_ref, m1_ref = refs
    i = pl.program_id(1)
    _cast_rows(pl.program_id(0) * pl.num_programs(1) + i, cast_srcs, cast_dsts, cast_blocks)
    tq = q_ref.shape[0]
    c_len = kc_ref.shape[0]
    n_chunks = vt_ref.shape[0]
    tk = vt_ref.shape[2]

    @pl.when(jnp.logical_and(pl.program_id(0) == 0, i == 0))
    def _():
        s1_ref[...] = jnp.zeros(s1_ref.shape, F32)
        m1_ref[...] = jnp.zeros(m1_ref.shape, F32)

    def step(s_cur, m_cur, s_prev, m_prev_ref):
        q = jnp.concatenate([q_ref[:, g * HEAD_DIM:(g + 1) * HEAD_DIM] for g in range(Q_PER_KV)], axis=0)
        rows = q.shape[0]
        m_prev = m_prev_ref[...]

        def sublane_groups(a):
            return a.reshape(a.shape[0] // 8, 8, rows)

        def pass1_chunk(k, key_rows, m8):
            s = lax.dot_general(k, q, (((1,), (1,)), ((), ())), preferred_element_type=F32)
            s_cur[key_rows, :] = s
            part = jnp.max(sublane_groups(s), axis=0)
            return part if m8 is None else jnp.maximum(m8, part)

        def pass2_chunk(vt, key_rows, l8, acc):
            p = jnp.exp2(s_prev[key_rows, :] - m_prev)
            part = jnp.sum(sublane_groups(p), axis=0)
            pv = _dot(vt, p.astype(BF16))
            return (part, pv) if l8 is None else (l8 + part, acc + pv)

        ctx_rows = slice(0, c_len)
        m8 = pass1_chunk(kc_ref[...], ctx_rows, None)
        l8, acc = pass2_chunk(vct_ref[0], ctx_rows, None, None)
        for t in range(n_chunks):
            key_rows = slice(c_len + t * tk, c_len + (t + 1) * tk)
            m8 = pass1_chunk(k_ref[t * tk:(t + 1) * tk, :], key_rows, m8)
            l8, acc = pass2_chunk(vt_ref[t], key_rows, l8, acc)

        m_cur[...] = jnp.max(m8, axis=0, keepdims=True)
        o = (acc / jnp.sum(l8, axis=0, keepdims=True)).T
        for g in range(Q_PER_KV):
            o_ref[:, g * HEAD_DIM:(g + 1) * HEAD_DIM] = o[g * tq:(g + 1) * tq].astype(o_ref.dtype)

    @pl.when(i % 2 == 0)
    def _():
        step(s0_ref, m0_ref, s1_ref, m1_ref)

    @pl.when(i % 2 == 1)
    def _():
        step(s1_ref, m1_ref, s0_ref, m0_ref)


def _attn_call(qk, vt, ck, vct, cast_srcs=()):
    s_len = qk.shape[0]
    c_len = ck.shape[0]
    n_chunks, _, tk = vt.shape
    n_tiles = s_len // ATTN_TQ
    gw = Q_PER_KV * HEAD_DIM
    rows = Q_PER_KV * ATTN_TQ
    cast_blocks, cast_in, cast_out, cast_shapes, cast_bytes = _cast_plan(
        cast_srcs, lambda kv, i: kv * (n_tiles + 1) + i, N_KV_HEADS * (n_tiles + 1))
    pipelined = 2 * ATTN_TQ * gw * 2 + 2 * s_len * HEAD_DIM * 2 + 2 * c_len * HEAD_DIM * 2 + cast_bytes
    resident = 2 * (s_len + c_len) * rows * 4 + 2 * tk * rows * 4
    return pl.pallas_call(
        functools.partial(_attn_kernel, cast_blocks=cast_blocks),
        grid=(N_KV_HEADS, n_tiles + 1),
        in_specs=[
            pl.BlockSpec((ATTN_TQ, gw), lambda kv, i: (jnp.minimum(i, n_tiles - 1), kv)),
            pl.BlockSpec((s_len, HEAD_DIM), lambda kv, i: (0, K_OFF // HEAD_DIM + kv)),
            pl.BlockSpec((n_chunks, HEAD_DIM, tk), lambda kv, i: (0, kv, 0)),
            pl.BlockSpec((c_len, HEAD_DIM), lambda kv, i: (0, kv)),
            pl.BlockSpec((1, HEAD_DIM, c_len), lambda kv, i: (0, kv, 0)),
        ] + cast_in,
        out_specs=[pl.BlockSpec((ATTN_TQ, gw), lambda kv, i: (jnp.maximum(i - 1, 0), kv))] + cast_out,
        out_shape=[jax.ShapeDtypeStruct((s_len, Q_W), BF16)] + cast_shapes,
        scratch_shapes=[
            pltpu.VMEM((c_len + s_len, rows), F32),
            pltpu.VMEM((c_len + s_len, rows), F32),
            pltpu.VMEM((1, rows), F32),
            pltpu.VMEM((1, rows), F32),
        ],
        compiler_params=pltpu.CompilerParams(
            dimension_semantics=("arbitrary", "arbitrary"),
            vmem_limit_bytes=_vmem_limit(pipelined, resident)),
        name="attn",
    )(qk, qk, vt, ck, vct, *cast_srcs)


def _gmlp_kernel(h_ref, *refs):
    n_tiles = GMLP_WIDTH // GMLP_TN
    wu_refs, wv_refs = refs[:n_tiles], refs[n_tiles:2 * n_tiles]
    lnw_ref, lnb_ref, ws_ref, bs_ref, o_ref, gu_ref, gv_ref = refs[2 * n_tiles:]
    tm = h_ref.shape[0]

    @pl.when(pl.program_id(0) == 0)
    def _():
        gu_ref[...] = jnp.zeros(gu_ref.shape, F32)
        gv_ref[...] = jnp.zeros(gv_ref.shape, F32)

    tiles = [slice(jj * GMLP_TN, (jj + 1) * GMLP_TN) for jj in range(n_tiles)]

    def finish_previous_tile():
        total = sum(gv_ref[:, tc].sum(axis=-1, keepdims=True) for tc in tiles)
        mu = total / GMLP_WIDTH
        sq = sum(((gv_ref[:, tc] - mu) ** 2).sum(axis=-1, keepdims=True) for tc in tiles)
        rstd = lax.rsqrt(sq / GMLP_WIDTH + EPS)
        groups_per_tile = GMLP_TN // GROUP_DIM
        for jj, tc in enumerate(tiles):
            vn = ((gv_ref[:, tc] - mu) * rstd * lnw_ref[:, tc] + lnb_ref[:, tc]).astype(BF16)
            for gg in range(groups_per_tile):
                g = jj * groups_per_tile + gg
                cols = slice(g * GROUP_DIM, (g + 1) * GROUP_DIM)
                bias = bs_ref[:, g:g + 1]
                for c in range(tm // CHUNK):
                    rows = slice(c * CHUNK, (c + 1) * CHUNK)
                    mixed = _dot(ws_ref[g], vn[rows, gg * GROUP_DIM:(gg + 1) * GROUP_DIM]) + bias
                    o_ref[rows, cols] = (gu_ref[rows, cols] * mixed).astype(o_ref.dtype)

    def project_this_tile():
        h = h_ref[...]
        for tc, wu_ref in zip(tiles, wu_refs):
            gu_ref[:, tc] = _gelu_exact(_dot(h, wu_ref[...]))
        for tc, wv_ref in zip(tiles, wv_refs):
            gv_ref[:, tc] = _gelu_exact(_dot(h, wv_ref[...]))

    last = pl.num_programs(0) - 1

    @pl.when(pl.program_id(0) < last)
    def _():
        finish_previous_tile()
        project_this_tile()

    @pl.when(pl.program_id(0) == last)
    def _():
        finish_previous_tile()


def _gmlp_call(h, w_in, lnw, lnb, w_s, b_s_t, *, tm):
    m = h.shape[0]
    n_i = m // tm
    n_tiles = GMLP_WIDTH // GMLP_TN
    vec = pl.BlockSpec((1, GMLP_WIDTH), lambda t: (0, 0))

    def w_tile(first_col, jj):
        return pl.BlockSpec((D_MODEL, GMLP_TN), lambda t: (0, first_col // GMLP_TN + jj),
                            pipeline_mode=pl.Buffered(1))

    pipelined = (tm * D_MODEL * 2 + tm * GMLP_WIDTH * 2 + GMLP_GROUPS * CHUNK * CHUNK * 2 + CHUNK * 128 * 4)
    resident = 2 * D_MODEL * GMLP_WIDTH * 2 + 2 * tm * GMLP_WIDTH * 4 + 6 * tm * GMLP_TN * 4
    return pl.pallas_call(
        _gmlp_kernel,
        grid=(n_i + 1,),
        in_specs=[pl.BlockSpec((tm, D_MODEL), lambda t: (jnp.minimum(t, n_i - 1), 0))]
        + [w_tile(U_OFF, jj) for jj in range(n_tiles)]
        + [w_tile(GV_OFF, jj) for jj in range(n_tiles)]
        + [
            vec, vec,
            pl.BlockSpec((GMLP_GROUPS, CHUNK, CHUNK), lambda t: (0, 0, 0)),
            pl.BlockSpec((CHUNK, GMLP_GROUPS), lambda t: (0, 0)),
        ],
        out_specs=pl.BlockSpec((tm, GMLP_WIDTH), lambda t: (jnp.maximum(t - 1, 0), 0)),
        out_shape=jax.ShapeDtypeStruct((m, GMLP_WIDTH), BF16),
        scratch_shapes=[pltpu.VMEM((tm, GMLP_WIDTH), F32), pltpu.VMEM((tm, GMLP_WIDTH), F32)],
        compiler_params=pltpu.CompilerParams(
            dimension_semantics=("arbitrary",),
            vmem_limit_bytes=_vmem_limit(pipelined, resident)),
        name="gmlp",
    )(h, *([w_in] * (2 * n_tiles)), lnw, lnb, w_s, b_s_t)


def _merge_kernel(x_ref, mod_ref, h_ref, attn_ref, gm_ref, wga_ref, wgb_ref, bg_ref, wba_ref, wbg_ref, wo_ref,
                  o_ref, *, gate_row):
    j = pl.program_id(1)

    def column_tile_product():
        h = h_ref[...]
        ga = jax.nn.sigmoid(_dot(h, wga_ref[...]) + bg_ref[0:1, :])
        gb = jax.nn.sigmoid(_dot(h, wgb_ref[...]) + bg_ref[1:2, :])
        merged = ga * _dot(attn_ref[...], wba_ref[...]) + gb * _dot(gm_ref[...], wbg_ref[...])
        return _dot(merged.astype(BF16), wo_ref[...])

    @pl.when(j == 0)
    def _():
        o_ref[...] = column_tile_product()

    last = pl.num_programs(1) - 1

    @pl.when(jnp.logical_and(j > 0, j < last))
    def _():
        o_ref[...] += column_tile_product()

    @pl.when(j == last)
    def _():
        o_ref[...] += column_tile_product()
        o_ref[...] = x_ref[...] + mod_ref[gate_row:gate_row + 1, :] * o_ref[...]


def _merge_call(x, mods, h, attn, gm, w_in, b_gate, w_ba, w_bg, w_o, *, gate_row, tm):
    m = x.shape[0]
    n_j = D_MODEL // MERGE_TN
    row_bf = pl.BlockSpec((tm, D_MODEL), lambda i, j: (i, 0))
    row_f32 = pl.BlockSpec((tm, D_MODEL), lambda i, j: (i, 0))
    col_w = pl.BlockSpec((D_MODEL, MERGE_TN), lambda i, j: (0, j))
    pipelined = 2 * tm * D_MODEL * 4 + 3 * tm * D_MODEL * 2 + 5 * D_MODEL * MERGE_TN * 2
    resident = 8 * tm * MERGE_TN * 4 + (2 << 20)
    return pl.pallas_call(
        functools.partial(_merge_kernel, gate_row=gate_row),
        grid=(m // tm, n_j),
        in_specs=[
            row_f32,
            pl.BlockSpec((N_MOD, D_MODEL), lambda i, j: (0, 0)),
            row_bf, row_bf, row_bf,
            pl.BlockSpec((D_MODEL, MERGE_TN), lambda i, j: (0, GATE_OFF // MERGE_TN + j)),
            pl.BlockSpec((D_MODEL, MERGE_TN), lambda i, j: (0, (GATE_OFF + D_MODEL) // MERGE_TN + j)),
            pl.BlockSpec((2, MERGE_TN), lambda i, j: (0, j)),
            col_w, col_w,
            pl.BlockSpec((MERGE_TN, D_MODEL), lambda i, j: (j, 0)),
        ],
        out_specs=row_f32,
        out_shape=jax.ShapeDtypeStruct((m, D_MODEL), F32),
        compiler_params=pltpu.CompilerParams(
            dimension_semantics=("parallel", "arbitrary"),
            vmem_limit_bytes=_vmem_limit(pipelined, resident)),
        name="merge",
    )(x, mods, h, attn, gm, w_in, w_in, b_gate, w_ba, w_bg, w_o)


def kernel(x, c, ctx, c_ctx, w_mod, b_mod, norm_w, w_ffn1_in, w_ffn1_out, w_ffn2_in, w_ffn2_out, w_in, b_gate,
           q_norm_w, k_norm_w, gmlp_ln_w, gmlp_ln_b, w_spatial, b_spatial, w_branch_attn, w_branch_gmlp, w_out,
           final_norm_w):
    batch, seq, d = x.shape
    assert batch == 1 and d == D_MODEL and seq == SEQ and seq % GRID_W == 0
    assert w_mod.shape[0] == 1 and ctx.shape == (1, CTX_LEN, D_MODEL) and w_in.shape[-1] == IN_W

    x0 = x[0]
    ctx0 = ctx[0]
    ws = w_spatial[0].astype(BF16)
    nw = norm_w[0]
    n_f = D_FF // FFN_TF

    c_rows = jnp.zeros((MOD_ROWS, D_MODEL), F32).at[0].set(c[0]).at[1].set(c_ctx)
    mods = _mod_call(c_rows, w_mod[0], b_mod).reshape(MOD_ROWS, N_MOD, D_MODEL)
    mx, mc = mods[0], mods[1]

    _, hc, w1a, w1b, w1o = _ffn_call(ctx0, mc, nw[0:1], nw[1:2], w_ffn1_in[0], w_ffn1_in[0], w_ffn1_out[0],
                                     b_tile0=n_f, mod_row=0, post_mod_row=3, tm=CTX_LEN, emit_weights=True)
    x1, hx, wi = _ffn_call(x0, mx, nw[0:1], nw[1:2], w1a, w1b, w1o, b_tile0=0, mod_row=0, post_mod_row=3,
                           tm=FFN_TM, cast_srcs=(w_in[0],))

    gq, gk = q_norm_w[0][None, :], k_norm_w[0][None, :]
    axis_dim = HEAD_DIM // 2
    inv_freq = ROPE_THETA ** (-jnp.arange(0, axis_dim, 2, dtype=F32) / axis_dim)
    inv_freq_lanes = jnp.tile(jnp.repeat(inv_freq, 2), 2)[None, :]
    qk, vt, wba, wbg, wo = _qkv_call(hx, wi, gq, gk, inv_freq_lanes, first_col=0, n_q_heads=N_Q_HEADS, rope=True,
                                     tm=QKV_TM, cast_srcs=(w_branch_attn[0], w_branch_gmlp[0], w_out[0]))
    ck, vct = _qkv_call(hc, wi, gq, gk, inv_freq_lanes, first_col=K_OFF, n_q_heads=0, rope=False, tm=CTX_LEN)
    attn, w2i, w2o = _attn_call(qk, vt, ck, vct, cast_srcs=(w_ffn2_in[0], w_ffn2_out[0]))

    gm = _gmlp_call(hx, wi, gmlp_ln_w, gmlp_ln_b, ws, b_spatial[0].T, tm=GMLP_TM)
    x2 = _merge_call(x1, mx, hx, attn, gm, wi, b_gate[0], wba, wbg, wo, gate_row=5, tm=MERGE_TM)

    (out,) = _ffn_call(x2, mx, nw[2:3], final_norm_w[None, :], w2i, w2i, w2o, b_tile0=n_f, mod_row=6,
                       post_mod_row=None, tm=FFN_TM)
    return out[None]
```

```python
import functools
import math

import jax
import jax.numpy as jnp
from jax import lax
from jax.experimental import pallas as pl
from jax.experimental.pallas import tpu as pltpu

D_MODEL = 2048
SEQ = 8192
CTX_LEN = 256
GRID_W = 64
HEAD_DIM = 128
N_Q_HEADS = 16
N_KV_HEADS = 4
Q_PER_KV = N_Q_HEADS // N_KV_HEADS
ROPE_THETA = 10000.0
ATTN_SCALE = HEAD_DIM ** -0.5
GMLP_GROUPS = 16
GMLP_WIDTH = 2048
GROUP_DIM = GMLP_WIDTH // GMLP_GROUPS
CHUNK = 128
D_FF = 5632
MACARON_WEIGHT = 0.5
N_MOD = 9
EPS = 1e-6
LOG2_E = math.log2(math.e)

Q_W = N_Q_HEADS * HEAD_DIM
KV_W = N_KV_HEADS * HEAD_DIM
K_OFF = Q_W
V_OFF = K_OFF + KV_W
U_OFF = V_OFF + KV_W
GV_OFF = U_OFF + GMLP_WIDTH
GATE_OFF = GV_OFF + GMLP_WIDTH
IN_W = GATE_OFF + 2 * D_MODEL

V7X_VMEM_BYTES = 64 * 1024 * 1024
VMEM_CAP_BYTES = V7X_VMEM_BYTES - 6 * 1024 * 1024

BF16_SUBLANES = 16
MOD_ROWS = BF16_SUBLANES
CAST_BLOCK_BYTES = 1024 * 1024
MOD_TN = 1024
FFN_TM = 1024
FFN1_TF = 256
FFN2_TF = 512
FFN_CTX_TF = 512
QKV_TM = 512
ATTN_TQ = 128
GMLP_TM = 512
GMLP_TN = 1024
MERGE_TM = 512
MERGE_TN = 512

F32 = jnp.float32
BF16 = jnp.bfloat16


def _vmem_limit(pipelined_bytes, resident_bytes):
    return int(min(2 * pipelined_bytes + resident_bytes, VMEM_CAP_BYTES))


def _dot(a, b):
    return jnp.dot(a, b, preferred_element_type=F32)


def _rms_factor(x):
    return lax.rsqrt(jnp.mean(x * x, axis=-1, keepdims=True) + EPS)


def _rmsnorm(x, w):
    return x * _rms_factor(x) * w


def _gelu_exact(x):
    return 0.5 * x * (1 + lax.erf(x * (2.0 ** -0.5)))


def _mod_kernel(c_ref, w_ref, b_ref, o_ref):
    sc = jax.nn.silu(c_ref[...])
    o_ref[...] = _dot(sc.astype(BF16), w_ref[...].astype(BF16)) + b_ref[...]


def _mod_call(c_rows, w_mod, b_mod):
    n = w_mod.shape[1]
    return pl.pallas_call(
        _mod_kernel,
        grid=(n // MOD_TN,),
        in_specs=[
            pl.BlockSpec((MOD_ROWS, D_MODEL), lambda j: (0, 0)),
            pl.BlockSpec((D_MODEL, MOD_TN), lambda j: (0, j)),
            pl.BlockSpec((1, MOD_TN), lambda j: (0, j)),
        ],
        out_specs=pl.BlockSpec((MOD_ROWS, MOD_TN), lambda j: (0, j)),
        out_shape=jax.ShapeDtypeStruct((MOD_ROWS, n), F32),
        compiler_params=pltpu.CompilerParams(
            dimension_semantics=("arbitrary",),
            vmem_limit_bytes=_vmem_limit(D_MODEL * MOD_TN * 4, D_MODEL * MOD_TN * 2 + (4 << 20))),
        name="mod",
    )(c_rows, w_mod, b_mod)


def _cast_row_block(rows, cols):
    rb = BF16_SUBLANES
    while rows % (2 * rb) == 0 and 2 * rb * cols * 4 <= CAST_BLOCK_BYTES:
        rb *= 2
    return rb


def _cast_plan(weights, step_of, n_steps):
    n_blocks, in_specs, out_specs, out_shapes, nbytes = [], [], [], [], 0
    for w in weights:
        rows, cols = w.shape
        rb = _cast_row_block(rows, cols)
        nb = rows // rb
        assert nb <= n_steps

        def index(*g, nb=nb):
            return (jnp.minimum(step_of(*g), nb - 1), 0)

        n_blocks.append(nb)
        in_specs.append(pl.BlockSpec((rb, cols), index))
        out_specs.append(pl.BlockSpec((rb, cols), index))
        out_shapes.append(jax.ShapeDtypeStruct((rows, cols), BF16))
        nbytes += rb * cols * 6
    return tuple(n_blocks), in_specs, out_specs, out_shapes, nbytes


def _cast_rows(step, srcs, dsts, n_blocks):
    for src, dst, nb in zip(srcs, dsts, n_blocks):
        @pl.when(step < nb)
        def _():
            dst[...] = src[...].astype(dst.dtype)


def _ffn_kernel(x_ref, mod_ref, nw_ref, pnw_ref, wa_ref, wb_ref, wo_ref, *refs, mod_row, post_mod_row,
                emit_weights, cast_blocks):
    refs = list(refs)
    cast_srcs = [refs.pop(0) for _ in cast_blocks]
    y_ref = refs.pop(0) if post_mod_row is not None else None
    o_ref = refs.pop(0)
    w_outs = [refs.pop(0) for _ in range(3)] if emit_weights else None
    cast_dsts = [refs.pop(0) for _ in cast_blocks]
    h_ref, r_ref = refs
    acc_ref = o_ref if post_mod_row is None else y_ref
    j = pl.program_id(1)
    _cast_rows(pl.program_id(0) * pl.num_programs(1) + j, cast_srcs, cast_dsts, cast_blocks)

    def hidden_chunk_product():
        wa, wb, wo = wa_ref[...], wb_ref[...], wo_ref[...]
        if emit_weights:
            wa, wb, wo = wa.astype(BF16), wb.astype(BF16), wo.astype(BF16)
            for w_out, w in zip(w_outs, (wa, wb, wo)):
                w_out[...] = w
        h = h_ref[...]
        a = _dot(h, wa)
        b = _dot(h, wb)
        act = jax.nn.silu(a) * b
        return _dot(act.astype(BF16), wo)

    @pl.when(j == 0)
    def _():
        shift = mod_ref[mod_row:mod_row + 1, :]
        scale = mod_ref[mod_row + 1:mod_row + 2, :]
        r_ref[...] = _rms_factor(x_ref[...])
        h = x_ref[...] * r_ref[...] * nw_ref[...] * (1 + scale) + shift
        h_ref[...] = h.astype(BF16)
        acc_ref[...] = hidden_chunk_product()

    last = pl.num_programs(1) - 1

    @pl.when(jnp.logical_and(j > 0, j < last))
    def _():
        acc_ref[...] += hidden_chunk_product()

    @pl.when(j == last)
    def _():
        acc_ref[...] += hidden_chunk_product()
        gate = mod_ref[mod_row + 2:mod_row + 3, :]
        y = x_ref[...] + MACARON_WEIGHT * gate * acc_ref[...]
        acc_ref[...] = y
        r_ref[...] = _rms_factor(y)
        yn = acc_ref[...] * r_ref[...] * pnw_ref[...]
        if post_mod_row is None:
            o_ref[...] = yn
        else:
            pshift = mod_ref[post_mod_row:post_mod_row + 1, :]
            pscale = mod_ref[post_mod_row + 1:post_mod_row + 2, :]
            o_ref[...] = (yn * (1 + pscale) + pshift).astype(BF16)


def _ffn_call(x, mods, nw, pnw, wa, wb, wo, *, b_col0, mod_row, post_mod_row, tm, tf, emit_weights=False,
              cast_srcs=()):
    m = x.shape[0]
    n_f = D_FF // tf
    b_tile0 = b_col0 // tf
    n_i = m // tm
    assert not emit_weights or n_i == 1
    w_bytes = 4 if emit_weights else 2
    row_f32 = pl.BlockSpec((tm, D_MODEL), lambda i, j: (i, 0))
    vec = pl.BlockSpec((1, D_MODEL), lambda i, j: (0, 0))
    cast_blocks, cast_in, cast_out, cast_shapes, cast_bytes = _cast_plan(
        cast_srcs, lambda i, j: i * n_f + j, n_i * n_f)
    out_shape, out_specs = [], []
    if post_mod_row is not None:
        out_shape += [jax.ShapeDtypeStruct((m, D_MODEL), F32), jax.ShapeDtypeStruct((m, D_MODEL), BF16)]
        out_specs += [row_f32, pl.BlockSpec((tm, D_MODEL), lambda i, j: (i, 0))]
        out_bytes = tm * D_MODEL * 6
    else:
        out_shape += [jax.ShapeDtypeStruct((m, D_MODEL), F32)]
        out_specs += [row_f32]
        out_bytes = tm * D_MODEL * 4
    if emit_weights:
        out_shape += [jax.ShapeDtypeStruct((D_MODEL, D_FF), BF16)] * 2 + [jax.ShapeDtypeStruct((D_FF, D_MODEL), BF16)]
        out_specs += [pl.BlockSpec((D_MODEL, tf), lambda i, j: (0, j))] * 2
        out_specs += [pl.BlockSpec((tf, D_MODEL), lambda i, j: (j, 0))]
        out_bytes += 3 * D_MODEL * tf * 2
    pipelined = tm * D_MODEL * 4 + out_bytes + 3 * D_MODEL * tf * w_bytes + cast_bytes
    resident = tm * D_MODEL * 2 + 6 * tm * tf * 4 + (2 << 20)
    return pl.pallas_call(
        functools.partial(_ffn_kernel, mod_row=mod_row, post_mod_row=post_mod_row, emit_weights=emit_weights,
                          cast_blocks=cast_blocks),
        grid=(n_i, n_f),
        in_specs=[
            row_f32,
            pl.BlockSpec((N_MOD, D_MODEL), lambda i, j: (0, 0)),
            vec,
            vec,
            pl.BlockSpec((D_MODEL, tf), lambda i, j: (0, j)),
            pl.BlockSpec((D_MODEL, tf), lambda i, j: (0, j + b_tile0)),
            pl.BlockSpec((tf, D_MODEL), lambda i, j: (j, 0)),
        ] + cast_in,
        out_specs=out_specs + cast_out,
        out_shape=out_shape + cast_shapes,
        scratch_shapes=[pltpu.VMEM((tm, D_MODEL), BF16), pltpu.VMEM((tm, 1), F32)],
        compiler_params=pltpu.CompilerParams(
            dimension_semantics=("arbitrary", "arbitrary"),
            vmem_limit_bytes=_vmem_limit(pipelined, resident)),
        name="ffn",
    )(x, mods, nw, pnw, wa, wb, wo, *cast_srcs)


def _rope_tables(tile, tm, inv_freq):
    t = tile * tm + lax.broadcasted_iota(jnp.int32, (tm, HEAD_DIM), 0)
    lane = lax.broadcasted_iota(jnp.int32, (tm, HEAD_DIM), 1)
    row = lax.shift_right_logical(t, jnp.int32(GRID_W.bit_length() - 1))
    col = lax.bitwise_and(t, jnp.int32(GRID_W - 1))
    ang = jnp.where(lane < HEAD_DIM // 2, row, col).astype(F32) * inv_freq
    cos, sin = jnp.cos(ang), jnp.sin(ang)
    even = lax.bitwise_and(lane, jnp.int32(1)) == 0
    return cos, jnp.where(even, -sin, 0.0), jnp.where(even, 0.0, sin)


def _qkv_kernel(h_ref, w_ref, gq_ref, gk_ref, freq_ref, *refs, n_q_heads, n_k_heads, rope, cast_blocks):
    refs = list(refs)
    cast_srcs = [refs.pop(0) for _ in cast_blocks]
    o_ref, vt_ref = refs.pop(0), refs.pop(0)
    cast_dsts = [refs.pop(0) for _ in cast_blocks]
    (z_ref,) = refs
    t = pl.program_id(0)
    _cast_rows(t, cast_srcs, cast_dsts, cast_blocks)
    tm = h_ref.shape[0]

    @pl.when(t == 0)
    def _():
        z_ref[...] = jnp.zeros(z_ref.shape, F32)

    def finish_previous_tile():
        gain_q = gq_ref[...] * (ATTN_SCALE * LOG2_E)
        gain_k = gk_ref[...]
        if rope:
            cos, sa, sb = _rope_tables(jnp.maximum(t - 1, 0), tm, freq_ref[...])
        for head in range(n_q_heads + n_k_heads):
            cols = slice(head * HEAD_DIM, (head + 1) * HEAD_DIM)
            y = _rmsnorm(z_ref[:, cols], gain_q if head < n_q_heads else gain_k)
            if rope:
                y = y * cos + pltpu.roll(y, HEAD_DIM - 1, 1) * sa + pltpu.roll(y, 1, 1) * sb
            o_ref[:, cols] = y.astype(o_ref.dtype)
        vt_ref[0] = z_ref[:, (n_q_heads + n_k_heads) * HEAD_DIM:].T.astype(vt_ref.dtype)

    last = pl.num_programs(0) - 1

    @pl.when(t < last)
    def _():
        finish_previous_tile()
        z_ref[...] = _dot(h_ref[...], w_ref[...])

    @pl.when(t == last)
    def _():
        finish_previous_tile()


def _qkv_call(h, w_in, gq, gk, inv_freq, *, first_col, n_q_heads, rope, tm, cast_srcs=()):
    m = h.shape[0]
    n_i = m // tm
    qk_width = (n_q_heads + N_KV_HEADS) * HEAD_DIM
    width = qk_width + KV_W
    assert first_col + width == U_OFF and first_col % width == 0 and GRID_W & (GRID_W - 1) == 0
    vec = pl.BlockSpec((1, HEAD_DIM), lambda t: (0, 0))
    cast_blocks, cast_in, cast_out, cast_shapes, cast_bytes = _cast_plan(cast_srcs, lambda t: t, n_i + 1)
    pipelined = tm * D_MODEL * 2 + tm * width * 2 + cast_bytes
    resident = D_MODEL * width * 2 + 3 * tm * width * 4 + 4 * tm * HEAD_DIM * 4
    return pl.pallas_call(
        functools.partial(_qkv_kernel, n_q_heads=n_q_heads, n_k_heads=N_KV_HEADS, rope=rope,
                          cast_blocks=cast_blocks),
        grid=(n_i + 1,),
        in_specs=[
            pl.BlockSpec((tm, D_MODEL), lambda t: (jnp.minimum(t, n_i - 1), 0)),
            pl.BlockSpec((D_MODEL, width), lambda t: (0, first_col // width), pipeline_mode=pl.Buffered(1)),
            vec, vec, vec,
        ] + cast_in,
        out_specs=[
            pl.BlockSpec((tm, qk_width), lambda t: (jnp.maximum(t - 1, 0), 0)),
            pl.BlockSpec((1, KV_W, tm), lambda t: (jnp.maximum(t - 1, 0), 0, 0)),
        ] + cast_out,
        out_shape=[
            jax.ShapeDtypeStruct((m, qk_width), BF16),
            jax.ShapeDtypeStruct((n_i, KV_W, tm), BF16),
        ] + cast_shapes,
        scratch_shapes=[pltpu.VMEM((tm, width), F32)],
        compiler_params=pltpu.CompilerParams(
            dimension_semantics=("arbitrary",),
            vmem_limit_bytes=_vmem_limit(pipelined, resident)),
        name="qkv",
    )(h, w_in, gq, gk, inv_freq, *cast_srcs)


def _attn_kernel(q_ref, k_ref, vt_ref, kc_ref, vct_ref, *refs, cast_blocks):
    refs = list(refs)
    cast_srcs = [refs.pop(0) for _ in cast_blocks]
    o_ref = refs.pop(0)
    cast_dsts = [refs.pop(0) for _ in cast_blocks]
    s0_ref, s1_ref, m0_ref, m1_ref = refs
    i = pl.program_id(1)
    _cast_rows(pl.program_id(0) * pl.num_programs(1) + i, cast_srcs, cast_dsts, cast_blocks)
    tq = q_ref.shape[0]
    c_len = kc_ref.shape[0]
    n_chunks = vt_ref.shape[0]
    tk = vt_ref.shape[2]

    @pl.when(jnp.logical_and(pl.program_id(0) == 0, i == 0))
    def _():
        s1_ref[...] = jnp.zeros(s1_ref.shape, F32)
        m1_ref[...] = jnp.zeros(m1_ref.shape, F32)

    def step(s_cur, m_cur, s_prev, m_prev_ref):
        q = jnp.concatenate([q_ref[:, g * HEAD_DIM:(g + 1) * HEAD_DIM] for g in range(Q_PER_KV)], axis=0)
        rows = q.shape[0]
        m_prev = m_prev_ref[...]

        def sublane_groups(a):
            return a.reshape(a.shape[0] // 8, 8, rows)

        def pass1_chunk(k, key_rows, m8):
            s = lax.dot_general(k, q, (((1,), (1,)), ((), ())), preferred_element_type=F32)
            s_cur[key_rows, :] = s
            part = jnp.max(sublane_groups(s), axis=0)
            return part if m8 is None else jnp.maximum(m8, part)

        def pass2_chunk(vt, key_rows, l8, acc):
            p = jnp.exp2(s_prev[key_rows, :] - m_prev)
            part = jnp.sum(sublane_groups(p), axis=0)
            pv = _dot(vt, p.astype(BF16))
            return (part, pv) if l8 is None else (l8 + part, acc + pv)

        ctx_rows = slice(0, c_len)
        m8 = pass1_chunk(kc_ref[...], ctx_rows, None)
        l8, acc = pass2_chunk(vct_ref[0], ctx_rows, None, None)
        for t in range(n_chunks):
            key_rows = slice(c_len + t * tk, c_len + (t + 1) * tk)
            m8 = pass1_chunk(k_ref[t * tk:(t + 1) * tk, :], key_rows, m8)
            l8, acc = pass2_chunk(vt_ref[t], key_rows, l8, acc)

        m_cur[...] = jnp.max(m8, axis=0, keepdims=True)
        o = (acc / jnp.sum(l8, axis=0, keepdims=True)).T
        for g in range(Q_PER_KV):
            o_ref[:, g * HEAD_DIM:(g + 1) * HEAD_DIM] = o[g * tq:(g + 1) * tq].astype(o_ref.dtype)

    @pl.when(i % 2 == 0)
    def _():
        step(s0_ref, m0_ref, s1_ref, m1_ref)

    @pl.when(i % 2 == 1)
    def _():
        step(s1_ref, m1_ref, s0_ref, m0_ref)


def _attn_call(qk, vt, ck, vct, cast_srcs=()):
    s_len = qk.shape[0]
    c_len = ck.shape[0]
    n_chunks, _, tk = vt.shape
    n_tiles = s_len // ATTN_TQ
    gw = Q_PER_KV * HEAD_DIM
    rows = Q_PER_KV * ATTN_TQ
    cast_blocks, cast_in, cast_out, cast_shapes, cast_bytes = _cast_plan(
        cast_srcs, lambda kv, i: kv * (n_tiles + 1) + i, N_KV_HEADS * (n_tiles + 1))
    pipelined = 2 * ATTN_TQ * gw * 2 + 2 * s_len * HEAD_DIM * 2 + 2 * c_len * HEAD_DIM * 2 + cast_bytes
    resident = 2 * (s_len + c_len) * rows * 4 + 2 * tk * rows * 4
    return pl.pallas_call(
        functools.partial(_attn_kernel, cast_blocks=cast_blocks),
        grid=(N_KV_HEADS, n_tiles + 1),
        in_specs=[
            pl.BlockSpec((ATTN_TQ, gw), lambda kv, i: (jnp.minimum(i, n_tiles - 1), kv)),
            pl.BlockSpec((s_len, HEAD_DIM), lambda kv, i: (0, K_OFF // HEAD_DIM + kv)),
            pl.BlockSpec((n_chunks, HEAD_DIM, tk), lambda kv, i: (0, kv, 0)),
            pl.BlockSpec((c_len, HEAD_DIM), lambda kv, i: (0, kv)),
            pl.BlockSpec((1, HEAD_DIM, c_len), lambda kv, i: (0, kv, 0)),
        ] + cast_in,
        out_specs=[pl.BlockSpec((ATTN_TQ, gw), lambda kv, i: (jnp.maximum(i - 1, 0), kv))] + cast_out,
        out_shape=[jax.ShapeDtypeStruct((s_len, Q_W), BF16)] + cast_shapes,
        scratch_shapes=[
            pltpu.VMEM((c_len + s_len, rows), F32),
            pltpu.VMEM((c_len + s_len, rows), F32),
            pltpu.VMEM((1, rows), F32),
            pltpu.VMEM((1, rows), F32),
        ],
        compiler_params=pltpu.CompilerParams(
            dimension_semantics=("arbitrary", "arbitrary"),
            vmem_limit_bytes=_vmem_limit(pipelined, resident)),
        name="attn",
    )(qk, qk, vt, ck, vct, *cast_srcs)


def _gmlp_kernel(h_ref, *refs):
    n_tiles = GMLP_WIDTH // GMLP_TN
    wu_refs, wv_refs = refs[:n_tiles], refs[n_tiles:2 * n_tiles]
    lnw_ref, lnb_ref, ws_ref, bs_ref, o_ref, gu_ref, gv_ref = refs[2 * n_tiles:]
    tm = h_ref.shape[0]

    @pl.when(pl.program_id(0) == 0)
    def _():
        gu_ref[...] = jnp.zeros(gu_ref.shape, F32)
        gv_ref[...] = jnp.zeros(gv_ref.shape, F32)

    tiles = [slice(jj * GMLP_TN, (jj + 1) * GMLP_TN) for jj in range(n_tiles)]

    def finish_previous_tile():
        total = sum(gv_ref[:, tc].sum(axis=-1, keepdims=True) for tc in tiles)
        mu = total / GMLP_WIDTH
        sq = sum(((gv_ref[:, tc] - mu) ** 2).sum(axis=-1, keepdims=True) for tc in tiles)
        rstd = lax.rsqrt(sq / GMLP_WIDTH + EPS)
        groups_per_tile = GMLP_TN // GROUP_DIM
        for jj, tc in enumerate(tiles):
            vn = ((gv_ref[:, tc] - mu) * rstd * lnw_ref[:, tc] + lnb_ref[:, tc]).astype(BF16)
            for gg in range(groups_per_tile):
                g = jj * groups_per_tile + gg
                cols = slice(g * GROUP_DIM, (g + 1) * GROUP_DIM)
                bias = bs_ref[:, g:g + 1]
                for c in range(tm // CHUNK):
                    rows = slice(c * CHUNK, (c + 1) * CHUNK)
                    mixed = _dot(ws_ref[g], vn[rows, gg * GROUP_DIM:(gg + 1) * GROUP_DIM]) + bias
                    o_ref[rows, cols] = (gu_ref[rows, cols] * mixed).astype(o_ref.dtype)

    def project_this_tile():
        h = h_ref[...]
        for tc, wu_ref in zip(tiles, wu_refs):
            gu_ref[:, tc] = _gelu_exact(_dot(h, wu_ref[...]))
        for tc, wv_ref in zip(tiles, wv_refs):
            gv_ref[:, tc] = _gelu_exact(_dot(h, wv_ref[...]))

    last = pl.num_programs(0) - 1

    @pl.when(pl.program_id(0) < last)
    def _():
        finish_previous_tile()
        project_this_tile()

    @pl.when(pl.program_id(0) == last)
    def _():
        finish_previous_tile()


def _gmlp_call(h, w_in, lnw, lnb, w_s, b_s_t, *, tm):
    m = h.shape[0]
    n_i = m // tm
    n_tiles = GMLP_WIDTH // GMLP_TN
    vec = pl.BlockSpec((1, GMLP_WIDTH), lambda t: (0, 0))

    def w_tile(first_col, jj):
        return pl.BlockSpec((D_MODEL, GMLP_TN), lambda t: (0, first_col // GMLP_TN + jj),
                            pipeline_mode=pl.Buffered(1))

    pipelined = (tm * D_MODEL * 2 + tm * GMLP_WIDTH * 2 + GMLP_GROUPS * CHUNK * CHUNK * 2 + CHUNK * 128 * 4)
    resident = 2 * D_MODEL * GMLP_WIDTH * 2 + 2 * tm * GMLP_WIDTH * 4 + 6 * tm * GMLP_TN * 4
    return pl.pallas_call(
        _gmlp_kernel,
        grid=(n_i + 1,),
        in_specs=[pl.BlockSpec((tm, D_MODEL), lambda t: (jnp.minimum(t, n_i - 1), 0))]
        + [w_tile(U_OFF, jj) for jj in range(n_tiles)]
        + [w_tile(GV_OFF, jj) for jj in range(n_tiles)]
        + [
            vec, vec,
            pl.BlockSpec((GMLP_GROUPS, CHUNK, CHUNK), lambda t: (0, 0, 0)),
            pl.BlockSpec((CHUNK, GMLP_GROUPS), lambda t: (0, 0)),
        ],
        out_specs=pl.BlockSpec((tm, GMLP_WIDTH), lambda t: (jnp.maximum(t - 1, 0), 0)),
        out_shape=jax.ShapeDtypeStruct((m, GMLP_WIDTH), BF16),
        scratch_shapes=[pltpu.VMEM((tm, GMLP_WIDTH), F32), pltpu.VMEM((tm, GMLP_WIDTH), F32)],
        compiler_params=pltpu.CompilerParams(
            dimension_semantics=("arbitrary",),
            vmem_limit_bytes=_vmem_limit(pipelined, resident)),
        name="gmlp",
    )(h, *([w_in] * (2 * n_tiles)), lnw, lnb, w_s, b_s_t)


def _merge_kernel(x_ref, mod_ref, h_ref, attn_ref, gm_ref, wga_ref, wgb_ref, bg_ref, wba_ref, wbg_ref, wo_ref,
                  o_ref, *, gate_row):
    j = pl.program_id(1)

    def column_tile_product():
        h = h_ref[...]
        ga = jax.nn.sigmoid(_dot(h, wga_ref[...]) + bg_ref[0:1, :])
        gb = jax.nn.sigmoid(_dot(h, wgb_ref[...]) + bg_ref[1:2, :])
        merged = ga * _dot(attn_ref[...], wba_ref[...]) + gb * _dot(gm_ref[...], wbg_ref[...])
        return _dot(merged.astype(BF16), wo_ref[...])

    @pl.when(j == 0)
    def _():
        o_ref[...] = column_tile_product()

    last = pl.num_programs(1) - 1

    @pl.when(jnp.logical_and(j > 0, j < last))
    def _():
        o_ref[...] += column_tile_product()

    @pl.when(j == last)
    def _():
        o_ref[...] += column_tile_product()
        o_ref[...] = x_ref[...] + mod_ref[gate_row:gate_row + 1, :] * o_ref[...]


def _merge_call(x, mods, h, attn, gm, w_in, b_gate, w_ba, w_bg, w_o, *, gate_row, tm):
    m = x.shape[0]
    n_j = D_MODEL // MERGE_TN
    row_bf = pl.BlockSpec((tm, D_MODEL), lambda i, j: (i, 0))
    row_f32 = pl.BlockSpec((tm, D_MODEL), lambda i, j: (i, 0))
    col_w = pl.BlockSpec((D_MODEL, MERGE_TN), lambda i, j: (0, j))
    pipelined = 2 * tm * D_MODEL * 4 + 3 * tm * D_MODEL * 2 + 5 * D_MODEL * MERGE_TN * 2
    resident = 8 * tm * MERGE_TN * 4 + (2 << 20)
    return pl.pallas_call(
        functools.partial(_merge_kernel, gate_row=gate_row),
        grid=(m // tm, n_j),
        in_specs=[
            row_f32,
            pl.BlockSpec((N_MOD, D_MODEL), lambda i, j: (0, 0)),
            row_bf, row_bf, row_bf,
            pl.BlockSpec((D_MODEL, MERGE_TN), lambda i, j: (0, GATE_OFF // MERGE_TN + j)),
            pl.BlockSpec((D_MODEL, MERGE_TN), lambda i, j: (0, (GATE_OFF + D_MODEL) // MERGE_TN + j)),
            pl.BlockSpec((2, MERGE_TN), lambda i, j: (0, j)),
            col_w, col_w,
            pl.BlockSpec((MERGE_TN, D_MODEL), lambda i, j: (j, 0)),
        ],
        out_specs=row_f32,
        out_shape=jax.ShapeDtypeStruct((m, D_MODEL), F32),
        compiler_params=pltpu.CompilerParams(
            dimension_semantics=("parallel", "arbitrary"),
            vmem_limit_bytes=_vmem_limit(pipelined, resident)),
        name="merge",
    )(x, mods, h, attn, gm, w_in, w_in, b_gate, w_ba, w_bg, w_o)


def kernel(x, c, ctx, c_ctx, w_mod, b_mod, norm_w, w_ffn1_in, w_ffn1_out, w_ffn2_in, w_ffn2_out, w_in, b_gate,
           q_norm_w, k_norm_w, gmlp_ln_w, gmlp_ln_b, w_spatial, b_spatial, w_branch_attn, w_branch_gmlp, w_out,
           final_norm_w):
    batch, seq, d = x.shape
    assert batch == 1 and d == D_MODEL and seq == SEQ and seq % GRID_W == 0
    assert w_mod.shape[0] == 1 and ctx.shape == (1, CTX_LEN, D_MODEL) and w_in.shape[-1] == IN_W

    x0 = x[0]
    ctx0 = ctx[0]
    ws = w_spatial[0].astype(BF16)
    nw = norm_w[0]

    c_rows = jnp.zeros((MOD_ROWS, D_MODEL), F32).at[0].set(c[0]).at[1].set(c_ctx)
    mods = _mod_call(c_rows, w_mod[0], b_mod).reshape(MOD_ROWS, N_MOD, D_MODEL)
    mx, mc = mods[0], mods[1]

    _, hc, w1a, w1b, w1o = _ffn_call(ctx0, mc, nw[0:1], nw[1:2], w_ffn1_in[0], w_ffn1_in[0], w_ffn1_out[0],
                                     b_col0=D_FF, mod_row=0, post_mod_row=3, tm=CTX_LEN, tf=FFN_CTX_TF,
                                     emit_weights=True)
    x1, hx, wi = _ffn_call(x0, mx, nw[0:1], nw[1:2], w1a, w1b, w1o, b_col0=0, mod_row=0, post_mod_row=3,
                           tm=FFN_TM, tf=FFN1_TF, cast_srcs=(w_in[0],))

    gq, gk = q_norm_w[0][None, :], k_norm_w[0][None, :]
    axis_dim = HEAD_DIM // 2
    inv_freq = ROPE_THETA ** (-jnp.arange(0, axis_dim, 2, dtype=F32) / axis_dim)
    inv_freq_lanes = jnp.tile(jnp.repeat(inv_freq, 2), 2)[None, :]
    qk, vt, wba, wbg, wo = _qkv_call(hx, wi, gq, gk, inv_freq_lanes, first_col=0, n_q_heads=N_Q_HEADS, rope=True,
                                     tm=QKV_TM, cast_srcs=(w_branch_attn[0], w_branch_gmlp[0], w_out[0]))
    ck, vct = _qkv_call(hc, wi, gq, gk, inv_freq_lanes, first_col=K_OFF, n_q_heads=0, rope=False, tm=CTX_LEN)
    attn, w2i, w2o = _attn_call(qk, vt, ck, vct, cast_srcs=(w_ffn2_in[0], w_ffn2_out[0]))

    gm = _gmlp_call(hx, wi, gmlp_ln_w, gmlp_ln_b, ws, b_spatial[0].T, tm=GMLP_TM)
    x2 = _merge_call(x1, mx, hx, attn, gm, wi, b_gate[0], wba, wbg, wo, gate_row=5, tm=MERGE_TM)

    (out,) = _ffn_call(x2, mx, nw[2:3], final_norm_w[None, :], w2i, w2i, w2o, b_col0=D_FF, mod_row=6,
                       post_mod_row=None, tm=FFN_TM, tf=FFN2_TF)
    return out[None]
```

```python
import functools
import math

import jax
import jax.numpy as jnp
from jax import lax
from jax.experimental import pallas as pl
from jax.experimental.pallas import tpu as pltpu

D_MODEL = 2048
SEQ = 8192
CTX_LEN = 256
GRID_W = 64
HEAD_DIM = 128
N_Q_HEADS = 16
N_KV_HEADS = 4
Q_PER_KV = N_Q_HEADS // N_KV_HEADS
ROPE_THETA = 10000.0
ATTN_SCALE = HEAD_DIM ** -0.5
GMLP_GROUPS = 16
GMLP_WIDTH = 2048
GROUP_DIM = GMLP_WIDTH // GMLP_GROUPS
CHUNK = 128
D_FF = 5632
MACARON_WEIGHT = 0.5
N_MOD = 9
EPS = 1e-6
LOG2_E = math.log2(math.e)

Q_W = N_Q_HEADS * HEAD_DIM
KV_W = N_KV_HEADS * HEAD_DIM
K_OFF = Q_W
V_OFF = K_OFF + KV_W
U_OFF = V_OFF + KV_W
GV_OFF = U_OFF + GMLP_WIDTH
GATE_OFF = GV_OFF + GMLP_WIDTH
IN_W = GATE_OFF + 2 * D_MODEL

V7X_VMEM_BYTES = 64 * 1024 * 1024
VMEM_CAP_BYTES = V7X_VMEM_BYTES - 6 * 1024 * 1024

BF16_SUBLANES = 16
MOD_ROWS = BF16_SUBLANES
CAST_BLOCK_BYTES = 1024 * 1024
MOD_TN = 1024
FFN_TM = 1024
FFN1_TF = 256
FFN2_TF = 512
FFN_CTX_TF = 512
QKV_TM = 512
ATTN_TQ = 128
GMLP_TM = 512
GMLP_TN = 1024
MERGE_TM = 512
MERGE_TN = 512

F32 = jnp.float32
BF16 = jnp.bfloat16


def _vmem_limit(pipelined_bytes, resident_bytes):
    return int(min(2 * pipelined_bytes + resident_bytes, VMEM_CAP_BYTES))


def _dot(a, b):
    return jnp.dot(a, b, preferred_element_type=F32)


def _rms_factor(x):
    return lax.rsqrt(jnp.mean(x * x, axis=-1, keepdims=True) + EPS)


def _rmsnorm(x, w):
    return x * _rms_factor(x) * w


def _gelu_exact(x):
    return 0.5 * x * (1 + lax.erf(x * (2.0 ** -0.5)))


def _mod_kernel(c_ref, w_ref, b_ref, o_ref):
    sc = jax.nn.silu(c_ref[...])
    o_ref[...] = _dot(sc.astype(BF16), w_ref[...].astype(BF16)) + b_ref[...]


def _mod_call(c_rows, w_mod, b_mod):
    n = w_mod.shape[1]
    return pl.pallas_call(
        _mod_kernel,
        grid=(n // MOD_TN,),
        in_specs=[
            pl.BlockSpec((MOD_ROWS, D_MODEL), lambda j: (0, 0)),
            pl.BlockSpec((D_MODEL, MOD_TN), lambda j: (0, j)),
            pl.BlockSpec((1, MOD_TN), lambda j: (0, j)),
        ],
        out_specs=pl.BlockSpec((MOD_ROWS, MOD_TN), lambda j: (0, j)),
        out_shape=jax.ShapeDtypeStruct((MOD_ROWS, n), F32),
        compiler_params=pltpu.CompilerParams(
            dimension_semantics=("arbitrary",),
            vmem_limit_bytes=_vmem_limit(D_MODEL * MOD_TN * 4, D_MODEL * MOD_TN * 2 + (4 << 20))),
        name="mod",
    )(c_rows, w_mod, b_mod)


def _cast_row_block(rows, cols):
    rb = BF16_SUBLANES
    while rows % (2 * rb) == 0 and 2 * rb * cols * 4 <= CAST_BLOCK_BYTES:
        rb *= 2
    return rb


def _cast_plan(weights, step_of, n_steps):
    n_blocks, in_specs, out_specs, out_shapes, nbytes = [], [], [], [], 0
    for w in weights:
        rows, cols = w.shape
        rb = _cast_row_block(rows, cols)
        nb = rows // rb
        assert nb <= n_steps

        def index(*g, nb=nb):
            return (jnp.minimum(step_of(*g), nb - 1), 0)

        n_blocks.append(nb)
        in_specs.append(pl.BlockSpec((rb, cols), index))
        out_specs.append(pl.BlockSpec((rb, cols), index))
        out_shapes.append(jax.ShapeDtypeStruct((rows, cols), BF16))
        nbytes += rb * cols * 6
    return tuple(n_blocks), in_specs, out_specs, out_shapes, nbytes


def _cast_rows(step, srcs, dsts, n_blocks):
    for src, dst, nb in zip(srcs, dsts, n_blocks):
        @pl.when(step < nb)
        def _():
            dst[...] = src[...].astype(dst.dtype)


def _ffn_kernel(x_ref, mod_ref, nw_ref, pnw_ref, wa_ref, wb_ref, wo_ref, *refs, mod_row, post_mod_row,
                emit_weights, cast_blocks):
    refs = list(refs)
    cast_srcs = [refs.pop(0) for _ in cast_blocks]
    y_ref = refs.pop(0) if post_mod_row is not None else None
    o_ref = refs.pop(0)
    w_outs = [refs.pop(0) for _ in range(3)] if emit_weights else None
    cast_dsts = [refs.pop(0) for _ in cast_blocks]
    h_ref, r_ref = refs
    acc_ref = o_ref if post_mod_row is None else y_ref
    j = pl.program_id(1)
    _cast_rows(pl.program_id(0) * pl.num_programs(1) + j, cast_srcs, cast_dsts, cast_blocks)

    def hidden_chunk_product():
        wa, wb, wo = wa_ref[...], wb_ref[...], wo_ref[...]
        if emit_weights:
            wa, wb, wo = wa.astype(BF16), wb.astype(BF16), wo.astype(BF16)
            for w_out, w in zip(w_outs, (wa, wb, wo)):
                w_out[...] = w
        h = h_ref[...]
        a = _dot(h, wa)
        b = _dot(h, wb)
        act = jax.nn.silu(a) * b
        return _dot(act.astype(BF16), wo)

    @pl.when(j == 0)
    def _():
        shift = mod_ref[mod_row:mod_row + 1, :]
        scale = mod_ref[mod_row + 1:mod_row + 2, :]
        r_ref[...] = _rms_factor(x_ref[...])
        h = x_ref[...] * r_ref[...] * nw_ref[...] * (1 + scale) + shift
        h_ref[...] = h.astype(BF16)
        acc_ref[...] = hidden_chunk_product()

    last = pl.num_programs(1) - 1

    @pl.when(jnp.logical_and(j > 0, j < last))
    def _():
        acc_ref[...] += hidden_chunk_product()

    @pl.when(j == last)
    def _():
        acc_ref[...] += hidden_chunk_product()
        gate = mod_ref[mod_row + 2:mod_row + 3, :]
        y = x_ref[...] + MACARON_WEIGHT * gate * acc_ref[...]
        acc_ref[...] = y
        r_ref[...] = _rms_factor(y)
        yn = acc_ref[...] * r_ref[...] * pnw_ref[...]
        if post_mod_row is None:
            o_ref[...] = yn
        else:
            pshift = mod_ref[post_mod_row:post_mod_row + 1, :]
            pscale = mod_ref[post_mod_row + 1:post_mod_row + 2, :]
            o_ref[...] = (yn * (1 + pscale) + pshift).astype(BF16)


def _ffn_call(x, mods, nw, pnw, wa, wb, wo, *, b_col0, mod_row, post_mod_row, tm, tf, emit_weights=False,
              cast_srcs=()):
    m = x.shape[0]
    n_f = D_FF // tf
    b_tile0 = b_col0 // tf
    n_i = m // tm
    assert not emit_weights or n_i == 1
    w_bytes = 4 if emit_weights else 2
    row_f32 = pl.BlockSpec((tm, D_MODEL), lambda i, j: (i, 0))
    vec = pl.BlockSpec((1, D_MODEL), lambda i, j: (0, 0))
    cast_blocks, cast_in, cast_out, cast_shapes, cast_bytes = _cast_plan(
        cast_srcs, lambda i, j: i * n_f + j, n_i * n_f)
    out_shape, out_specs = [], []
    if post_mod_row is not None:
        out_shape += [jax.ShapeDtypeStruct((m, D_MODEL), F32), jax.ShapeDtypeStruct((m, D_MODEL), BF16)]
        out_specs += [row_f32, pl.BlockSpec((tm, D_MODEL), lambda i, j: (i, 0))]
        out_bytes = tm * D_MODEL * 6
    else:
        out_shape += [jax.ShapeDtypeStruct((m, D_MODEL), F32)]
        out_specs += [row_f32]
        out_bytes = tm * D_MODEL * 4
    if emit_weights:
        out_shape += [jax.ShapeDtypeStruct((D_MODEL, D_FF), BF16)] * 2 + [jax.ShapeDtypeStruct((D_FF, D_MODEL), BF16)]
        out_specs += [pl.BlockSpec((D_MODEL, tf), lambda i, j: (0, j))] * 2
        out_specs += [pl.BlockSpec((tf, D_MODEL), lambda i, j: (j, 0))]
        out_bytes += 3 * D_MODEL * tf * 2
    pipelined = tm * D_MODEL * 4 + out_bytes + 3 * D_MODEL * tf * w_bytes + cast_bytes
    resident = tm * D_MODEL * 2 + 6 * tm * tf * 4 + (2 << 20)
    return pl.pallas_call(
        functools.partial(_ffn_kernel, mod_row=mod_row, post_mod_row=post_mod_row, emit_weights=emit_weights,
                          cast_blocks=cast_blocks),
        grid=(n_i, n_f),
        in_specs=[
            row_f32,
            pl.BlockSpec((N_MOD, D_MODEL), lambda i, j: (0, 0)),
            vec,
            vec,
            pl.BlockSpec((D_MODEL, tf), lambda i, j: (0, j)),
            pl.BlockSpec((D_MODEL, tf), lambda i, j: (0, j + b_tile0)),
            pl.BlockSpec((tf, D_MODEL), lambda i, j: (j, 0)),
        ] + cast_in,
        out_specs=out_specs + cast_out,
        out_shape=out_shape + cast_shapes,
        scratch_shapes=[pltpu.VMEM((tm, D_MODEL), BF16), pltpu.VMEM((tm, 1), F32)],
        compiler_params=pltpu.CompilerParams(
            dimension_semantics=("arbitrary", "arbitrary"),
            vmem_limit_bytes=_vmem_limit(pipelined, resident)),
        name="ffn",
    )(x, mods, nw, pnw, wa, wb, wo, *cast_srcs)


def _rope_tables(tile, tm, inv_freq):
    t = tile * tm + lax.broadcasted_iota(jnp.int32, (tm, HEAD_DIM), 0)
    lane = lax.broadcasted_iota(jnp.int32, (tm, HEAD_DIM), 1)
    row = lax.shift_right_logical(t, jnp.int32(GRID_W.bit_length() - 1))
    col = lax.bitwise_and(t, jnp.int32(GRID_W - 1))
    ang = jnp.where(lane < HEAD_DIM // 2, row, col).astype(F32) * inv_freq
    cos, sin = jnp.cos(ang), jnp.sin(ang)
    even = lax.bitwise_and(lane, jnp.int32(1)) == 0
    return cos, jnp.where(even, -sin, 0.0), jnp.where(even, 0.0, sin)


def _qkv_kernel(h_ref, w_ref, gq_ref, gk_ref, freq_ref, *refs, n_q_heads, n_k_heads, rope, cast_blocks):
    refs = list(refs)
    cast_srcs = [refs.pop(0) for _ in cast_blocks]
    o_ref, vt_ref = refs.pop(0), refs.pop(0)
    cast_dsts = [refs.pop(0) for _ in cast_blocks]
    (z_ref,) = refs
    t = pl.program_id(0)
    _cast_rows(t, cast_srcs, cast_dsts, cast_blocks)
    tm = h_ref.shape[0]

    @pl.when(t == 0)
    def _():
        z_ref[...] = jnp.zeros(z_ref.shape, F32)

    def finish_previous_tile():
        gain_q = gq_ref[...] * (ATTN_SCALE * LOG2_E)
        gain_k = gk_ref[...]
        if rope:
            cos, sa, sb = _rope_tables(jnp.maximum(t - 1, 0), tm, freq_ref[...])
        for head in range(n_q_heads + n_k_heads):
            cols = slice(head * HEAD_DIM, (head + 1) * HEAD_DIM)
            y = _rmsnorm(z_ref[:, cols], gain_q if head < n_q_heads else gain_k)
            if rope:
                y = y * cos + pltpu.roll(y, HEAD_DIM - 1, 1) * sa + pltpu.roll(y, 1, 1) * sb
            o_ref[:, cols] = y.astype(o_ref.dtype)
        vt_ref[0] = z_ref[:, (n_q_heads + n_k_heads) * HEAD_DIM:].T.astype(vt_ref.dtype)

    last = pl.num_programs(0) - 1

    @pl.when(t < last)
    def _():
        finish_previous_tile()
        z_ref[...] = _dot(h_ref[...], w_ref[...])

    @pl.when(t == last)
    def _():
        finish_previous_tile()


def _qkv_call(h, w_in, gq, gk, inv_freq, *, first_col, n_q_heads, rope, tm, cast_srcs=()):
    m = h.shape[0]
    n_i = m // tm
    qk_width = (n_q_heads + N_KV_HEADS) * HEAD_DIM
    width = qk_width + KV_W
    assert first_col + width == U_OFF and first_col % width == 0 and GRID_W & (GRID_W - 1) == 0
    vec = pl.BlockSpec((1, HEAD_DIM), lambda t: (0, 0))
    cast_blocks, cast_in, cast_out, cast_shapes, cast_bytes = _cast_plan(cast_srcs, lambda t: t, n_i + 1)
    pipelined = tm * D_MODEL * 2 + tm * width * 2 + cast_bytes
    resident = D_MODEL * width * 2 + 3 * tm * width * 4 + 4 * tm * HEAD_DIM * 4
    return pl.pallas_call(
        functools.partial(_qkv_kernel, n_q_heads=n_q_heads, n_k_heads=N_KV_HEADS, rope=rope,
                          cast_blocks=cast_blocks),
        grid=(n_i + 1,),
        in_specs=[
            pl.BlockSpec((tm, D_MODEL), lambda t: (jnp.minimum(t, n_i - 1), 0)),
            pl.BlockSpec((D_MODEL, width), lambda t: (0, first_col // width), pipeline_mode=pl.Buffered(1)),
            vec, vec, vec,
        ] + cast_in,
        out_specs=[
            pl.BlockSpec((tm, qk_width), lambda t: (jnp.maximum(t - 1, 0), 0)),
            pl.BlockSpec((1, KV_W, tm), lambda t: (jnp.maximum(t - 1, 0), 0, 0)),
        ] + cast_out,
        out_shape=[
            jax.ShapeDtypeStruct((m, qk_width), BF16),
            jax.ShapeDtypeStruct((n_i, KV_W, tm), BF16),
        ] + cast_shapes,
        scratch_shapes=[pltpu.VMEM((tm, width), F32)],
        compiler_params=pltpu.CompilerParams(
            dimension_semantics=("arbitrary",),
            vmem_limit_bytes=_vmem_limit(pipelined, resident)),
        name="qkv",
    )(h, w_in, gq, gk, inv_freq, *cast_srcs)


def _attn_kernel(q_ref, k_ref, vt_ref, kc_ref, vct_ref, *refs, cast_blocks):
    refs = list(refs)
    cast_srcs = [refs.pop(0) for _ in cast_blocks]
    o_ref = refs.pop(0)
    cast_dsts = [refs.pop(0) for _ in cast_blocks]
    s0_ref, s1_ref, m0_ref, m1_ref = refs
    i = pl.program_id(1)
    _cast_rows(pl.program_id(0) * pl.num_programs(1) + i, cast_srcs, cast_dsts, cast_blocks)
    tq = q_ref.shape[0]
    c_len = kc_ref.shape[0]
    n_chunks = vt_ref.shape[0]
    tk = vt_ref.shape[2]

    @pl.when(jnp.logical_and(pl.program_id(0) == 0, i == 0))
    def _():
        s1_ref[...] = jnp.zeros(s1_ref.shape, F32)
        m1_ref[...] = jnp.zeros(m1_ref.shape, F32)

    def step(s_cur, m_cur, s_prev, m_prev_ref):
        q = jnp.concatenate([q_ref[:, g * HEAD_DIM:(g + 1) * HEAD_DIM] for g in range(Q_PER_KV)], axis=0)
        rows = q.shape[0]
        m_prev = m_prev_ref[...]

        def sublane_groups(a):
            return a.reshape(a.shape[0] // 8, 8, rows)

        def pass1_chunk(k, key_rows, m8):
            s = lax.dot_general(k, q, (((1,), (1,)), ((), ())), preferred_element_type=F32)
            s_cur[key_rows, :] = s
            part = jnp.max(sublane_groups(s), axis=0)
            return part if m8 is None else jnp.maximum(m8, part)

        def pass2_chunk(vt, key_rows, l8, acc):
            p = jnp.exp2(s_prev[key_rows, :] - m_prev)
            part = jnp.sum(sublane_groups(p), axis=0)
            pv = _dot(vt, p.astype(BF16))
            return (part, pv) if l8 is None else (l8 + part, acc + pv)

        ctx_rows = slice(0, c_len)
        m8 = pass1_chunk(kc_ref[...], ctx_rows, None)
        l8, acc = pass2_chunk(vct_ref[0], ctx_rows, None, None)
        for t in range(n_chunks):
            key_rows = slice(c_len + t * tk, c_len + (t + 1) * tk)
            m8 = pass1_chunk(k_ref[t * tk:(t + 1) * tk, :], key_rows, m8)
            l8, acc = pass2_chunk(vt_ref[t], key_rows, l8, acc)

        m_cur[...] = jnp.max(m8, axis=0, keepdims=True)
        o = (acc / jnp.sum(l8, axis=0, keepdims=True)).T
        for g in range(Q_PER_KV):
            o_ref[:, g * HEAD_DIM:(g + 1) * HEAD_DIM] = o[g * tq:(g + 1) * tq].astype(o_ref.dtype)

    @pl.when(i % 2 == 0)
    def _():
        step(s0_ref, m0_ref, s1_ref, m1_ref)

    @pl.when(i % 2 == 1)
    def _():
        step(s1_ref, m1_ref, s0_ref, m0_ref)


def _attn_call(qk, vt, ck, vct, cast_srcs=()):
    s_len = qk.shape[0]
    c_len = ck.shape[0]
    n_chunks, _, tk = vt.shape
    n_tiles = s_len // ATTN_TQ
    gw = Q_PER_KV * HEAD_DIM
    rows = Q_PER_KV * ATTN_TQ
    cast_blocks, cast_in, cast_out, cast_shapes, cast_bytes = _cast_plan(
        cast_srcs, lambda kv, i: kv * (n_tiles + 1) + i, N_KV_HEADS * (n_tiles + 1))
    pipelined = 2 * ATTN_TQ * gw * 2 + 2 * s_len * HEAD_DIM * 2 + 2 * c_len * HEAD_DIM * 2 + cast_bytes
    resident = 2 * (s_len + c_len) * rows * 4 + 2 * tk * rows * 4
    return pl.pallas_call(
        functools.partial(_attn_kernel, cast_blocks=cast_blocks),
        grid=(N_KV_HEADS, n_tiles + 1),
        in_specs=[
            pl.BlockSpec((ATTN_TQ, gw), lambda kv, i: (jnp.minimum(i, n_tiles - 1), kv)),
            pl.BlockSpec((s_len, HEAD_DIM), lambda kv, i: (0, K_OFF // HEAD_DIM + kv)),
            pl.BlockSpec((n_chunks, HEAD_DIM, tk), lambda kv, i: (0, kv, 0)),
            pl.BlockSpec((c_len, HEAD_DIM), lambda kv, i: (0, kv)),
            pl.BlockSpec((1, HEAD_DIM, c_len), lambda kv, i: (0, kv, 0)),
        ] + cast_in,
        out_specs=[pl.BlockSpec((ATTN_TQ, gw), lambda kv, i: (jnp.maximum(i - 1, 0), kv))] + cast_out,
        out_shape=[jax.ShapeDtypeStruct((s_len, Q_W), BF16)] + cast_shapes,
        scratch_shapes=[
            pltpu.VMEM((c_len + s_len, rows), F32),
            pltpu.VMEM((c_len + s_len, rows), F32),
            pltpu.VMEM((1, rows), F32),
            pltpu.VMEM((1, rows), F32),
        ],
        compiler_params=pltpu.CompilerParams(
            dimension_semantics=("arbitrary", "arbitrary"),
            vmem_limit_bytes=_vmem_limit(pipelined, resident)),
        name="attn",
    )(qk, qk, vt, ck, vct, *cast_srcs)


def _gmlp_kernel(h_ref, *refs):
    n_tiles = GMLP_WIDTH // GMLP_TN
    wu_refs, wv_refs = refs[:n_tiles], refs[n_tiles:2 * n_tiles]
    lnw_ref, lnb_ref, ws_ref, bs_ref, o_ref, gu_ref, gv_ref = refs[2 * n_tiles:]
    tm = h_ref.shape[0]

    @pl.when(pl.program_id(0) == 0)
    def _():
        gu_ref[...] = jnp.zeros(gu_ref.shape, F32)
        gv_ref[...] = jnp.zeros(gv_ref.shape, F32)

    tiles = [slice(jj * GMLP_TN, (jj + 1) * GMLP_TN) for jj in range(n_tiles)]

    def finish_previous_tile():
        total = sum(gv_ref[:, tc].sum(axis=-1, keepdims=True) for tc in tiles)
        mu = total / GMLP_WIDTH
        sq = sum(((gv_ref[:, tc] - mu) ** 2).sum(axis=-1, keepdims=True) for tc in tiles)
        rstd = lax.rsqrt(sq / GMLP_WIDTH + EPS)
        groups_per_tile = GMLP_TN // GROUP_DIM
        for jj, tc in enumerate(tiles):
            vn = ((gv_ref[:, tc] - mu) * rstd * lnw_ref[:, tc] + lnb_ref[:, tc]).astype(BF16)
            for gg in range(groups_per_tile):
                g = jj * groups_per_tile + gg
                cols = slice(g * GROUP_DIM, (g + 1) * GROUP_DIM)
                bias = bs_ref[:, g:g + 1]
                for c in range(tm // CHUNK):
                    rows = slice(c * CHUNK, (c + 1) * CHUNK)
                    mixed = _dot(ws_ref[g], vn[rows, gg * GROUP_DIM:(gg + 1) * GROUP_DIM]) + bias
                    o_ref[rows, cols] = (gu_ref[rows, cols] * mixed).astype(o_ref.dtype)

    def project_this_tile():
        h = h_ref[...]
        for tc, wu_ref in zip(tiles, wu_refs):
            gu_ref[:, tc] = _gelu_exact(_dot(h, wu_ref[...]))
        for tc, wv_ref in zip(tiles, wv_refs):
            gv_ref[:, tc] = _gelu_exact(_dot(h, wv_ref[...]))

    last = pl.num_programs(0) - 1

    @pl.when(pl.program_id(0) < last)
    def _():
        finish_previous_tile()
        project_this_tile()

    @pl.when(pl.program_id(0) == last)
    def _():
        finish_previous_tile()


def _gmlp_call(h, w_in, lnw, lnb, w_s, b_s_t, *, tm):
    m = h.shape[0]
    n_i = m // tm
    n_tiles = GMLP_WIDTH // GMLP_TN
    vec = pl.BlockSpec((1, GMLP_WIDTH), lambda t: (0, 0))

    def w_tile(first_col, jj):
        return pl.BlockSpec((D_MODEL, GMLP_TN), lambda t: (0, first_col // GMLP_TN + jj),
                            pipeline_mode=pl.Buffered(1))

    pipelined = (tm * D_MODEL * 2 + tm * GMLP_WIDTH * 2 + GMLP_GROUPS * CHUNK * CHUNK * 2 + CHUNK * 128 * 4)
    resident = 2 * D_MODEL * GMLP_WIDTH * 2 + 2 * tm * GMLP_WIDTH * 4 + 6 * tm * GMLP_TN * 4
    return pl.pallas_call(
        _gmlp_kernel,
        grid=(n_i + 1,),
        in_specs=[pl.BlockSpec((tm, D_MODEL), lambda t: (jnp.minimum(t, n_i - 1), 0))]
        + [w_tile(U_OFF, jj) for jj in range(n_tiles)]
        + [w_tile(GV_OFF, jj) for jj in range(n_tiles)]
        + [
            vec, vec,
            pl.BlockSpec((GMLP_GROUPS, CHUNK, CHUNK), lambda t: (0, 0, 0)),
            pl.BlockSpec((CHUNK, GMLP_GROUPS), lambda t: (0, 0)),
        ],
        out_specs=pl.BlockSpec((tm, GMLP_WIDTH), lambda t: (jnp.maximum(t - 1, 0), 0)),
        out_shape=jax.ShapeDtypeStruct((m, GMLP_WIDTH), BF16),
        scratch_shapes=[pltpu.VMEM((tm, GMLP_WIDTH), F32), pltpu.VMEM((tm, GMLP_WIDTH), F32)],
        compiler_params=pltpu.CompilerParams(
            dimension_semantics=("arbitrary",),
            vmem_limit_bytes=_vmem_limit(pipelined, resident)),
        name="gmlp",
    )(h, *([w_in] * (2 * n_tiles)), lnw, lnb, w_s, b_s_t)


def _merge_kernel(x_ref, mod_ref, h_ref, attn_ref, gm_ref, wga_ref, wgb_ref, bg_ref, wba_ref, wbg_ref, wo_ref,
                  o_ref, *, gate_row):
    j = pl.program_id(1)

    def column_tile_product():
        h = h_ref[...]
        ga = jax.nn.sigmoid(_dot(h, wga_ref[...]) + bg_ref[0:1, :])
        gb = jax.nn.sigmoid(_dot(h, wgb_ref[...]) + bg_ref[1:2, :])
        merged = ga * _dot(attn_ref[...], wba_ref[...]) + gb * _dot(gm_ref[...], wbg_ref[...])
        wo_rows = pl.ds(pl.multiple_of(j * MERGE_TN, MERGE_TN), MERGE_TN)
        return _dot(merged.astype(BF16), wo_ref[wo_rows, :])

    @pl.when(j == 0)
    def _():
        o_ref[...] = column_tile_product()

    last = pl.num_programs(1) - 1

    @pl.when(jnp.logical_and(j > 0, j < last))
    def _():
        o_ref[...] += column_tile_product()

    @pl.when(j == last)
    def _():
        o_ref[...] += column_tile_product()
        o_ref[...] = x_ref[...] + mod_ref[gate_row:gate_row + 1, :] * o_ref[...]


def _merge_call(x, mods, h, attn, gm, w_in, b_gate, w_ba, w_bg, w_o, *, gate_row, tm):
    m = x.shape[0]
    n_j = D_MODEL // MERGE_TN
    row_bf = pl.BlockSpec((tm, D_MODEL), lambda i, j: (i, 0))
    row_f32 = pl.BlockSpec((tm, D_MODEL), lambda i, j: (i, 0))
    col_w = pl.BlockSpec((D_MODEL, MERGE_TN), lambda i, j: (0, j))
    pipelined = 2 * tm * D_MODEL * 4 + 3 * tm * D_MODEL * 2 + 4 * D_MODEL * MERGE_TN * 2
    resident = D_MODEL * D_MODEL * 2 + 8 * tm * MERGE_TN * 4 + (2 << 20)
    return pl.pallas_call(
        functools.partial(_merge_kernel, gate_row=gate_row),
        grid=(m // tm, n_j),
        in_specs=[
            row_f32,
            pl.BlockSpec((N_MOD, D_MODEL), lambda i, j: (0, 0)),
            row_bf, row_bf, row_bf,
            pl.BlockSpec((D_MODEL, MERGE_TN), lambda i, j: (0, GATE_OFF // MERGE_TN + j)),
            pl.BlockSpec((D_MODEL, MERGE_TN), lambda i, j: (0, (GATE_OFF + D_MODEL) // MERGE_TN + j)),
            pl.BlockSpec((2, MERGE_TN), lambda i, j: (0, j)),
            col_w, col_w,
            pl.BlockSpec((D_MODEL, D_MODEL), lambda i, j: (0, 0), pipeline_mode=pl.Buffered(1)),
        ],
        out_specs=row_f32,
        out_shape=jax.ShapeDtypeStruct((m, D_MODEL), F32),
        compiler_params=pltpu.CompilerParams(
            dimension_semantics=("parallel", "arbitrary"),
            vmem_limit_bytes=_vmem_limit(pipelined, resident)),
        name="merge",
    )(x, mods, h, attn, gm, w_in, w_in, b_gate, w_ba, w_bg, w_o)


def kernel(x, c, ctx, c_ctx, w_mod, b_mod, norm_w, w_ffn1_in, w_ffn1_out, w_ffn2_in, w_ffn2_out, w_in, b_gate,
           q_norm_w, k_norm_w, gmlp_ln_w, gmlp_ln_b, w_spatial, b_spatial, w_branch_attn, w_branch_gmlp, w_out,
           final_norm_w):
    batch, seq, d = x.shape
    assert batch == 1 and d == D_MODEL and seq == SEQ and seq % GRID_W == 0
    assert w_mod.shape[0] == 1 and ctx.shape == (1, CTX_LEN, D_MODEL) and w_in.shape[-1] == IN_W

    x0 = x[0]
    ctx0 = ctx[0]
    ws = w_spatial[0].astype(BF16)
    nw = norm_w[0]

    c_rows = jnp.zeros((MOD_ROWS, D_MODEL), F32).at[0].set(c[0]).at[1].set(c_ctx)
    mods = _mod_call(c_rows, w_mod[0], b_mod).reshape(MOD_ROWS, N_MOD, D_MODEL)
    mx, mc = mods[0], mods[1]

    _, hc, w1a, w1b, w1o = _ffn_call(ctx0, mc, nw[0:1], nw[1:2], w_ffn1_in[0], w_ffn1_in[0], w_ffn1_out[0],
                                     b_col0=D_FF, mod_row=0, post_mod_row=3, tm=CTX_LEN, tf=FFN_CTX_TF,
                                     emit_weights=True)
    x1, hx, wi = _ffn_call(x0, mx, nw[0:1], nw[1:2], w1a, w1b, w1o, b_col0=0, mod_row=0, post_mod_row=3,
                           tm=FFN_TM, tf=FFN1_TF, cast_srcs=(w_in[0],))

    gq, gk = q_norm_w[0][None, :], k_norm_w[0][None, :]
    axis_dim = HEAD_DIM // 2
    inv_freq = ROPE_THETA ** (-jnp.arange(0, axis_dim, 2, dtype=F32) / axis_dim)
    inv_freq_lanes = jnp.tile(jnp.repeat(inv_freq, 2), 2)[None, :]
    qk, vt, wba, wbg, wo = _qkv_call(hx, wi, gq, gk, inv_freq_lanes, first_col=0, n_q_heads=N_Q_HEADS, rope=True,
                                     tm=QKV_TM, cast_srcs=(w_branch_attn[0], w_branch_gmlp[0], w_out[0]))
    ck, vct = _qkv_call(hc, wi, gq, gk, inv_freq_lanes, first_col=K_OFF, n_q_heads=0, rope=False, tm=CTX_LEN)
    attn, w2i, w2o = _attn_call(qk, vt, ck, vct, cast_srcs=(w_ffn2_in[0], w_ffn2_out[0]))

    gm = _gmlp_call(hx, wi, gmlp_ln_w, gmlp_ln_b, ws, b_spatial[0].T, tm=GMLP_TM)
    x2 = _merge_call(x1, mx, hx, attn, gm, wi, b_gate[0], wba, wbg, wo, gate_row=5, tm=MERGE_TM)

    (out,) = _ffn_call(x2, mx, nw[2:3], final_norm_w[None, :], w2i, w2i, w2o, b_col0=D_FF, mod_row=6,
                       post_mod_row=None, tm=FFN_TM, tf=FFN2_TF)
    return out[None]
```

```python
import functools
import math

import jax
import jax.numpy as jnp
from jax import lax
from jax.experimental import pallas as pl
from jax.experimental.pallas import tpu as pltpu

D_MODEL = 2048
SEQ = 8192
CTX_LEN = 256
GRID_W = 64
HEAD_DIM = 128
N_Q_HEADS = 16
N_KV_HEADS = 4
Q_PER_KV = N_Q_HEADS // N_KV_HEADS
ROPE_THETA = 10000.0
ATTN_SCALE = HEAD_DIM ** -0.5
GMLP_GROUPS = 16
GMLP_WIDTH = 2048
GROUP_DIM = GMLP_WIDTH // GMLP_GROUPS
CHUNK = 128
D_FF = 5632
MACARON_WEIGHT = 0.5
N_MOD = 9
EPS = 1e-6
LOG2_E = math.log2(math.e)

Q_W = N_Q_HEADS * HEAD_DIM
KV_W = N_KV_HEADS * HEAD_DIM
K_OFF = Q_W
V_OFF = K_OFF + KV_W
U_OFF = V_OFF + KV_W
GV_OFF = U_OFF + GMLP_WIDTH
GATE_OFF = GV_OFF + GMLP_WIDTH
IN_W = GATE_OFF + 2 * D_MODEL

V7X_VMEM_BYTES = 64 * 1024 * 1024
VMEM_CAP_BYTES = V7X_VMEM_BYTES - 6 * 1024 * 1024

BF16_SUBLANES = 16
MOD_ROWS = BF16_SUBLANES
CAST_BLOCK_BYTES = 1024 * 1024
MOD_TN = 1024
FFN_TM = 1024
FFN1_TF = 256
FFN2_TF = 512
FFN_CTX_TF = 512
QKV_TM = 512
ATTN_TQ = 128
GMLP_TM = 512
GMLP_TN = 1024
MERGE_TM = 1024
MERGE_TN = 512
OUTPROJ_TM = 512

F32 = jnp.float32
BF16 = jnp.bfloat16


def _vmem_limit(pipelined_bytes, resident_bytes):
    return int(min(2 * pipelined_bytes + resident_bytes, VMEM_CAP_BYTES))


def _dot(a, b):
    return jnp.dot(a, b, preferred_element_type=F32)


def _rms_factor(x):
    return lax.rsqrt(jnp.mean(x * x, axis=-1, keepdims=True) + EPS)


def _rmsnorm(x, w):
    return x * _rms_factor(x) * w


def _gelu_exact(x):
    return 0.5 * x * (1 + lax.erf(x * (2.0 ** -0.5)))


def _mod_kernel(c_ref, w_ref, b_ref, o_ref):
    sc = jax.nn.silu(c_ref[...])
    o_ref[...] = _dot(sc.astype(BF16), w_ref[...].astype(BF16)) + b_ref[...]


def _mod_call(c_rows, w_mod, b_mod):
    n = w_mod.shape[1]
    return pl.pallas_call(
        _mod_kernel,
        grid=(n // MOD_TN,),
        in_specs=[
            pl.BlockSpec((MOD_ROWS, D_MODEL), lambda j: (0, 0)),
            pl.BlockSpec((D_MODEL, MOD_TN), lambda j: (0, j)),
            pl.BlockSpec((1, MOD_TN), lambda j: (0, j)),
        ],
        out_specs=pl.BlockSpec((MOD_ROWS, MOD_TN), lambda j: (0, j)),
        out_shape=jax.ShapeDtypeStruct((MOD_ROWS, n), F32),
        compiler_params=pltpu.CompilerParams(
            dimension_semantics=("arbitrary",),
            vmem_limit_bytes=_vmem_limit(D_MODEL * MOD_TN * 4, D_MODEL * MOD_TN * 2 + (4 << 20))),
        name="mod",
    )(c_rows, w_mod, b_mod)


def _cast_row_block(rows, cols):
    rb = BF16_SUBLANES
    while rows % (2 * rb) == 0 and 2 * rb * cols * 4 <= CAST_BLOCK_BYTES:
        rb *= 2
    return rb


def _cast_plan(weights, step_of, n_steps):
    n_blocks, in_specs, out_specs, out_shapes, nbytes = [], [], [], [], 0
    for w in weights:
        rows, cols = w.shape
        rb = _cast_row_block(rows, cols)
        nb = rows // rb
        assert nb <= n_steps

        def index(*g, nb=nb):
            return (jnp.minimum(step_of(*g), nb - 1), 0)

        n_blocks.append(nb)
        in_specs.append(pl.BlockSpec((rb, cols), index))
        out_specs.append(pl.BlockSpec((rb, cols), index))
        out_shapes.append(jax.ShapeDtypeStruct((rows, cols), BF16))
        nbytes += rb * cols * 6
    return tuple(n_blocks), in_specs, out_specs, out_shapes, nbytes


def _cast_rows(step, srcs, dsts, n_blocks):
    for src, dst, nb in zip(srcs, dsts, n_blocks):
        @pl.when(step < nb)
        def _():
            dst[...] = src[...].astype(dst.dtype)


def _ffn_kernel(x_ref, mod_ref, nw_ref, pnw_ref, wa_ref, wb_ref, wo_ref, *refs, mod_row, post_mod_row,
                emit_weights, cast_blocks):
    refs = list(refs)
    cast_srcs = [refs.pop(0) for _ in cast_blocks]
    y_ref = refs.pop(0) if post_mod_row is not None else None
    o_ref = refs.pop(0)
    w_outs = [refs.pop(0) for _ in range(3)] if emit_weights else None
    cast_dsts = [refs.pop(0) for _ in cast_blocks]
    h_ref, r_ref = refs
    acc_ref = o_ref if post_mod_row is None else y_ref
    j = pl.program_id(1)
    _cast_rows(pl.program_id(0) * pl.num_programs(1) + j, cast_srcs, cast_dsts, cast_blocks)

    def hidden_chunk_product():
        wa, wb, wo = wa_ref[...], wb_ref[...], wo_ref[...]
        if emit_weights:
            wa, wb, wo = wa.astype(BF16), wb.astype(BF16), wo.astype(BF16)
            for w_out, w in zip(w_outs, (wa, wb, wo)):
                w_out[...] = w
        h = h_ref[...]
        a = _dot(h, wa)
        b = _dot(h, wb)
        act = jax.nn.silu(a) * b
        return _dot(act.astype(BF16), wo)

    @pl.when(j == 0)
    def _():
        shift = mod_ref[mod_row:mod_row + 1, :]
        scale = mod_ref[mod_row + 1:mod_row + 2, :]
        r_ref[...] = _rms_factor(x_ref[...])
        h = x_ref[...] * r_ref[...] * nw_ref[...] * (1 + scale) + shift
        h_ref[...] = h.astype(BF16)
        acc_ref[...] = hidden_chunk_product()

    last = pl.num_programs(1) - 1

    @pl.when(jnp.logical_and(j > 0, j < last))
    def _():
        acc_ref[...] += hidden_chunk_product()

    @pl.when(j == last)
    def _():
        acc_ref[...] += hidden_chunk_product()
        gate = mod_ref[mod_row + 2:mod_row + 3, :]
        y = x_ref[...] + MACARON_WEIGHT * gate * acc_ref[...]
        acc_ref[...] = y
        r_ref[...] = _rms_factor(y)
        yn = acc_ref[...] * r_ref[...] * pnw_ref[...]
        if post_mod_row is None:
            o_ref[...] = yn
        else:
            pshift = mod_ref[post_mod_row:post_mod_row + 1, :]
            pscale = mod_ref[post_mod_row + 1:post_mod_row + 2, :]
            o_ref[...] = (yn * (1 + pscale) + pshift).astype(BF16)


def _ffn_call(x, mods, nw, pnw, wa, wb, wo, *, b_col0, mod_row, post_mod_row, tm, tf, emit_weights=False,
              cast_srcs=()):
    m = x.shape[0]
    n_f = D_FF // tf
    b_tile0 = b_col0 // tf
    n_i = m // tm
    assert not emit_weights or n_i == 1
    w_bytes = 4 if emit_weights else 2
    row_f32 = pl.BlockSpec((tm, D_MODEL), lambda i, j: (i, 0))
    vec = pl.BlockSpec((1, D_MODEL), lambda i, j: (0, 0))
    cast_blocks, cast_in, cast_out, cast_shapes, cast_bytes = _cast_plan(
        cast_srcs, lambda i, j: i * n_f + j, n_i * n_f)
    out_shape, out_specs = [], []
    if post_mod_row is not None:
        out_shape += [jax.ShapeDtypeStruct((m, D_MODEL), F32), jax.ShapeDtypeStruct((m, D_MODEL), BF16)]
        out_specs += [row_f32, pl.BlockSpec((tm, D_MODEL), lambda i, j: (i, 0))]
        out_bytes = tm * D_MODEL * 6
    else:
        out_shape += [jax.ShapeDtypeStruct((m, D_MODEL), F32)]
        out_specs += [row_f32]
        out_bytes = tm * D_MODEL * 4
    if emit_weights:
        out_shape += [jax.ShapeDtypeStruct((D_MODEL, D_FF), BF16)] * 2 + [jax.ShapeDtypeStruct((D_FF, D_MODEL), BF16)]
        out_specs += [pl.BlockSpec((D_MODEL, tf), lambda i, j: (0, j))] * 2
        out_specs += [pl.BlockSpec((tf, D_MODEL), lambda i, j: (j, 0))]
        out_bytes += 3 * D_MODEL * tf * 2
    pipelined = tm * D_MODEL * 4 + out_bytes + 3 * D_MODEL * tf * w_bytes + cast_bytes
    resident = tm * D_MODEL * 2 + 6 * tm * tf * 4 + (2 << 20)
    return pl.pallas_call(
        functools.partial(_ffn_kernel, mod_row=mod_row, post_mod_row=post_mod_row, emit_weights=emit_weights,
                          cast_blocks=cast_blocks),
        grid=(n_i, n_f),
        in_specs=[
            row_f32,
            pl.BlockSpec((N_MOD, D_MODEL), lambda i, j: (0, 0)),
            vec,
            vec,
            pl.BlockSpec((D_MODEL, tf), lambda i, j: (0, j)),
            pl.BlockSpec((D_MODEL, tf), lambda i, j: (0, j + b_tile0)),
            pl.BlockSpec((tf, D_MODEL), lambda i, j: (j, 0)),
        ] + cast_in,
        out_specs=out_specs + cast_out,
        out_shape=out_shape + cast_shapes,
        scratch_shapes=[pltpu.VMEM((tm, D_MODEL), BF16), pltpu.VMEM((tm, 1), F32)],
        compiler_params=pltpu.CompilerParams(
            dimension_semantics=("arbitrary", "arbitrary"),
            vmem_limit_bytes=_vmem_limit(pipelined, resident)),
        name="ffn",
    )(x, mods, nw, pnw, wa, wb, wo, *cast_srcs)


def _rope_tables(tile, tm, inv_freq):
    t = tile * tm + lax.broadcasted_iota(jnp.int32, (tm, HEAD_DIM), 0)
    lane = lax.broadcasted_iota(jnp.int32, (tm, HEAD_DIM), 1)
    row = lax.shift_right_logical(t, jnp.int32(GRID_W.bit_length() - 1))
    col = lax.bitwise_and(t, jnp.int32(GRID_W - 1))
    ang = jnp.where(lane < HEAD_DIM // 2, row, col).astype(F32) * inv_freq
    cos, sin = jnp.cos(ang), jnp.sin(ang)
    even = lax.bitwise_and(lane, jnp.int32(1)) == 0
    return cos, jnp.where(even, -sin, 0.0), jnp.where(even, 0.0, sin)


def _qkv_kernel(h_ref, w_ref, gq_ref, gk_ref, freq_ref, *refs, n_q_heads, n_k_heads, rope, cast_blocks):
    refs = list(refs)
    cast_srcs = [refs.pop(0) for _ in cast_blocks]
    o_ref, vt_ref = refs.pop(0), refs.pop(0)
    cast_dsts = [refs.pop(0) for _ in cast_blocks]
    (z_ref,) = refs
    t = pl.program_id(0)
    _cast_rows(t, cast_srcs, cast_dsts, cast_blocks)
    tm = h_ref.shape[0]

    @pl.when(t == 0)
    def _():
        z_ref[...] = jnp.zeros(z_ref.shape, F32)

    def finish_previous_tile():
        gain_q = gq_ref[...] * (ATTN_SCALE * LOG2_E)
        gain_k = gk_ref[...]
        if rope:
            cos, sa, sb = _rope_tables(jnp.maximum(t - 1, 0), tm, freq_ref[...])
        for head in range(n_q_heads + n_k_heads):
            cols = slice(head * HEAD_DIM, (head + 1) * HEAD_DIM)
            y = _rmsnorm(z_ref[:, cols], gain_q if head < n_q_heads else gain_k)
            if rope:
                y = y * cos + pltpu.roll(y, HEAD_DIM - 1, 1) * sa + pltpu.roll(y, 1, 1) * sb
            o_ref[:, cols] = y.astype(o_ref.dtype)
        vt_ref[0] = z_ref[:, (n_q_heads + n_k_heads) * HEAD_DIM:].T.astype(vt_ref.dtype)

    last = pl.num_programs(0) - 1

    @pl.when(t < last)
    def _():
        finish_previous_tile()
        z_ref[...] = _dot(h_ref[...], w_ref[...])

    @pl.when(t == last)
    def _():
        finish_previous_tile()


def _qkv_call(h, w_in, gq, gk, inv_freq, *, first_col, n_q_heads, rope, tm, cast_srcs=()):
    m = h.shape[0]
    n_i = m // tm
    qk_width = (n_q_heads + N_KV_HEADS) * HEAD_DIM
    width = qk_width + KV_W
    assert first_col + width == U_OFF and first_col % width == 0 and GRID_W & (GRID_W - 1) == 0
    vec = pl.BlockSpec((1, HEAD_DIM), lambda t: (0, 0))
    cast_blocks, cast_in, cast_out, cast_shapes, cast_bytes = _cast_plan(cast_srcs, lambda t: t, n_i + 1)
    pipelined = tm * D_MODEL * 2 + tm * width * 2 + cast_bytes
    resident = D_MODEL * width * 2 + 3 * tm * width * 4 + 4 * tm * HEAD_DIM * 4
    return pl.pallas_call(
        functools.partial(_qkv_kernel, n_q_heads=n_q_heads, n_k_heads=N_KV_HEADS, rope=rope,
                          cast_blocks=cast_blocks),
        grid=(n_i + 1,),
        in_specs=[
            pl.BlockSpec((tm, D_MODEL), lambda t: (jnp.minimum(t, n_i - 1), 0)),
            pl.BlockSpec((D_MODEL, width), lambda t: (0, first_col // width), pipeline_mode=pl.Buffered(1)),
            vec, vec, vec,
        ] + cast_in,
        out_specs=[
            pl.BlockSpec((tm, qk_width), lambda t: (jnp.maximum(t - 1, 0), 0)),
            pl.BlockSpec((1, KV_W, tm), lambda t: (jnp.maximum(t - 1, 0), 0, 0)),
        ] + cast_out,
        out_shape=[
            jax.ShapeDtypeStruct((m, qk_width), BF16),
            jax.ShapeDtypeStruct((n_i, KV_W, tm), BF16),
        ] + cast_shapes,
        scratch_shapes=[pltpu.VMEM((tm, width), F32)],
        compiler_params=pltpu.CompilerParams(
            dimension_semantics=("arbitrary",),
            vmem_limit_bytes=_vmem_limit(pipelined, resident)),
        name="qkv",
    )(h, w_in, gq, gk, inv_freq, *cast_srcs)


def _attn_kernel(q_ref, k_ref, vt_ref, kc_ref, vct_ref, *refs, cast_blocks):
    refs = list(refs)
    cast_srcs = [refs.pop(0) for _ in cast_blocks]
    o_ref = refs.pop(0)
    cast_dsts = [refs.pop(0) for _ in cast_blocks]
    s0_ref, s1_ref, m0_ref, m1_ref = refs
    i = pl.program_id(1)
    _cast_rows(pl.program_id(0) * pl.num_programs(1) + i, cast_srcs, cast_dsts, cast_blocks)
    tq = q_ref.shape[0]
    c_len = kc_ref.shape[0]
    n_chunks = vt_ref.shape[0]
    tk = vt_ref.shape[2]

    @pl.when(jnp.logical_and(pl.program_id(0) == 0, i == 0))
    def _():
        s1_ref[...] = jnp.zeros(s1_ref.shape, F32)
        m1_ref[...] = jnp.zeros(m1_ref.shape, F32)

    def step(s_cur, m_cur, s_prev, m_prev_ref):
        q = jnp.concatenate([q_ref[:, g * HEAD_DIM:(g + 1) * HEAD_DIM] for g in range(Q_PER_KV)], axis=0)
        rows = q.shape[0]
        m_prev = m_prev_ref[...]

        def sublane_groups(a):
            return a.reshape(a.shape[0] // 8, 8, rows)

        def pass1_chunk(k, key_rows, m8):
            s = lax.dot_general(k, q, (((1,), (1,)), ((), ())), preferred_element_type=F32)
            s_cur[key_rows, :] = s
            part = jnp.max(sublane_groups(s), axis=0)
            return part if m8 is None else jnp.maximum(m8, part)

        def pass2_chunk(vt, key_rows, l8, acc):
            p = jnp.exp2(s_prev[key_rows, :] - m_prev)
            part = jnp.sum(sublane_groups(p), axis=0)
            pv = _dot(vt, p.astype(BF16))
            return (part, pv) if l8 is None else (l8 + part, acc + pv)

        ctx_rows = slice(0, c_len)
        m8 = pass1_chunk(kc_ref[...], ctx_rows, None)
        l8, acc = pass2_chunk(vct_ref[0], ctx_rows, None, None)
        for t in range(n_chunks):
            key_rows = slice(c_len + t * tk, c_len + (t + 1) * tk)
            m8 = pass1_chunk(k_ref[t * tk:(t + 1) * tk, :], key_rows, m8)
            l8, acc = pass2_chunk(vt_ref[t], key_rows, l8, acc)

        m_cur[...] = jnp.max(m8, axis=0, keepdims=True)
        o = (acc / jnp.sum(l8, axis=0, keepdims=True)).T
        for g in range(Q_PER_KV):
            o_ref[:, g * HEAD_DIM:(g + 1) * HEAD_DIM] = o[g * tq:(g + 1) * tq].astype(o_ref.dtype)

    @pl.when(i % 2 == 0)
    def _():
        step(s0_ref, m0_ref, s1_ref, m1_ref)

    @pl.when(i % 2 == 1)
    def _():
        step(s1_ref, m1_ref, s0_ref, m0_ref)


def _attn_call(qk, vt, ck, vct, cast_srcs=()):
    s_len = qk.shape[0]
    c_len = ck.shape[0]
    n_chunks, _, tk = vt.shape
    n_tiles = s_len // ATTN_TQ
    gw = Q_PER_KV * HEAD_DIM
    rows = Q_PER_KV * ATTN_TQ
    cast_blocks, cast_in, cast_out, cast_shapes, cast_bytes = _cast_plan(
        cast_srcs, lambda kv, i: kv * (n_tiles + 1) + i, N_KV_HEADS * (n_tiles + 1))
    pipelined = 2 * ATTN_TQ * gw * 2 + 2 * s_len * HEAD_DIM * 2 + 2 * c_len * HEAD_DIM * 2 + cast_bytes
    resident = 2 * (s_len + c_len) * rows * 4 + 2 * tk * rows * 4
    return pl.pallas_call(
        functools.partial(_attn_kernel, cast_blocks=cast_blocks),
        grid=(N_KV_HEADS, n_tiles + 1),
        in_specs=[
            pl.BlockSpec((ATTN_TQ, gw), lambda kv, i: (jnp.minimum(i, n_tiles - 1), kv)),
            pl.BlockSpec((s_len, HEAD_DIM), lambda kv, i: (0, K_OFF // HEAD_DIM + kv)),
            pl.BlockSpec((n_chunks, HEAD_DIM, tk), lambda kv, i: (0, kv, 0)),
            pl.BlockSpec((c_len, HEAD_DIM), lambda kv, i: (0, kv)),
            pl.BlockSpec((1, HEAD_DIM, c_len), lambda kv, i: (0, kv, 0)),
        ] + cast_in,
        out_specs=[pl.BlockSpec((ATTN_TQ, gw), lambda kv, i: (jnp.maximum(i - 1, 0), kv))] + cast_out,
        out_shape=[jax.ShapeDtypeStruct((s_len, Q_W), BF16)] + cast_shapes,
        scratch_shapes=[
            pltpu.VMEM((c_len + s_len, rows), F32),
            pltpu.VMEM((c_len + s_len, rows), F32),
            pltpu.VMEM((1, rows), F32),
            pltpu.VMEM((1, rows), F32),
        ],
        compiler_params=pltpu.CompilerParams(
            dimension_semantics=("arbitrary", "arbitrary"),
            vmem_limit_bytes=_vmem_limit(pipelined, resident)),
        name="attn",
    )(qk, qk, vt, ck, vct, *cast_srcs)


def _gmlp_kernel(h_ref, *refs):
    n_tiles = GMLP_WIDTH // GMLP_TN
    wu_refs, wv_refs = refs[:n_tiles], refs[n_tiles:2 * n_tiles]
    lnw_ref, lnb_ref, ws_ref, bs_ref, o_ref, gu_ref, gv_ref = refs[2 * n_tiles:]
    tm = h_ref.shape[0]

    @pl.when(pl.program_id(0) == 0)
    def _():
        gu_ref[...] = jnp.zeros(gu_ref.shape, F32)
        gv_ref[...] = jnp.zeros(gv_ref.shape, F32)

    tiles = [slice(jj * GMLP_TN, (jj + 1) * GMLP_TN) for jj in range(n_tiles)]

    def finish_previous_tile():
        total = sum(gv_ref[:, tc].sum(axis=-1, keepdims=True) for tc in tiles)
        mu = total / GMLP_WIDTH
        sq = sum(((gv_ref[:, tc] - mu) ** 2).sum(axis=-1, keepdims=True) for tc in tiles)
        rstd = lax.rsqrt(sq / GMLP_WIDTH + EPS)
        groups_per_tile = GMLP_TN // GROUP_DIM
        for jj, tc in enumerate(tiles):
            vn = ((gv_ref[:, tc] - mu) * rstd * lnw_ref[:, tc] + lnb_ref[:, tc]).astype(BF16)
            for gg in range(groups_per_tile):
                g = jj * groups_per_tile + gg
                cols = slice(g * GROUP_DIM, (g + 1) * GROUP_DIM)
                bias = bs_ref[:, g:g + 1]
                for c in range(tm // CHUNK):
                    rows = slice(c * CHUNK, (c + 1) * CHUNK)
                    mixed = _dot(ws_ref[g], vn[rows, gg * GROUP_DIM:(gg + 1) * GROUP_DIM]) + bias
                    o_ref[rows, cols] = (gu_ref[rows, cols] * mixed).astype(o_ref.dtype)

    def project_this_tile():
        h = h_ref[...]
        for tc, wu_ref in zip(tiles, wu_refs):
            gu_ref[:, tc] = _gelu_exact(_dot(h, wu_ref[...]))
        for tc, wv_ref in zip(tiles, wv_refs):
            gv_ref[:, tc] = _gelu_exact(_dot(h, wv_ref[...]))

    last = pl.num_programs(0) - 1

    @pl.when(pl.program_id(0) < last)
    def _():
        finish_previous_tile()
        project_this_tile()

    @pl.when(pl.program_id(0) == last)
    def _():
        finish_previous_tile()


def _gmlp_call(h, w_in, lnw, lnb, w_s, b_s_t, *, tm):
    m = h.shape[0]
    n_i = m // tm
    n_tiles = GMLP_WIDTH // GMLP_TN
    vec = pl.BlockSpec((1, GMLP_WIDTH), lambda t: (0, 0))

    def w_tile(first_col, jj):
        return pl.BlockSpec((D_MODEL, GMLP_TN), lambda t: (0, first_col // GMLP_TN + jj),
                            pipeline_mode=pl.Buffered(1))

    pipelined = (tm * D_MODEL * 2 + tm * GMLP_WIDTH * 2 + GMLP_GROUPS * CHUNK * CHUNK * 2 + CHUNK * 128 * 4)
    resident = 2 * D_MODEL * GMLP_WIDTH * 2 + 2 * tm * GMLP_WIDTH * 4 + 6 * tm * GMLP_TN * 4
    return pl.pallas_call(
        _gmlp_kernel,
        grid=(n_i + 1,),
        in_specs=[pl.BlockSpec((tm, D_MODEL), lambda t: (jnp.minimum(t, n_i - 1), 0))]
        + [w_tile(U_OFF, jj) for jj in range(n_tiles)]
        + [w_tile(GV_OFF, jj) for jj in range(n_tiles)]
        + [
            vec, vec,
            pl.BlockSpec((GMLP_GROUPS, CHUNK, CHUNK), lambda t: (0, 0, 0)),
            pl.BlockSpec((CHUNK, GMLP_GROUPS), lambda t: (0, 0)),
        ],
        out_specs=pl.BlockSpec((tm, GMLP_WIDTH), lambda t: (jnp.maximum(t - 1, 0), 0)),
        out_shape=jax.ShapeDtypeStruct((m, GMLP_WIDTH), BF16),
        scratch_shapes=[pltpu.VMEM((tm, GMLP_WIDTH), F32), pltpu.VMEM((tm, GMLP_WIDTH), F32)],
        compiler_params=pltpu.CompilerParams(
            dimension_semantics=("arbitrary",),
            vmem_limit_bytes=_vmem_limit(pipelined, resident)),
        name="gmlp",
    )(h, *([w_in] * (2 * n_tiles)), lnw, lnb, w_s, b_s_t)


def _branch_kernel(h_ref, attn_ref, gm_ref, wga_ref, wgb_ref, bg_ref, wba_ref, wbg_ref, o_ref):
    h = h_ref[...]
    ga = jax.nn.sigmoid(_dot(h, wga_ref[...]) + bg_ref[0:1, :])
    gb = jax.nn.sigmoid(_dot(h, wgb_ref[...]) + bg_ref[1:2, :])
    merged = ga * _dot(attn_ref[...], wba_ref[...]) + gb * _dot(gm_ref[...], wbg_ref[...])
    o_ref[...] = merged.astype(o_ref.dtype)


def _branch_call(h, attn, gm, w_in, b_gate, w_ba, w_bg, *, tm):
    m = h.shape[0]
    n_j = D_MODEL // MERGE_TN
    row_bf = pl.BlockSpec((tm, D_MODEL), lambda i, j: (i, 0))
    col_w = pl.BlockSpec((D_MODEL, MERGE_TN), lambda i, j: (0, j))
    pipelined = 3 * tm * D_MODEL * 2 + 4 * D_MODEL * MERGE_TN * 2 + tm * MERGE_TN * 2
    resident = 6 * tm * MERGE_TN * 4 + (2 << 20)
    return pl.pallas_call(
        _branch_kernel,
        grid=(m // tm, n_j),
        in_specs=[
            row_bf, row_bf, row_bf,
            pl.BlockSpec((D_MODEL, MERGE_TN), lambda i, j: (0, GATE_OFF // MERGE_TN + j)),
            pl.BlockSpec((D_MODEL, MERGE_TN), lambda i, j: (0, (GATE_OFF + D_MODEL) // MERGE_TN + j)),
            pl.BlockSpec((2, MERGE_TN), lambda i, j: (0, j)),
            col_w, col_w,
        ],
        out_specs=pl.BlockSpec((tm, MERGE_TN), lambda i, j: (i, j)),
        out_shape=jax.ShapeDtypeStruct((m, D_MODEL), BF16),
        compiler_params=pltpu.CompilerParams(
            dimension_semantics=("parallel", "parallel"),
            vmem_limit_bytes=_vmem_limit(pipelined, resident)),
        name="branch",
    )(h, attn, gm, w_in, w_in, b_gate, w_ba, w_bg)


def _outproj_kernel(x_ref, mod_ref, mg_ref, wo_ref, o_ref, *, gate_row):
    o_ref[...] = x_ref[...] + mod_ref[gate_row:gate_row + 1, :] * _dot(mg_ref[...], wo_ref[...])


def _outproj_call(x, mods, merged, w_o, *, gate_row, tm):
    m = x.shape[0]
    row = pl.BlockSpec((tm, D_MODEL), lambda i: (i, 0))
    pipelined = 2 * tm * D_MODEL * 4 + tm * D_MODEL * 2
    resident = D_MODEL * D_MODEL * 2 + 2 * tm * D_MODEL * 4
    return pl.pallas_call(
        functools.partial(_outproj_kernel, gate_row=gate_row),
        grid=(m // tm,),
        in_specs=[
            row,
            pl.BlockSpec((N_MOD, D_MODEL), lambda i: (0, 0)),
            row,
            pl.BlockSpec((D_MODEL, D_MODEL), lambda i: (0, 0), pipeline_mode=pl.Buffered(1)),
        ],
        out_specs=row,
        out_shape=jax.ShapeDtypeStruct((m, D_MODEL), F32),
        compiler_params=pltpu.CompilerParams(
            dimension_semantics=("parallel",),
            vmem_limit_bytes=_vmem_limit(pipelined, resident)),
        name="outproj",
    )(x, mods, merged, w_o)


def kernel(x, c, ctx, c_ctx, w_mod, b_mod, norm_w, w_ffn1_in, w_ffn1_out, w_ffn2_in, w_ffn2_out, w_in, b_gate,
           q_norm_w, k_norm_w, gmlp_ln_w, gmlp_ln_b, w_spatial, b_spatial, w_branch_attn, w_branch_gmlp, w_out,
           final_norm_w):
    batch, seq, d = x.shape
    assert batch == 1 and d == D_MODEL and seq == SEQ and seq % GRID_W == 0
    assert w_mod.shape[0] == 1 and ctx.shape == (1, CTX_LEN, D_MODEL) and w_in.shape[-1] == IN_W

    x0 = x[0]
    ctx0 = ctx[0]
    ws = w_spatial[0].astype(BF16)
    nw = norm_w[0]

    c_rows = jnp.zeros((MOD_ROWS, D_MODEL), F32).at[0].set(c[0]).at[1].set(c_ctx)
    mods = _mod_call(c_rows, w_mod[0], b_mod).reshape(MOD_ROWS, N_MOD, D_MODEL)
    mx, mc = mods[0], mods[1]

    _, hc, w1a, w1b, w1o = _ffn_call(ctx0, mc, nw[0:1], nw[1:2], w_ffn1_in[0], w_ffn1_in[0], w_ffn1_out[0],
                                     b_col0=D_FF, mod_row=0, post_mod_row=3, tm=CTX_LEN, tf=FFN_CTX_TF,
                                     emit_weights=True)
    x1, hx, wi = _ffn_call(x0, mx, nw[0:1], nw[1:2], w1a, w1b, w1o, b_col0=0, mod_row=0, post_mod_row=3,
                           tm=FFN_TM, tf=FFN1_TF, cast_srcs=(w_in[0],))

    gq, gk = q_norm_w[0][None, :], k_norm_w[0][None, :]
    axis_dim = HEAD_DIM // 2
    inv_freq = ROPE_THETA ** (-jnp.arange(0, axis_dim, 2, dtype=F32) / axis_dim)
    inv_freq_lanes = jnp.tile(jnp.repeat(inv_freq, 2), 2)[None, :]
    qk, vt, wba, wbg, wo = _qkv_call(hx, wi, gq, gk, inv_freq_lanes, first_col=0, n_q_heads=N_Q_HEADS, rope=True,
                                     tm=QKV_TM, cast_srcs=(w_branch_attn[0], w_branch_gmlp[0], w_out[0]))
    ck, vct = _qkv_call(hc, wi, gq, gk, inv_freq_lanes, first_col=K_OFF, n_q_heads=0, rope=False, tm=CTX_LEN)
    attn, w2i, w2o = _attn_call(qk, vt, ck, vct, cast_srcs=(w_ffn2_in[0], w_ffn2_out[0]))

    gm = _gmlp_call(hx, wi, gmlp_ln_w, gmlp_ln_b, ws, b_spatial[0].T, tm=GMLP_TM)
    merged = _branch_call(hx, attn, gm, wi, b_gate[0], wba, wbg, tm=MERGE_TM)
    x2 = _outproj_call(x1, mx, merged, wo, gate_row=5, tm=OUTPROJ_TM)

    (out,) = _ffn_call(x2, mx, nw[2:3], final_norm_w[None, :], w2i, w2i, w2o, b_col0=D_FF, mod_row=6,
                       post_mod_row=None, tm=FFN_TM, tf=FFN2_TF)
    return out[None]
```

```python
import functools
import math

import jax
import jax.numpy as jnp
from jax import lax
from jax.experimental import pallas as pl
from jax.experimental.pallas import tpu as pltpu

D_MODEL = 2048
SEQ = 8192
CTX_LEN = 256
GRID_W = 64
HEAD_DIM = 128
N_Q_HEADS = 16
N_KV_HEADS = 4
Q_PER_KV = N_Q_HEADS // N_KV_HEADS
ROPE_THETA = 10000.0
ATTN_SCALE = HEAD_DIM ** -0.5
GMLP_GROUPS = 16
GMLP_WIDTH = 2048
GROUP_DIM = GMLP_WIDTH // GMLP_GROUPS
CHUNK = 128
D_FF = 5632
MACARON_WEIGHT = 0.5
N_MOD = 9
EPS = 1e-6
LOG2_E = math.log2(math.e)

Q_W = N_Q_HEADS * HEAD_DIM
KV_W = N_KV_HEADS * HEAD_DIM
K_OFF = Q_W
V_OFF = K_OFF + KV_W
U_OFF = V_OFF + KV_W
GV_OFF = U_OFF + GMLP_WIDTH
GATE_OFF = GV_OFF + GMLP_WIDTH
IN_W = GATE_OFF + 2 * D_MODEL

V7X_VMEM_BYTES = 64 * 1024 * 1024
VMEM_CAP_BYTES = V7X_VMEM_BYTES - 6 * 1024 * 1024

BF16_SUBLANES = 16
MOD_ROWS = BF16_SUBLANES
CAST_BLOCK_BYTES = 1024 * 1024
MOD_TN = 1024
FFN_TM = 1024
FFN1_TF = 256
FFN2_TF = 512
FFN_CTX_TF = 512
QKV_TM = 512
ATTN_TQ = 128
GMLP_TM = 512
GMLP_TN = 1024
MERGE_TM = 1024
MERGE_TN = 512
OUTPROJ_TM = 512

F32 = jnp.float32
BF16 = jnp.bfloat16


def _vmem_limit(pipelined_bytes, resident_bytes):
    return int(min(2 * pipelined_bytes + resident_bytes, VMEM_CAP_BYTES))


def _dot(a, b):
    return jnp.dot(a, b, preferred_element_type=F32)


def _rms_factor(x):
    return lax.rsqrt(jnp.mean(x * x, axis=-1, keepdims=True) + EPS)


def _rmsnorm(x, w):
    return x * _rms_factor(x) * w


def _gelu_exact(x):
    return 0.5 * x * (1 + lax.erf(x * (2.0 ** -0.5)))


def _mod_kernel(c_ref, w_ref, b_ref, o_ref):
    sc = jax.nn.silu(c_ref[...])
    o_ref[...] = _dot(sc.astype(BF16), w_ref[...].astype(BF16)) + b_ref[...]


def _mod_call(c_rows, w_mod, b_mod):
    n = w_mod.shape[1]
    return pl.pallas_call(
        _mod_kernel,
        grid=(n // MOD_TN,),
        in_specs=[
            pl.BlockSpec((MOD_ROWS, D_MODEL), lambda j: (0, 0)),
            pl.BlockSpec((D_MODEL, MOD_TN), lambda j: (0, j)),
            pl.BlockSpec((1, MOD_TN), lambda j: (0, j)),
        ],
        out_specs=pl.BlockSpec((MOD_ROWS, MOD_TN), lambda j: (0, j)),
        out_shape=jax.ShapeDtypeStruct((MOD_ROWS, n), F32),
        compiler_params=pltpu.CompilerParams(
            dimension_semantics=("arbitrary",),
            vmem_limit_bytes=_vmem_limit(D_MODEL * MOD_TN * 4, D_MODEL * MOD_TN * 2 + (4 << 20))),
        name="mod",
    )(c_rows, w_mod, b_mod)


def _cast_row_block(rows, cols, n_steps):
    rb = BF16_SUBLANES
    while rows % (2 * rb) == 0 and (2 * rb * cols * 4 <= CAST_BLOCK_BYTES or rows // rb > n_steps):
        rb *= 2
    return rb


def _cast_plan(weights, step_of, n_steps):
    n_blocks, in_specs, out_specs, out_shapes, nbytes = [], [], [], [], 0
    for w in weights:
        rows, cols = w.shape
        rb = _cast_row_block(rows, cols, n_steps)
        nb = rows // rb
        assert nb <= n_steps

        def index(*g, nb=nb):
            return (jnp.minimum(step_of(*g), nb - 1), 0)

        n_blocks.append(nb)
        in_specs.append(pl.BlockSpec((rb, cols), index))
        out_specs.append(pl.BlockSpec((rb, cols), index))
        out_shapes.append(jax.ShapeDtypeStruct((rows, cols), BF16))
        nbytes += rb * cols * 6
    return tuple(n_blocks), in_specs, out_specs, out_shapes, nbytes


def _cast_rows(step, srcs, dsts, n_blocks):
    for src, dst, nb in zip(srcs, dsts, n_blocks):
        @pl.when(step < nb)
        def _():
            dst[...] = src[...].astype(dst.dtype)


def _ffn_kernel(x_ref, mod_ref, nw_ref, pnw_ref, wa_ref, wb_ref, wo_ref, *refs, mod_row, post_mod_row,
                emit_weights, cast_blocks):
    refs = list(refs)
    cast_srcs = [refs.pop(0) for _ in cast_blocks]
    y_ref = refs.pop(0) if post_mod_row is not None else None
    o_ref = refs.pop(0)
    w_outs = [refs.pop(0) for _ in range(3)] if emit_weights else None
    cast_dsts = [refs.pop(0) for _ in cast_blocks]
    h_ref, r_ref = refs
    acc_ref = o_ref if post_mod_row is None else y_ref
    j = pl.program_id(1)
    _cast_rows(pl.program_id(0) * pl.num_programs(1) + j, cast_srcs, cast_dsts, cast_blocks)

    def hidden_chunk_product():
        wa, wb, wo = wa_ref[...], wb_ref[...], wo_ref[...]
        if emit_weights:
            wa, wb, wo = wa.astype(BF16), wb.astype(BF16), wo.astype(BF16)
            for w_out, w in zip(w_outs, (wa, wb, wo)):
                w_out[...] = w
        h = h_ref[...]
        a = _dot(h, wa)
        b = _dot(h, wb)
        act = jax.nn.silu(a) * b
        return _dot(act.astype(BF16), wo)

    @pl.when(j == 0)
    def _():
        shift = mod_ref[mod_row:mod_row + 1, :]
        scale = mod_ref[mod_row + 1:mod_row + 2, :]
        r_ref[...] = _rms_factor(x_ref[...])
        h = x_ref[...] * r_ref[...] * nw_ref[...] * (1 + scale) + shift
        h_ref[...] = h.astype(BF16)
        acc_ref[...] = hidden_chunk_product()

    last = pl.num_programs(1) - 1

    @pl.when(jnp.logical_and(j > 0, j < last))
    def _():
        acc_ref[...] += hidden_chunk_product()

    @pl.when(j == last)
    def _():
        acc_ref[...] += hidden_chunk_product()
        gate = mod_ref[mod_row + 2:mod_row + 3, :]
        y = x_ref[...] + MACARON_WEIGHT * gate * acc_ref[...]
        acc_ref[...] = y
        r_ref[...] = _rms_factor(y)
        yn = acc_ref[...] * r_ref[...] * pnw_ref[...]
        if post_mod_row is None:
            o_ref[...] = yn
        else:
            pshift = mod_ref[post_mod_row:post_mod_row + 1, :]
            pscale = mod_ref[post_mod_row + 1:post_mod_row + 2, :]
            o_ref[...] = (yn * (1 + pscale) + pshift).astype(BF16)


def _ffn_call(x, mods, nw, pnw, wa, wb, wo, *, b_col0, mod_row, post_mod_row, tm, tf, emit_weights=False,
              cast_srcs=()):
    m = x.shape[0]
    n_f = D_FF // tf
    b_tile0 = b_col0 // tf
    n_i = m // tm
    assert not emit_weights or n_i == 1
    w_bytes = 4 if emit_weights else 2
    row_f32 = pl.BlockSpec((tm, D_MODEL), lambda i, j: (i, 0))
    vec = pl.BlockSpec((1, D_MODEL), lambda i, j: (0, 0))
    cast_blocks, cast_in, cast_out, cast_shapes, cast_bytes = _cast_plan(
        cast_srcs, lambda i, j: i * n_f + j, n_i * n_f)
    out_shape, out_specs = [], []
    if post_mod_row is not None:
        out_shape += [jax.ShapeDtypeStruct((m, D_MODEL), F32), jax.ShapeDtypeStruct((m, D_MODEL), BF16)]
        out_specs += [row_f32, pl.BlockSpec((tm, D_MODEL), lambda i, j: (i, 0))]
        out_bytes = tm * D_MODEL * 6
    else:
        out_shape += [jax.ShapeDtypeStruct((m, D_MODEL), F32)]
        out_specs += [row_f32]
        out_bytes = tm * D_MODEL * 4
    if emit_weights:
        out_shape += [jax.ShapeDtypeStruct((D_MODEL, D_FF), BF16)] * 2 + [jax.ShapeDtypeStruct((D_FF, D_MODEL), BF16)]
        out_specs += [pl.BlockSpec((D_MODEL, tf), lambda i, j: (0, j))] * 2
        out_specs += [pl.BlockSpec((tf, D_MODEL), lambda i, j: (j, 0))]
        out_bytes += 3 * D_MODEL * tf * 2
    pipelined = tm * D_MODEL * 4 + out_bytes + 3 * D_MODEL * tf * w_bytes + cast_bytes
    resident = tm * D_MODEL * 2 + 6 * tm * tf * 4 + (2 << 20)
    return pl.pallas_call(
        functools.partial(_ffn_kernel, mod_row=mod_row, post_mod_row=post_mod_row, emit_weights=emit_weights,
                          cast_blocks=cast_blocks),
        grid=(n_i, n_f),
        in_specs=[
            row_f32,
            pl.BlockSpec((N_MOD, D_MODEL), lambda i, j: (0, 0)),
            vec,
            vec,
            pl.BlockSpec((D_MODEL, tf), lambda i, j: (0, j)),
            pl.BlockSpec((D_MODEL, tf), lambda i, j: (0, j + b_tile0)),
            pl.BlockSpec((tf, D_MODEL), lambda i, j: (j, 0)),
        ] + cast_in,
        out_specs=out_specs + cast_out,
        out_shape=out_shape + cast_shapes,
        scratch_shapes=[pltpu.VMEM((tm, D_MODEL), BF16), pltpu.VMEM((tm, 1), F32)],
        compiler_params=pltpu.CompilerParams(
            dimension_semantics=("arbitrary", "arbitrary"),
            vmem_limit_bytes=_vmem_limit(pipelined, resident)),
        name="ffn",
    )(x, mods, nw, pnw, wa, wb, wo, *cast_srcs)


def _rope_tables(tile, tm, inv_freq):
    t = tile * tm + lax.broadcasted_iota(jnp.int32, (tm, HEAD_DIM), 0)
    lane = lax.broadcasted_iota(jnp.int32, (tm, HEAD_DIM), 1)
    row = lax.shift_right_logical(t, jnp.int32(GRID_W.bit_length() - 1))
    col = lax.bitwise_and(t, jnp.int32(GRID_W - 1))
    ang = jnp.where(lane < HEAD_DIM // 2, row, col).astype(F32) * inv_freq
    cos, sin = jnp.cos(ang), jnp.sin(ang)
    even = lax.bitwise_and(lane, jnp.int32(1)) == 0
    return cos, jnp.where(even, -sin, 0.0), jnp.where(even, 0.0, sin)


def _qkv_kernel(h_ref, w_ref, gq_ref, gk_ref, freq_ref, *refs, n_q_heads, n_k_heads, rope, cast_blocks):
    refs = list(refs)
    cast_srcs = [refs.pop(0) for _ in cast_blocks]
    o_ref, vt_ref = refs.pop(0), refs.pop(0)
    cast_dsts = [refs.pop(0) for _ in cast_blocks]
    (z_ref,) = refs
    t = pl.program_id(0)
    _cast_rows(t, cast_srcs, cast_dsts, cast_blocks)
    tm = h_ref.shape[0]

    @pl.when(t == 0)
    def _():
        z_ref[...] = jnp.zeros(z_ref.shape, F32)

    def finish_previous_tile():
        gain_q = gq_ref[...] * (ATTN_SCALE * LOG2_E)
        gain_k = gk_ref[...]
        if rope:
            cos, sa, sb = _rope_tables(jnp.maximum(t - 1, 0), tm, freq_ref[...])
        for head in range(n_q_heads + n_k_heads):
            cols = slice(head * HEAD_DIM, (head + 1) * HEAD_DIM)
            y = _rmsnorm(z_ref[:, cols], gain_q if head < n_q_heads else gain_k)
            if rope:
                y = y * cos + pltpu.roll(y, HEAD_DIM - 1, 1) * sa + pltpu.roll(y, 1, 1) * sb
            o_ref[:, cols] = y.astype(o_ref.dtype)
        vt_ref[0] = z_ref[:, (n_q_heads + n_k_heads) * HEAD_DIM:].T.astype(vt_ref.dtype)

    last = pl.num_programs(0) - 1

    @pl.when(t < last)
    def _():
        finish_previous_tile()
        z_ref[...] = _dot(h_ref[...], w_ref[...])

    @pl.when(t == last)
    def _():
        finish_previous_tile()


def _qkv_call(h, w_in, gq, gk, inv_freq, *, first_col, n_q_heads, rope, tm, cast_srcs=()):
    m = h.shape[0]
    n_i = m // tm
    qk_width = (n_q_heads + N_KV_HEADS) * HEAD_DIM
    width = qk_width + KV_W
    assert first_col + width == U_OFF and first_col % width == 0 and GRID_W & (GRID_W - 1) == 0
    vec = pl.BlockSpec((1, HEAD_DIM), lambda t: (0, 0))
    cast_blocks, cast_in, cast_out, cast_shapes, cast_bytes = _cast_plan(cast_srcs, lambda t: t, n_i + 1)
    pipelined = tm * D_MODEL * 2 + tm * width * 2 + cast_bytes
    resident = D_MODEL * width * 2 + 3 * tm * width * 4 + 4 * tm * HEAD_DIM * 4
    return pl.pallas_call(
        functools.partial(_qkv_kernel, n_q_heads=n_q_heads, n_k_heads=N_KV_HEADS, rope=rope,
                          cast_blocks=cast_blocks),
        grid=(n_i + 1,),
        in_specs=[
            pl.BlockSpec((tm, D_MODEL), lambda t: (jnp.minimum(t, n_i - 1), 0)),
            pl.BlockSpec((D_MODEL, width), lambda t: (0, first_col // width), pipeline_mode=pl.Buffered(1)),
            vec, vec, vec,
        ] + cast_in,
        out_specs=[
            pl.BlockSpec((tm, qk_width), lambda t: (jnp.maximum(t - 1, 0), 0)),
            pl.BlockSpec((1, KV_W, tm), lambda t: (jnp.maximum(t - 1, 0), 0, 0)),
        ] + cast_out,
        out_shape=[
            jax.ShapeDtypeStruct((m, qk_width), BF16),
            jax.ShapeDtypeStruct((n_i, KV_W, tm), BF16),
        ] + cast_shapes,
        scratch_shapes=[pltpu.VMEM((tm, width), F32)],
        compiler_params=pltpu.CompilerParams(
            dimension_semantics=("arbitrary",),
            vmem_limit_bytes=_vmem_limit(pipelined, resident)),
        name="qkv",
    )(h, w_in, gq, gk, inv_freq, *cast_srcs)


def _attn_kernel(q_ref, k_ref, vt_ref, kc_ref, vct_ref, *refs, cast_blocks):
    refs = list(refs)
    cast_srcs = [refs.pop(0) for _ in cast_blocks]
    o_ref = refs.pop(0)
    cast_dsts = [refs.pop(0) for _ in cast_blocks]
    s0_ref, s1_ref, m0_ref, m1_ref = refs
    i = pl.program_id(1)
    _cast_rows(pl.program_id(0) * pl.num_programs(1) + i, cast_srcs, cast_dsts, cast_blocks)
    tq = q_ref.shape[0]
    c_len = kc_ref.shape[0]
    n_chunks = vt_ref.shape[0]
    tk = vt_ref.shape[2]

    @pl.when(jnp.logical_and(pl.program_id(0) == 0, i == 0))
    def _():
        s1_ref[...] = jnp.zeros(s1_ref.shape, F32)
        m1_ref[...] = jnp.zeros(m1_ref.shape, F32)

    def step(s_cur, m_cur, s_prev, m_prev_ref):
        q = jnp.concatenate([q_ref[:, g * HEAD_DIM:(g + 1) * HEAD_DIM] for g in range(Q_PER_KV)], axis=0)
        rows = q.shape[0]
        m_prev = m_prev_ref[...]

        def sublane_groups(a):
            return a.reshape(a.shape[0] // 8, 8, rows)

        def pass1_chunk(k, key_rows, m8):
            s = lax.dot_general(k, q, (((1,), (1,)), ((), ())), preferred_element_type=F32)
            s_cur[key_rows, :] = s
            part = jnp.max(sublane_groups(s), axis=0)
            return part if m8 is None else jnp.maximum(m8, part)

        def pass2_chunk(vt, key_rows, l8, acc):
            p = jnp.exp2(s_prev[key_rows, :] - m_prev)
            part = jnp.sum(sublane_groups(p), axis=0)
            pv = _dot(vt, p.astype(BF16))
            return (part, pv) if l8 is None else (l8 + part, acc + pv)

        ctx_rows = slice(0, c_len)
        m8 = pass1_chunk(kc_ref[...], ctx_rows, None)
        l8, acc = pass2_chunk(vct_ref[0], ctx_rows, None, None)
        for t in range(n_chunks):
            key_rows = slice(c_len + t * tk, c_len + (t + 1) * tk)
            m8 = pass1_chunk(k_ref[t * tk:(t + 1) * tk, :], key_rows, m8)
            l8, acc = pass2_chunk(vt_ref[t], key_rows, l8, acc)

        m_cur[...] = jnp.max(m8, axis=0, keepdims=True)
        o = (acc / jnp.sum(l8, axis=0, keepdims=True)).T
        for g in range(Q_PER_KV):
            o_ref[:, g * HEAD_DIM:(g + 1) * HEAD_DIM] = o[g * tq:(g + 1) * tq].astype(o_ref.dtype)

    @pl.when(i % 2 == 0)
    def _():
        step(s0_ref, m0_ref, s1_ref, m1_ref)

    @pl.when(i % 2 == 1)
    def _():
        step(s1_ref, m1_ref, s0_ref, m0_ref)


def _attn_call(qk, vt, ck, vct, cast_srcs=()):
    s_len = qk.shape[0]
    c_len = ck.shape[0]
    n_chunks, _, tk = vt.shape
    n_tiles = s_len // ATTN_TQ
    gw = Q_PER_KV * HEAD_DIM
    rows = Q_PER_KV * ATTN_TQ
    cast_blocks, cast_in, cast_out, cast_shapes, cast_bytes = _cast_plan(
        cast_srcs, lambda kv, i: kv * (n_tiles + 1) + i, N_KV_HEADS * (n_tiles + 1))
    pipelined = 2 * ATTN_TQ * gw * 2 + 2 * s_len * HEAD_DIM * 2 + 2 * c_len * HEAD_DIM * 2 + cast_bytes
    resident = 2 * (s_len + c_len) * rows * 4 + 2 * tk * rows * 4
    return pl.pallas_call(
        functools.partial(_attn_kernel, cast_blocks=cast_blocks),
        grid=(N_KV_HEADS, n_tiles + 1),
        in_specs=[
            pl.BlockSpec((ATTN_TQ, gw), lambda kv, i: (jnp.minimum(i, n_tiles - 1), kv)),
            pl.BlockSpec((s_len, HEAD_DIM), lambda kv, i: (0, K_OFF // HEAD_DIM + kv)),
            pl.BlockSpec((n_chunks, HEAD_DIM, tk), lambda kv, i: (0, kv, 0)),
            pl.BlockSpec((c_len, HEAD_DIM), lambda kv, i: (0, kv)),
            pl.BlockSpec((1, HEAD_DIM, c_len), lambda kv, i: (0, kv, 0)),
        ] + cast_in,
        out_specs=[pl.BlockSpec((ATTN_TQ, gw), lambda kv, i: (jnp.maximum(i - 1, 0), kv))] + cast_out,
        out_shape=[jax.ShapeDtypeStruct((s_len, Q_W), BF16)] + cast_shapes,
        scratch_shapes=[
            pltpu.VMEM((c_len + s_len, rows), F32),
            pltpu.VMEM((c_len + s_len, rows), F32),
            pltpu.VMEM((1, rows), F32),
            pltpu.VMEM((1, rows), F32),
        ],
        compiler_params=pltpu.CompilerParams(
            dimension_semantics=("arbitrary", "arbitrary"),
            vmem_limit_bytes=_vmem_limit(pipelined, resident)),
        name="attn",
    )(qk, qk, vt, ck, vct, *cast_srcs)


def _gmlp_kernel(h_ref, *refs):
    n_tiles = GMLP_WIDTH // GMLP_TN
    wu_refs, wv_refs = refs[:n_tiles], refs[n_tiles:2 * n_tiles]
    lnw_ref, lnb_ref, ws_ref, bs_ref, o_ref, gu_ref, gv_ref = refs[2 * n_tiles:]
    tm = h_ref.shape[0]

    @pl.when(pl.program_id(0) == 0)
    def _():
        gu_ref[...] = jnp.zeros(gu_ref.shape, F32)
        gv_ref[...] = jnp.zeros(gv_ref.shape, F32)

    tiles = [slice(jj * GMLP_TN, (jj + 1) * GMLP_TN) for jj in range(n_tiles)]

    def finish_previous_tile():
        total = sum(gv_ref[:, tc].sum(axis=-1, keepdims=True) for tc in tiles)
        mu = total / GMLP_WIDTH
        sq = sum(((gv_ref[:, tc] - mu) ** 2).sum(axis=-1, keepdims=True) for tc in tiles)
        rstd = lax.rsqrt(sq / GMLP_WIDTH + EPS)
        groups_per_tile = GMLP_TN // GROUP_DIM
        for jj, tc in enumerate(tiles):
            vn = ((gv_ref[:, tc] - mu) * rstd * lnw_ref[:, tc] + lnb_ref[:, tc]).astype(BF16)
            for gg in range(groups_per_tile):
                g = jj * groups_per_tile + gg
                cols = slice(g * GROUP_DIM, (g + 1) * GROUP_DIM)
                bias = bs_ref[:, g:g + 1]
                for c in range(tm // CHUNK):
                    rows = slice(c * CHUNK, (c + 1) * CHUNK)
                    mixed = _dot(ws_ref[g], vn[rows, gg * GROUP_DIM:(gg + 1) * GROUP_DIM]) + bias
                    o_ref[rows, cols] = (gu_ref[rows, cols] * mixed).astype(o_ref.dtype)

    def project_this_tile():
        h = h_ref[...]
        for tc, wu_ref in zip(tiles, wu_refs):
            gu_ref[:, tc] = _gelu_exact(_dot(h, wu_ref[...]))
        for tc, wv_ref in zip(tiles, wv_refs):
            gv_ref[:, tc] = _gelu_exact(_dot(h, wv_ref[...]))

    last = pl.num_programs(0) - 1

    @pl.when(pl.program_id(0) < last)
    def _():
        finish_previous_tile()
        project_this_tile()

    @pl.when(pl.program_id(0) == last)
    def _():
        finish_previous_tile()


def _gmlp_call(h, w_in, lnw, lnb, w_s, b_s_t, *, tm):
    m = h.shape[0]
    n_i = m // tm
    n_tiles = GMLP_WIDTH // GMLP_TN
    vec = pl.BlockSpec((1, GMLP_WIDTH), lambda t: (0, 0))

    def w_tile(first_col, jj):
        return pl.BlockSpec((D_MODEL, GMLP_TN), lambda t: (0, first_col // GMLP_TN + jj),
                            pipeline_mode=pl.Buffered(1))

    pipelined = (tm * D_MODEL * 2 + tm * GMLP_WIDTH * 2 + GMLP_GROUPS * CHUNK * CHUNK * 2 + CHUNK * 128 * 4)
    resident = 2 * D_MODEL * GMLP_WIDTH * 2 + 2 * tm * GMLP_WIDTH * 4 + 6 * tm * GMLP_TN * 4
    return pl.pallas_call(
        _gmlp_kernel,
        grid=(n_i + 1,),
        in_specs=[pl.BlockSpec((tm, D_MODEL), lambda t: (jnp.minimum(t, n_i - 1), 0))]
        + [w_tile(U_OFF, jj) for jj in range(n_tiles)]
        + [w_tile(GV_OFF, jj) for jj in range(n_tiles)]
        + [
            vec, vec,
            pl.BlockSpec((GMLP_GROUPS, CHUNK, CHUNK), lambda t: (0, 0, 0)),
            pl.BlockSpec((CHUNK, GMLP_GROUPS), lambda t: (0, 0)),
        ],
        out_specs=pl.BlockSpec((tm, GMLP_WIDTH), lambda t: (jnp.maximum(t - 1, 0), 0)),
        out_shape=jax.ShapeDtypeStruct((m, GMLP_WIDTH), BF16),
        scratch_shapes=[pltpu.VMEM((tm, GMLP_WIDTH), F32), pltpu.VMEM((tm, GMLP_WIDTH), F32)],
        compiler_params=pltpu.CompilerParams(
            dimension_semantics=("arbitrary",),
            vmem_limit_bytes=_vmem_limit(pipelined, resident)),
        name="gmlp",
    )(h, *([w_in] * (2 * n_tiles)), lnw, lnb, w_s, b_s_t)


def _branch_kernel(h_ref, attn_ref, gm_ref, wga_ref, wgb_ref, bg_ref, wba_ref, wbg_ref, *refs, cast_blocks):
    refs = list(refs)
    cast_srcs = [refs.pop(0) for _ in cast_blocks]
    o_ref = refs.pop(0)
    cast_dsts = refs
    _cast_rows(pl.program_id(0) * pl.num_programs(1) + pl.program_id(1), cast_srcs, cast_dsts, cast_blocks)
    h = h_ref[...]
    ga = jax.nn.sigmoid(_dot(h, wga_ref[...]) + bg_ref[0:1, :])
    gb = jax.nn.sigmoid(_dot(h, wgb_ref[...]) + bg_ref[1:2, :])
    merged = ga * _dot(attn_ref[...], wba_ref[...]) + gb * _dot(gm_ref[...], wbg_ref[...])
    o_ref[...] = merged.astype(o_ref.dtype)


def _branch_call(h, attn, gm, w_in, b_gate, w_ba, w_bg, *, tm, cast_srcs=()):
    m = h.shape[0]
    n_j = D_MODEL // MERGE_TN
    row_bf = pl.BlockSpec((tm, D_MODEL), lambda i, j: (i, 0))
    col_w = pl.BlockSpec((D_MODEL, MERGE_TN), lambda i, j: (0, j))
    n_steps = (m // tm) * n_j
    cast_blocks, cast_in, cast_out, cast_shapes, cast_bytes = _cast_plan(cast_srcs, lambda i, j: i * n_j + j, n_steps)
    pipelined = 3 * tm * D_MODEL * 2 + 4 * D_MODEL * MERGE_TN * 2 + tm * MERGE_TN * 2 + cast_bytes
    resident = 6 * tm * MERGE_TN * 4 + (2 << 20)
    return pl.pallas_call(
        functools.partial(_branch_kernel, cast_blocks=cast_blocks),
        grid=(m // tm, n_j),
        in_specs=[
            row_bf, row_bf, row_bf,
            pl.BlockSpec((D_MODEL, MERGE_TN), lambda i, j: (0, GATE_OFF // MERGE_TN + j)),
            pl.BlockSpec((D_MODEL, MERGE_TN), lambda i, j: (0, (GATE_OFF + D_MODEL) // MERGE_TN + j)),
            pl.BlockSpec((2, MERGE_TN), lambda i, j: (0, j)),
            col_w, col_w,
        ] + cast_in,
        out_specs=[pl.BlockSpec((tm, MERGE_TN), lambda i, j: (i, j))] + cast_out,
        out_shape=[jax.ShapeDtypeStruct((m, D_MODEL), BF16)] + cast_shapes,
        compiler_params=pltpu.CompilerParams(
            dimension_semantics=("arbitrary", "arbitrary"),
            vmem_limit_bytes=_vmem_limit(pipelined, resident)),
        name="branch",
    )(h, attn, gm, w_in, w_in, b_gate, w_ba, w_bg, *cast_srcs)


def _outproj_kernel(x_ref, mod_ref, mg_ref, wo_ref, o_ref, *, gate_row):
    o_ref[...] = x_ref[...] + mod_ref[gate_row:gate_row + 1, :] * _dot(mg_ref[...], wo_ref[...])


def _outproj_call(x, mods, merged, w_o, *, gate_row, tm):
    m = x.shape[0]
    row = pl.BlockSpec((tm, D_MODEL), lambda i: (i, 0))
    pipelined = 2 * tm * D_MODEL * 4 + tm * D_MODEL * 2
    resident = D_MODEL * D_MODEL * 2 + 2 * tm * D_MODEL * 4
    return pl.pallas_call(
        functools.partial(_outproj_kernel, gate_row=gate_row),
        grid=(m // tm,),
        in_specs=[
            row,
            pl.BlockSpec((N_MOD, D_MODEL), lambda i: (0, 0)),
            row,
            pl.BlockSpec((D_MODEL, D_MODEL), lambda i: (0, 0), pipeline_mode=pl.Buffered(1)),
        ],
        out_specs=row,
        out_shape=jax.ShapeDtypeStruct((m, D_MODEL), F32),
        compiler_params=pltpu.CompilerParams(
            dimension_semantics=("parallel",),
            vmem_limit_bytes=_vmem_limit(pipelined, resident)),
        name="outproj",
    )(x, mods, merged, w_o)


def kernel(x, c, ctx, c_ctx, w_mod, b_mod, norm_w, w_ffn1_in, w_ffn1_out, w_ffn2_in, w_ffn2_out, w_in, b_gate,
           q_norm_w, k_norm_w, gmlp_ln_w, gmlp_ln_b, w_spatial, b_spatial, w_branch_attn, w_branch_gmlp, w_out,
           final_norm_w):
    batch, seq, d = x.shape
    assert batch == 1 and d == D_MODEL and seq == SEQ and seq % GRID_W == 0
    assert w_mod.shape[0] == 1 and ctx.shape == (1, CTX_LEN, D_MODEL) and w_in.shape[-1] == IN_W

    x0 = x[0]
    ctx0 = ctx[0]
    ws = w_spatial[0].astype(BF16)
    nw = norm_w[0]

    c_rows = jnp.zeros((MOD_ROWS, D_MODEL), F32).at[0].set(c[0]).at[1].set(c_ctx)
    mods = _mod_call(c_rows, w_mod[0], b_mod).reshape(MOD_ROWS, N_MOD, D_MODEL)
    mx, mc = mods[0], mods[1]

    _, hc, w1a, w1b, w1o = _ffn_call(ctx0, mc, nw[0:1], nw[1:2], w_ffn1_in[0], w_ffn1_in[0], w_ffn1_out[0],
                                     b_col0=D_FF, mod_row=0, post_mod_row=3, tm=CTX_LEN, tf=FFN_CTX_TF,
                                     emit_weights=True)
    x1, hx, wi = _ffn_call(x0, mx, nw[0:1], nw[1:2], w1a, w1b, w1o, b_col0=0, mod_row=0, post_mod_row=3,
                           tm=FFN_TM, tf=FFN1_TF, cast_srcs=(w_in[0],))

    gq, gk = q_norm_w[0][None, :], k_norm_w[0][None, :]
    axis_dim = HEAD_DIM // 2
    inv_freq = ROPE_THETA ** (-jnp.arange(0, axis_dim, 2, dtype=F32) / axis_dim)
    inv_freq_lanes = jnp.tile(jnp.repeat(inv_freq, 2), 2)[None, :]
    qk, vt, wba, wbg, wo = _qkv_call(hx, wi, gq, gk, inv_freq_lanes, first_col=0, n_q_heads=N_Q_HEADS, rope=True,
                                     tm=QKV_TM, cast_srcs=(w_branch_attn[0], w_branch_gmlp[0], w_out[0]))
    ck, vct = _qkv_call(hc, wi, gq, gk, inv_freq_lanes, first_col=K_OFF, n_q_heads=0, rope=False, tm=CTX_LEN)
    attn, w2i = _attn_call(qk, vt, ck, vct, cast_srcs=(w_ffn2_in[0],))

    gm = _gmlp_call(hx, wi, gmlp_ln_w, gmlp_ln_b, ws, b_spatial[0].T, tm=GMLP_TM)
    merged, w2o = _branch_call(hx, attn, gm, wi, b_gate[0], wba, wbg, tm=MERGE_TM, cast_srcs=(w_ffn2_out[0],))
    x2 = _outproj_call(x1, mx, merged, wo, gate_row=5, tm=OUTPROJ_TM)

    (out,) = _ffn_call(x2, mx, nw[2:3], final_norm_w[None, :], w2i, w2i, w2o, b_col0=D_FF, mod_row=6,
                       post_mod_row=None, tm=FFN_TM, tf=FFN2_TF)
    return out[None]
```
